```python
import math
import jax
import jax.numpy as jnp
from jax import lax
import numpy as np

D_MODEL = 2048
BATCH = 16
SEQ = 256
DEPTH = 2
DEC_BATCH = 4
DEC_SEQ = 1024
PAST_LEN = 512

GRID_W = 64
N_GROUPS = 4
D_GRP = D_MODEL // N_GROUPS
D_MIX = N_GROUPS * D_GRP
H_A = 4
DH_A = D_GRP // H_A
CHUNK = 64
H_B = 4
DV_B = D_GRP // H_B
DK_B = DV_B // 2
H_C = 4
KV_C = 2
G_C = H_C // KV_C
DH_C = D_GRP // H_C
WINDOW = 128
BLOCK = WINDOW
Q_BLOCK = 128
HY_ORDER = 2
HY_POS_DIM = 33
HY_BANDS = (HY_POS_DIM - 1) // 2
HY_FF = 64
HY_FAST = math.log(1e-2) / 0.3
HY_SLOW = math.log(1e-2) / 1.5
D_FF = 5632
ROPE_BASE = 10000.0
EPS = 1e-6
N_A = 4 * D_GRP + 4 * H_A
N_B = 3 * D_GRP
N_C = D_GRP + 2 * KV_C * DH_C
N_D = 3 * D_GRP
N_IN = N_A + N_B + N_C + N_D

kernel_name = 'hybrid_diffusion_parallel_groups_step'


def rms_norm(x, g):
    xf = x.astype(jnp.float32)
    y = xf * lax.rsqrt(jnp.mean(xf * xf, axis=-1, keepdims=True) + EPS)
    return (y * g.astype(jnp.float32)).astype(x.dtype)


def dwconv3(x, w, b):
    ch = x.shape[-1]
    y = lax.conv_general_dilated(x, w[:, None, :].astype(x.dtype), window_strides=(1,),
                                 padding=((1, 1),), dimension_numbers=('NWC', 'WIO', 'NWC'),
                                 feature_group_count=ch)
    return y + b.astype(x.dtype)


def axial_rope(x):
    n_tok, dh = x.shape[1], x.shape[-1]
    rows = n_tok // GRID_W
    t_row = jnp.repeat(jnp.arange(rows, dtype=jnp.float32), GRID_W)
    t_col = jnp.tile(jnp.arange(GRID_W, dtype=jnp.float32), rows)
    n_freq = dh // 4
    inv = ROPE_BASE ** (-jnp.arange(n_freq, dtype=jnp.float32) / n_freq)
    ang = jnp.concatenate([t_row[:, None] * inv, t_col[:, None] * inv], axis=-1)
    ang = ang.reshape((n_tok,) + (1,) * (x.ndim - 3) + (dh // 2,))
    cos, sin = jnp.cos(ang), jnp.sin(ang)
    xf = x.astype(jnp.float32)
    x1, x2 = xf[..., :dh // 2], xf[..., dh // 2:]
    return jnp.concatenate([x1 * cos - x2 * sin, x2 * cos + x1 * sin], axis=-1).astype(x.dtype)


def sink_softmax(s, sink):
    m = jnp.maximum(jnp.max(s, axis=-1, keepdims=True), sink)
    p = jnp.exp(s - m)
    return p / (jnp.sum(p, axis=-1, keepdims=True) + jnp.exp(sink - m))


def mlstm_scan(q, k, v, ig, lf, state):
    bsz, nh, n_tok, dh = q.shape
    nc = n_tok // CHUNK

    def chunks(a):
        return jnp.moveaxis(a.reshape(bsz, nh, nc, CHUNK, *a.shape[3:]), 2, 0)

    tril = jnp.tril(jnp.ones((CHUNK, CHUNK), dtype=bool))

    def step(carry, xs):
        c_mem, n_mem, m_prev = carry
        qc, kc, vc, ic, fc = xs
        b = jnp.cumsum(fc, axis=-1)
        dmat = jnp.where(tril, b[..., :, None] - b[..., None, :] + ic[..., None, :], -jnp.inf)
        inter = b + m_prev[..., None]
        m_t = jnp.maximum(inter, jnp.max(dmat, axis=-1))
        s = jnp.einsum('bhtd,bhsd->bhts', qc, kc) * jnp.exp(dmat - m_t[..., None])
        w_inter = jnp.exp(inter - m_t)
        num = jnp.einsum('bhts,bhsd->bhtd', s, vc) + w_inter[..., None] * jnp.einsum('bhtd,bhde->bhte', qc, c_mem)
        den = jnp.sum(s, axis=-1) + w_inter * jnp.einsum('bhtd,bhd->bht', qc, n_mem)
        h = num / jnp.maximum(jnp.abs(den), jnp.exp(-m_t))[..., None]
        b_last = b[..., -1]
        w_s = b_last[..., None] - b + ic
        m_new = jnp.maximum(b_last + m_prev, jnp.max(w_s, axis=-1))
        decay = jnp.exp(b_last + m_prev - m_new)
        w_k = jnp.exp(w_s - m_new[..., None])
        c_new = decay[..., None, None] * c_mem + jnp.einsum('bhs,bhsd,bhse->bhde', w_k, kc, vc)
        n_new = decay[..., None] * n_mem + jnp.einsum('bhs,bhsd->bhd', w_k, kc)
        return (c_new, n_new, m_new), h

    final, hs = lax.scan(step, state, (chunks(q), chunks(k), chunks(v), chunks(ig), chunks(lf)))
    return jnp.moveaxis(hs, 0, 2).reshape(bsz, nh, n_tok, dh), final


def mlstm_mixer(za, ig_b, fg_b, norm_g, state):
    f32 = jnp.float32
    bsz, n_tok = za.shape[:2]

    def heads(a):
        return a.reshape(bsz, n_tok, H_A, DH_A).transpose(0, 2, 1, 3).astype(f32)

    q = heads(za[..., :D_GRP]) * (DH_A ** -0.5)
    k = heads(za[..., D_GRP:2 * D_GRP])
    v = heads(za[..., 2 * D_GRP:3 * D_GRP])
    o = za[..., 3 * D_GRP:4 * D_GRP]
    gates = za[..., 4 * D_GRP:].astype(f32).reshape(bsz, n_tok, 2, 2, H_A)
    ig = jnp.transpose(gates[:, :, 0] + ig_b.astype(f32), (2, 0, 3, 1))
    lf = jnp.transpose(jax.nn.log_sigmoid(gates[:, :, 1] + fg_b.astype(f32)), (2, 0, 3, 1))
    c0, n0, m0 = (s.astype(f32) for s in state)
    h_f, (cf, nf, mf) = mlstm_scan(q, k, v, ig[0], lf[0], (c0[:, 0], n0[:, 0], m0[:, 0]))
    flip = lambda a: jnp.flip(a, axis=2)
    h_b, (cb, nb, mb) = mlstm_scan(flip(q), flip(k), flip(v), flip(ig[1]), flip(lf[1]),
                                   (c0[:, 1], n0[:, 1], m0[:, 1]))
    h = rms_norm(h_f + flip(h_b), norm_g.reshape(H_A, 1, DH_A))
    h = h.transpose(0, 2, 1, 3).reshape(bsz, n_tok, D_GRP) * jax.nn.sigmoid(o.astype(f32))
    new_state = (jnp.stack([cf, cb], axis=1), jnp.stack([nf, nb], axis=1), jnp.stack([mf, mb], axis=1))
    return h.astype(za.dtype), new_state


def diff_attend(q, k, v, lam):
    bsz, n_tok = q.shape[:2]
    qb = jnp.moveaxis(q.reshape(bsz, n_tok // Q_BLOCK, Q_BLOCK, *q.shape[2:]), 1, 0)
    scale = DK_B ** -0.5

    def blk(qi):
        s = jnp.einsum('bqhmd,bshmd->bhmqs', qi, k).astype(jnp.float32) * scale
        p = jax.nn.softmax(s, axis=-1)
        a = p[:, :, 0] - lam * p[:, :, 1]
        return jnp.einsum('bhqs,bshe->bqhe', a.astype(v.dtype), v)

    out = lax.map(blk, qb)
    return jnp.moveaxis(out, 0, 1).reshape(bsz, n_tok, *v.shape[2:])


def diff_mixer(zb, qn_g, kn_g, lam, out_g, lam_init, ctx_kv):
    bsz, n_tok = zb.shape[:2]
    q = rms_norm(zb[..., :D_GRP].reshape(bsz, n_tok, H_B, 2, DK_B), qn_g)
    k = rms_norm(zb[..., D_GRP:2 * D_GRP].reshape(bsz, n_tok, H_B, 2, DK_B), kn_g)
    v = zb[..., 2 * D_GRP:].reshape(bsz, n_tok, H_B, DV_B)
    if ctx_kv is None:
        keys, vals = k, v
    else:
        q, k = axial_rope(q), axial_rope(k)
        keys = jnp.concatenate([ctx_kv[0].astype(k.dtype), k], axis=1)
        vals = jnp.concatenate([ctx_kv[1].astype(v.dtype), v], axis=1)
    lp = lam.astype(jnp.float32)
    lam_full = jnp.exp(jnp.sum(lp[0] * lp[1])) - jnp.exp(jnp.sum(lp[2] * lp[3])) + lam_init
    out = rms_norm(diff_attend(q, keys, vals, lam_full), out_g) * (1.0 - lam_init)
    return out.reshape(bsz, n_tok, D_GRP), (k, v)


def swa_dense_ctx(q, k, v, sink):
    bsz, n_tok = q.shape[:2]
    qb = jnp.moveaxis(q.reshape(bsz, n_tok // Q_BLOCK, Q_BLOCK, KV_C, G_C, DH_C), 1, 0)
    scale = DH_C ** -0.5

    def blk(qi):
        s = jnp.einsum('bqkgd,bskd->bkgqs', qi, k).astype(jnp.float32) * scale
        p = sink_softmax(s, sink[None, :, :, None, None]).astype(v.dtype)
        return jnp.einsum('bkgqs,bskd->bqkgd', p, v)

    out = lax.map(blk, qb)
    return jnp.moveaxis(out, 0, 1).reshape(bsz, n_tok, D_GRP)


def swa_banded(q, k, v, ck, cv, sink):
    bsz, n_tok = q.shape[:2]
    nb = n_tok // BLOCK

    def band(a):
        ap = jnp.pad(a, ((0, 0), (BLOCK, BLOCK), (0, 0), (0, 0))).reshape(bsz, nb + 2, BLOCK, *a.shape[2:])
        return jnp.concatenate([ap[:, :-2], ap[:, 1:-1], ap[:, 2:]], axis=2)

    kb, vb = band(k), band(v)
    qb = q.reshape(bsz, nb, BLOCK, KV_C, G_C, DH_C)
    scale = DH_C ** -0.5
    s_band = jnp.einsum('bnqkgd,bnskd->bnkgqs', qb, kb).astype(jnp.float32) * scale
    s_ctx = jnp.einsum('bnqkgd,bskd->bnkgqs', qb, ck).astype(jnp.float32) * scale
    start = jnp.arange(nb)[:, None, None] * BLOCK
    qpos = start + jnp.arange(BLOCK)[None, :, None]
    kpos = start - BLOCK + jnp.arange(3 * BLOCK)[None, None, :]
    mask = (jnp.abs(kpos - qpos) <= WINDOW) & (kpos >= 0) & (kpos < n_tok)
    s_band = jnp.where(mask[None, :, None, None], s_band, -jnp.inf)
    p = sink_softmax(jnp.concatenate([s_band, s_ctx], axis=-1),
                     sink[None, None, :, :, None, None]).astype(v.dtype)
    n_band = 3 * BLOCK
    out = (jnp.einsum('bnkgqs,bnskd->bnqkgd', p[..., :n_band], vb)
           + jnp.einsum('bnkgqs,bskd->bnqkgd', p[..., n_band:], cv))
    return out.reshape(bsz, n_tok, D_GRP)


def swa_mixer(zc, qn_g, kn_g, sink, ctx_kv):
    bsz, n_tok = zc.shape[:2]
    q = rms_norm(zc[..., :D_GRP].reshape(bsz, n_tok, KV_C, G_C, DH_C), qn_g)
    k = rms_norm(zc[..., D_GRP:D_GRP + KV_C * DH_C].reshape(bsz, n_tok, KV_C, DH_C), kn_g)
    v = zc[..., D_GRP + KV_C * DH_C:].reshape(bsz, n_tok, KV_C, DH_C)
    sk = sink.astype(jnp.float32).reshape(KV_C, G_C)
    if ctx_kv is None:
        out = swa_dense_ctx(q, k, v, sk)
    else:
        q, k = axial_rope(q), axial_rope(k)
        out = swa_banded(q, k, v, ctx_kv[0].astype(k.dtype), ctx_kv[1].astype(v.dtype), sk)
    return out, (k, v)


def hyena_filters(n_tok, w1, b1, freq, w2, b2, w3, b3):
    f32 = jnp.float32
    t = jnp.linspace(0.0, 1.0, n_tok, dtype=f32)[:, None]
    w = (2.0 * math.pi / n_tok) * jnp.arange(n_tok, dtype=f32)[:, None]
    bands = jnp.linspace(1e-4, HY_BANDS - 1, HY_BANDS, dtype=f32)[None, :]
    feats = jnp.concatenate([t, jnp.cos(bands * w), -jnp.sin(bands * w)], axis=-1)
    fr = freq.astype(f32)
    h = jnp.sin(fr * (feats @ w1.astype(f32) + b1.astype(f32)))
    h = jnp.sin(fr * (h @ w2.astype(f32) + b2.astype(f32)))
    h = h @ w3.astype(f32) + b3.astype(f32)
    rates = jnp.abs(jnp.linspace(HY_FAST, HY_SLOW, D_GRP, dtype=f32))
    window = jnp.exp(-t * rates)
    return h.reshape(n_tok, HY_ORDER, 2, D_GRP) * window[:, None, None, :]


def bidir_long_conv(u, h_fwd, h_bwd, skip):
    n_tok, ch = u.shape[1], u.shape[2]
    g = jnp.concatenate([h_fwd, jnp.zeros((1, ch), h_fwd.dtype), h_bwd[:0:-1]], axis=0)
    uf = u.astype(jnp.float32)
    spec = jnp.fft.rfft(uf, n=2 * n_tok, axis=1) * jnp.fft.rfft(g, n=2 * n_tok, axis=0)[None]
    y = jnp.fft.irfft(spec, n=2 * n_tok, axis=1)[:, :n_tok]
    return (y + uf * skip.astype(jnp.float32)).astype(u.dtype)


def hyena_mixer(zd, conv_w, conv_b, filt, skip):
    u = dwconv3(zd, conv_w, conv_b)
    y, x1, x2 = jnp.split(u, 3, axis=-1)
    for o, gate in enumerate((x1, x2)):
        y = gate * bidir_long_conv(y, filt[:, o, 0], filt[:, o, 1], skip[o])
    return y


def trunk_layer(x, cond, P, l, ctx):
    bsz, n_tok = x.shape[:2]
    mod = (jax.nn.silu(cond) @ P['w_mod'][l] + P['b_mod'][l])[:, None, :]
    sh1, sc1, g1, sh2, sc2, g2 = jnp.split(mod, 6, axis=-1)
    h = rms_norm(x, P['norm1_g'][l]) * (1.0 + sc1) + sh1
    z = h @ P['w_in'][l]
    o1, o2, o3 = N_A, N_A + N_B, N_A + N_B + N_C
    za, zb, zc, zd = z[..., :o1], z[..., o1:o2], z[..., o2:o3], z[..., o3:]
    if ctx is None:
        m_state = (jnp.zeros((bsz, 2, H_A, DH_A, DH_A), jnp.float32),
                   jnp.zeros((bsz, 2, H_A, DH_A), jnp.float32),
                   jnp.zeros((bsz, 2, H_A), jnp.float32))
        diff_ctx, swa_ctx = None, None
    else:
        m_state = (ctx[4], ctx[5], ctx[6])
        diff_ctx, swa_ctx = (ctx[0], ctx[1]), (ctx[2], ctx[3])
    a_out, m_new = mlstm_mixer(za, P['mlstm_ig_b'][l], P['mlstm_fg_b'][l], P['mlstm_norm_g'][l], m_state)
    lam_init = 0.8 - 0.6 * math.exp(-0.3 * l)
    b_out, diff_kv = diff_mixer(zb, P['diff_qn_g'][l], P['diff_kn_g'][l], P['diff_lam'][l],
                                P['diff_out_g'][l], lam_init, diff_ctx)
    c_out, swa_kv = swa_mixer(zc, P['swa_qn_g'][l], P['swa_kn_g'][l], P['swa_sink'][l], swa_ctx)
    filt = hyena_filters(n_tok, P['hy_w1'][l], P['hy_b1'][l], P['hy_freq'][l], P['hy_w2'][l],
                         P['hy_b2'][l], P['hy_w3'][l], P['hy_b3'][l])
    d_out = hyena_mixer(zd, P['hy_conv_w'][l], P['hy_conv_b'][l], filt, P['hy_bias'][l])
    mix = jnp.concatenate([a_out, b_out, c_out, d_out], axis=-1) @ P['w_out'][l]
    x = x + g1 * mix
    h = rms_norm(x, P['norm2_g'][l]) * (1.0 + sc2) + sh2
    u = dwconv3(h @ P['ffn_w_up'][l], P['ffn_conv_w'][l], P['ffn_conv_b'][l])
    ua, ub = jnp.split(u, 2, axis=-1)
    x = x + g2 * ((jax.nn.silu(ua) * ub) @ P['ffn_w_down'][l])
    return x, (diff_kv[0], diff_kv[1], swa_kv[0], swa_kv[1], m_new[0], m_new[1], m_new[2])


def setup_inputs(seed: int = 0) -> dict:
    key = jax.random.key(seed)
    ks = iter(jax.random.split(key, 48))
    f32 = jnp.float32
    nrm = lambda shape, s: jax.random.normal(next(ks), shape, f32) * s
    gain = lambda shape: 1.0 + jax.random.normal(next(ks), shape, f32) * 0.02
    return {
        'x_prompt': nrm((BATCH, SEQ, D_MODEL), 1.0),
        'x_sample': nrm((DEC_BATCH, DEC_SEQ, D_MODEL), 1.0),
        'cache_diff_k': nrm((DEC_BATCH, DEPTH, PAST_LEN, H_B, 2, DK_B), 1.0),
        'cache_diff_v': nrm((DEC_BATCH, DEPTH, PAST_LEN, H_B, DV_B), 1.0),
        'cache_swa_k': nrm((DEC_BATCH, DEPTH, PAST_LEN, KV_C, DH_C), 1.0),
        'cache_swa_v': nrm((DEC_BATCH, DEPTH, PAST_LEN, KV_C, DH_C), 1.0),
        'state_mlstm_C': nrm((DEC_BATCH, DEPTH, 2, H_A, DH_A, DH_A), 0.5),
        'state_mlstm_n': nrm((DEC_BATCH, DEPTH, 2, H_A, DH_A), 0.5),
        'state_mlstm_m': 1.0 + nrm((DEC_BATCH, DEPTH, 2, H_A), 1.0),
        'c': nrm((DEC_BATCH, D_MODEL), 1.0),
        'c_ctx': nrm((D_MODEL,), 1.0),
        'w_mod': nrm((DEPTH, D_MODEL, 6 * D_MODEL), D_MODEL ** -0.5),
        'b_mod': nrm((DEPTH, 6 * D_MODEL), 0.02),
        'norm1_g': gain((DEPTH, D_MODEL)),
        'norm2_g': gain((DEPTH, D_MODEL)),
        'w_in': nrm((DEPTH, D_MODEL, N_IN), D_MODEL ** -0.5),
        'mlstm_ig_b': nrm((DEPTH, 2, H_A), 0.1),
        'mlstm_fg_b': jax.random.uniform(next(ks), (DEPTH, 2, H_A), f32, 3.0, 6.0),
        'mlstm_norm_g': gain((DEPTH, D_GRP)),
        'diff_qn_g': gain((DEPTH, DK_B)),
        'diff_kn_g': gain((DEPTH, DK_B)),
        'diff_lam': nrm((DEPTH, 4, DK_B), 0.1),
        'diff_out_g': gain((DEPTH, DV_B)),
        'swa_qn_g': gain((DEPTH, DH_C)),
        'swa_kn_g': gain((DEPTH, DH_C)),
        'swa_sink': nrm((DEPTH, H_C), 0.5),
        'hy_conv_w': nrm((DEPTH, 3, 3 * D_GRP), 3 ** -0.5),
        'hy_conv_b': nrm((DEPTH, 3 * D_GRP), 0.02),
        'hy_w1': nrm((DEPTH, HY_POS_DIM, HY_FF), HY_POS_DIM ** -0.5),
        'hy_b1': nrm((DEPTH, HY_FF), 0.02),
        'hy_freq': 1.0 + nrm((DEPTH, HY_FF), 0.1),
        'hy_w2': nrm((DEPTH, HY_FF, HY_FF), HY_FF ** -0.5),
        'hy_b2': nrm((DEPTH, HY_FF), 0.02),
        'hy_w3': nrm((DEPTH, HY_FF, HY_ORDER * 2 * D_GRP), 0.02),
        'hy_b3': nrm((DEPTH, HY_ORDER * 2 * D_GRP), 0.002),
        'hy_bias': nrm((DEPTH, HY_ORDER, D_GRP), 0.5),
        'w_out': nrm((DEPTH, D_MIX, D_MODEL), D_MIX ** -0.5),
        'ffn_w_up': nrm((DEPTH, D_MODEL, 2 * D_FF), D_MODEL ** -0.5),
        'ffn_conv_w': nrm((DEPTH, 3, 2 * D_FF), 3 ** -0.5),
        'ffn_conv_b': nrm((DEPTH, 2 * D_FF), 0.02),
        'ffn_w_down': nrm((DEPTH, D_FF, D_MODEL), D_FF ** -0.5),
    }


def reference(x_prompt, x_sample, cache_diff_k, cache_diff_v, cache_swa_k, cache_swa_v,
              state_mlstm_C, state_mlstm_n, state_mlstm_m, c, c_ctx,
              w_mod, b_mod, norm1_g, norm2_g, w_in, mlstm_ig_b, mlstm_fg_b, mlstm_norm_g,
              diff_qn_g, diff_kn_g, diff_lam, diff_out_g, swa_qn_g, swa_kn_g, swa_sink,
              hy_conv_w, hy_conv_b, hy_w1, hy_b1, hy_freq, hy_w2, hy_b2, hy_w3, hy_b3, hy_bias,
              w_out, ffn_w_up, ffn_conv_w, ffn_conv_b, ffn_w_down):
    P = {
        'w_mod': w_mod, 'b_mod': b_mod, 'norm1_g': norm1_g, 'norm2_g': norm2_g, 'w_in': w_in,
        'mlstm_ig_b': mlstm_ig_b, 'mlstm_fg_b': mlstm_fg_b, 'mlstm_norm_g': mlstm_norm_g,
        'diff_qn_g': diff_qn_g, 'diff_kn_g': diff_kn_g, 'diff_lam': diff_lam, 'diff_out_g': diff_out_g,
        'swa_qn_g': swa_qn_g, 'swa_kn_g': swa_kn_g, 'swa_sink': swa_sink,
        'hy_conv_w': hy_conv_w, 'hy_conv_b': hy_conv_b, 'hy_w1': hy_w1, 'hy_b1': hy_b1,
        'hy_freq': hy_freq, 'hy_w2': hy_w2, 'hy_b2': hy_b2, 'hy_w3': hy_w3, 'hy_b3': hy_b3,
        'hy_bias': hy_bias, 'w_out': w_out, 'ffn_w_up': ffn_w_up, 'ffn_conv_w': ffn_conv_w,
        'ffn_conv_b': ffn_conv_b, 'ffn_w_down': ffn_w_down,
    }
    y_prompt = x_prompt
    ctx_out = []
    for l in range(DEPTH):
        y_prompt, st = trunk_layer(y_prompt, c_ctx[None, :], P, l, None)
        ctx_out.append(st)
    new_diff_k = jnp.stack([s[0] for s in ctx_out], axis=1)
    new_diff_v = jnp.stack([s[1] for s in ctx_out], axis=1)
    new_swa_k = jnp.stack([s[2] for s in ctx_out], axis=1)
    new_swa_v = jnp.stack([s[3] for s in ctx_out], axis=1)
    new_mlstm_C = jnp.stack([s[4] for s in ctx_out], axis=1)
    new_mlstm_n = jnp.stack([s[5] for s in ctx_out], axis=1)
    new_mlstm_m = jnp.stack([s[6] for s in ctx_out], axis=1)
    y_sample = x_sample
    for l in range(DEPTH):
        ctx = (cache_diff_k[:, l], cache_diff_v[:, l], cache_swa_k[:, l], cache_swa_v[:, l],
               state_mlstm_C[:, l], state_mlstm_n[:, l], state_mlstm_m[:, l])
        y_sample, _ = trunk_layer(y_sample, c, P, l, ctx)
    return (y_prompt, y_sample, new_diff_k, new_diff_v, new_swa_k, new_swa_v, new_mlstm_C, new_mlstm_n, new_mlstm_m)
```

```python
import functools
import math

import jax
import jax.numpy as jnp
from jax import lax
from jax.experimental import pallas as pl
from jax.experimental.pallas import tpu as pltpu

D_MODEL = 2048
DEPTH = 2
GRID_W = 64
N_GROUPS = 4
D_GRP = D_MODEL // N_GROUPS
H_A = 4
DH_A = D_GRP // H_A
CHUNK = 64
H_B = 4
DV_B = D_GRP // H_B
DK_B = DV_B // 2
H_C = 4
KV_C = 2
G_C = H_C // KV_C
DH_C = D_GRP // H_C
WINDOW = 128
BLOCK = WINDOW
Q_BLOCK = 128
HY_ORDER = 2
HY_POS_DIM = 33
HY_BANDS = (HY_POS_DIM - 1) // 2
HY_FAST = math.log(1e-2) / 0.3
HY_SLOW = math.log(1e-2) / 1.5
D_FF = 5632
ROPE_BASE = 10000.0
EPS = 1e-6
N_A = 4 * D_GRP + 4 * H_A
N_B = 3 * D_GRP
N_C = D_GRP + 2 * KV_C * DH_C
N_D = 3 * D_GRP

V7X_VMEM_LIMIT_BYTES = 56 * 1024 * 1024


def _matmul_kernel(x_ref, w_ref, o_ref):
    x = x_ref[...].astype(jnp.bfloat16)
    w = w_ref[...].astype(jnp.bfloat16)
    o_ref[...] = jnp.dot(x, w, preferred_element_type=jnp.float32)


def _pick(n, cands):
    for c in cands:
        if n % c == 0:
            return c
    return n


def matmul(x, w):
    m, k = x.shape
    n = w.shape[1]
    tm = _pick(m, (512, 256, 128, 64, 32, 16, 8))
    tn = _pick(n, (256, 128)) if k > 4096 else _pick(n, (512, 256, 128))
    return pl.pallas_call(
        _matmul_kernel,
        out_shape=jax.ShapeDtypeStruct((m, n), jnp.float32),
        grid=(n // tn, m // tm),
        in_specs=[pl.BlockSpec((tm, k), lambda j, i: (i, 0)),
                  pl.BlockSpec((k, tn), lambda j, i: (0, j))],
        out_specs=pl.BlockSpec((tm, tn), lambda j, i: (i, j)),
        compiler_params=pltpu.CompilerParams(
            dimension_semantics=("arbitrary", "arbitrary"),
            vmem_limit_bytes=V7X_VMEM_LIMIT_BYTES),
        name="matmul",
    )(x, w)


def _pad_cols(w, mult=128):
    n = w.shape[-1]
    p = (-n) % mult
    if p:
        w = jnp.pad(w, ((0, 0), (0, p)))
    return w


def dense(x, w):
    lead = x.shape[:-1]
    k = x.shape[-1]
    n = w.shape[1]
    x2 = x.reshape(-1, k)
    m = x2.shape[0]
    pm = (-m) % 8
    if pm:
        x2 = jnp.pad(x2, ((0, pm), (0, 0)))
    y = matmul(x2, _pad_cols(w))[:m, :n]
    return y.reshape(*lead, n)


def rms_norm(x, g):
    xf = x.astype(jnp.float32)
    y = xf * lax.rsqrt(jnp.mean(xf * xf, axis=-1, keepdims=True) + EPS)
    return (y * g.astype(jnp.float32)).astype(x.dtype)


def dwconv3(x, w, b):
    xp = jnp.pad(x, ((0, 0), (1, 1), (0, 0)))
    return xp[:, :-2] * w[0] + xp[:, 1:-1] * w[1] + xp[:, 2:] * w[2] + b


def axial_rope(x):
    n_tok, dh = x.shape[1], x.shape[-1]
    rows = n_tok // GRID_W
    t_row = jnp.repeat(jnp.arange(rows, dtype=jnp.float32), GRID_W)
    t_col = jnp.tile(jnp.arange(GRID_W, dtype=jnp.float32), rows)
    n_freq = dh // 4
    inv = ROPE_BASE ** (-jnp.arange(n_freq, dtype=jnp.float32) / n_freq)
    ang = jnp.concatenate([t_row[:, None] * inv, t_col[:, None] * inv], axis=-1)
    ang = ang.reshape((n_tok,) + (1,) * (x.ndim - 3) + (dh // 2,))
    cos, sin = jnp.cos(ang), jnp.sin(ang)
    x1, x2 = x[..., :dh // 2], x[..., dh // 2:]
    return jnp.concatenate([x1 * cos - x2 * sin, x2 * cos + x1 * sin], axis=-1)


def sink_softmax(s, sink):
    m = jnp.maximum(jnp.max(s, axis=-1, keepdims=True), sink)
    p = jnp.exp(s - m)
    return p / (jnp.sum(p, axis=-1, keepdims=True) + jnp.exp(sink - m))


def mlstm_scan(q, k, v, ig, lf, state):
    bsz, nh, n_tok, dh = q.shape
    nc = n_tok // CHUNK

    def chunks(a):
        return jnp.moveaxis(a.reshape(bsz, nh, nc, CHUNK, *a.shape[3:]), 2, 0)

    tril = jnp.tril(jnp.ones((CHUNK, CHUNK), dtype=bool))

    def step(carry, xs):
        c_mem, n_mem, m_prev = carry
        qc, kc, vc, ic, fc = xs
        b = jnp.cumsum(fc, axis=-1)
        dmat = jnp.where(tril, b[..., :, None] - b[..., None, :] + ic[..., None, :], -jnp.inf)
        inter = b + m_prev[..., None]
        m_t = jnp.maximum(inter, jnp.max(dmat, axis=-1))
        s = jnp.einsum('bhtd,bhsd->bhts', qc, kc) * jnp.exp(dmat - m_t[..., None])
        w_inter = jnp.exp(inter - m_t)
        num = jnp.einsum('bhts,bhsd->bhtd', s, vc) + w_inter[..., None] * jnp.einsum('bhtd,bhde->bhte', qc, c_mem)
        den = jnp.sum(s, axis=-1) + w_inter * jnp.einsum('bhtd,bhd->bht', qc, n_mem)
        h = num / jnp.maximum(jnp.abs(den), jnp.exp(-m_t))[..., None]
        b_last = b[..., -1]
        w_s = b_last[..., None] - b + ic
        m_new = jnp.maximum(b_last + m_prev, jnp.max(w_s, axis=-1))
        decay = jnp.exp(b_last + m_prev - m_new)
        w_k = jnp.exp(w_s - m_new[..., None])
        c_new = decay[..., None, None] * c_mem + jnp.einsum('bhs,bhsd,bhse->bhde', w_k, kc, vc)
        n_new = decay[..., None] * n_mem + jnp.einsum('bhs,bhsd->bhd', w_k, kc)
        return (c_new, n_new, m_new), h

    final, hs = lax.scan(step, state, (chunks(q), chunks(k), chunks(v), chunks(ig), chunks(lf)))
    return jnp.moveaxis(hs, 0, 2).reshape(bsz, nh, n_tok, dh), final


def mlstm_mixer(za, ig_b, fg_b, norm_g, state):
    f32 = jnp.float32
    bsz, n_tok = za.shape[:2]

    def heads(a):
        return a.reshape(bsz, n_tok, H_A, DH_A).transpose(0, 2, 1, 3).astype(f32)

    q = heads(za[..., :D_GRP]) * (DH_A ** -0.5)
    k = heads(za[..., D_GRP:2 * D_GRP])
    v = heads(za[..., 2 * D_GRP:3 * D_GRP])
    o = za[..., 3 * D_GRP:4 * D_GRP]
    gates = za[..., 4 * D_GRP:].astype(f32).reshape(bsz, n_tok, 2, 2, H_A)
    ig = jnp.transpose(gates[:, :, 0] + ig_b.astype(f32), (2, 0, 3, 1))
    lf = jnp.transpose(jax.nn.log_sigmoid(gates[:, :, 1] + fg_b.astype(f32)), (2, 0, 3, 1))
    c0, n0, m0 = (s.astype(f32) for s in state)
    h_f, (cf, nf, mf) = mlstm_scan(q, k, v, ig[0], lf[0], (c0[:, 0], n0[:, 0], m0[:, 0]))
    flip = lambda a: jnp.flip(a, axis=2)
    h_b, (cb, nb, mb) = mlstm_scan(flip(q), flip(k), flip(v), flip(ig[1]), flip(lf[1]),
                                   (c0[:, 1], n0[:, 1], m0[:, 1]))
    h = rms_norm(h_f + flip(h_b), norm_g.reshape(H_A, 1, DH_A))
    h = h.transpose(0, 2, 1, 3).reshape(bsz, n_tok, D_GRP) * jax.nn.sigmoid(o.astype(f32))
    new_state = (jnp.stack([cf, cb], axis=1), jnp.stack([nf, nb], axis=1), jnp.stack([mf, mb], axis=1))
    return h.astype(za.dtype), new_state


def diff_attend(q, k, v, lam):
    scale = DK_B ** -0.5
    s = jnp.einsum('bqhmd,bshmd->bhmqs', q, k).astype(jnp.float32) * scale
    p = jax.nn.softmax(s, axis=-1)
    a = p[:, :, 0] - lam * p[:, :, 1]
    return jnp.einsum('bhqs,bshe->bqhe', a.astype(v.dtype), v)


def diff_mixer(zb, qn_g, kn_g, lam, out_g, lam_init, ctx_kv):
    bsz, n_tok = zb.shape[:2]
    q = rms_norm(zb[..., :D_GRP].reshape(bsz, n_tok, H_B, 2, DK_B), qn_g)
    k = rms_norm(zb[..., D_GRP:2 * D_GRP].reshape(bsz, n_tok, H_B, 2, DK_B), kn_g)
    v = zb[..., 2 * D_GRP:].reshape(bsz, n_tok, H_B, DV_B)
    if ctx_kv is None:
        keys, vals = k, v
    else:
        q, k = axial_rope(q), axial_rope(k)
        keys = jnp.concatenate([ctx_kv[0].astype(k.dtype), k], axis=1)
        vals = jnp.concatenate([ctx_kv[1].astype(v.dtype), v], axis=1)
    lp = lam.astype(jnp.float32)
    lam_full = jnp.exp(jnp.sum(lp[0] * lp[1])) - jnp.exp(jnp.sum(lp[2] * lp[3])) + lam_init
    out = rms_norm(diff_attend(q, keys, vals, lam_full), out_g) * (1.0 - lam_init)
    return out.reshape(bsz, n_tok, D_GRP), (k, v)


def swa_dense_ctx(q, k, v, sink):
    bsz, n_tok = q.shape[:2]
    scale = DH_C ** -0.5
    s = jnp.einsum('bqkgd,bskd->bkgqs', q, k).astype(jnp.float32) * scale
    p = sink_softmax(s, sink[None, :, :, None, None]).astype(v.dtype)
    return jnp.einsum('bkgqs,bskd->bqkgd', p, v).reshape(bsz, n_tok, D_GRP)


def swa_banded(q, k, v, ck, cv, sink):
    bsz, n_tok = q.shape[:2]
    nb = n_tok // BLOCK

    def band(a):
        ap = jnp.pad(a, ((0, 0), (BLOCK, BLOCK), (0, 0), (0, 0))).reshape(bsz, nb + 2, BLOCK, *a.shape[2:])
        return jnp.concatenate([ap[:, :-2], ap[:, 1:-1], ap[:, 2:]], axis=2)

    kb, vb = band(k), band(v)
    qb = q.reshape(bsz, nb, BLOCK, KV_C, G_C, DH_C)
    scale = DH_C ** -0.5
    s_band = jnp.einsum('bnqkgd,bnskd->bnkgqs', qb, kb).astype(jnp.float32) * scale
    s_ctx = jnp.einsum('bnqkgd,bskd->bnkgqs', qb, ck).astype(jnp.float32) * scale
    start = jnp.arange(nb)[:, None, None] * BLOCK
    qpos = start + jnp.arange(BLOCK)[None, :, None]
    kpos = start - BLOCK + jnp.arange(3 * BLOCK)[None, None, :]
    mask = (jnp.abs(kpos - qpos) <= WINDOW) & (kpos >= 0) & (kpos < n_tok)
    s_band = jnp.where(mask[None, :, None, None], s_band, -jnp.inf)
    p = sink_softmax(jnp.concatenate([s_band, s_ctx], axis=-1),
                     sink[None, None, :, :, None, None]).astype(v.dtype)
    n_band = 3 * BLOCK
    out = (jnp.einsum('bnkgqs,bnskd->bnqkgd', p[..., :n_band], vb)
           + jnp.einsum('bnkgqs,bskd->bnqkgd', p[..., n_band:], cv))
    return out.reshape(bsz, n_tok, D_GRP)


def swa_mixer(zc, qn_g, kn_g, sink, ctx_kv):
    bsz, n_tok = zc.shape[:2]
    q = rms_norm(zc[..., :D_GRP].reshape(bsz, n_tok, KV_C, G_C, DH_C), qn_g)
    k = rms_norm(zc[..., D_GRP:D_GRP + KV_C * DH_C].reshape(bsz, n_tok, KV_C, DH_C), kn_g)
    v = zc[..., D_GRP + KV_C * DH_C:].reshape(bsz, n_tok, KV_C, DH_C)
    sk = sink.astype(jnp.float32).reshape(KV_C, G_C)
    if ctx_kv is None:
        out = swa_dense_ctx(q, k, v, sk)
    else:
        q, k = axial_rope(q), axial_rope(k)
        out = swa_banded(q, k, v, ctx_kv[0].astype(k.dtype), ctx_kv[1].astype(v.dtype), sk)
    return out, (k, v)


def hyena_filters(n_tok, w1, b1, freq, w2, b2, w3, b3):
    f32 = jnp.float32
    t = jnp.linspace(0.0, 1.0, n_tok, dtype=f32)[:, None]
    w = (2.0 * math.pi / n_tok) * jnp.arange(n_tok, dtype=f32)[:, None]
    bands = jnp.linspace(1e-4, HY_BANDS - 1, HY_BANDS, dtype=f32)[None, :]
    feats = jnp.concatenate([t, jnp.cos(bands * w), -jnp.sin(bands * w)], axis=-1)
    fr = freq.astype(f32)
    h = jnp.sin(fr * (feats @ w1.astype(f32) + b1.astype(f32)))
    h = jnp.sin(fr * (h @ w2.astype(f32) + b2.astype(f32)))
    h = h @ w3.astype(f32) + b3.astype(f32)
    rates = jnp.abs(jnp.linspace(HY_FAST, HY_SLOW, D_GRP, dtype=f32))
    window = jnp.exp(-t * rates)
    return h.reshape(n_tok, HY_ORDER, 2, D_GRP) * window[:, None, None, :]


def bidir_long_conv(u, h_fwd, h_bwd, skip):
    n_tok, ch = u.shape[1], u.shape[2]
    g = jnp.concatenate([h_fwd, jnp.zeros((1, ch), h_fwd.dtype), h_bwd[:0:-1]], axis=0)
    uf = u.astype(jnp.float32)
    spec = jnp.fft.rfft(uf, n=2 * n_tok, axis=1) * jnp.fft.rfft(g, n=2 * n_tok, axis=0)[None]
    y = jnp.fft.irfft(spec, n=2 * n_tok, axis=1)[:, :n_tok]
    return (y + uf * skip.astype(jnp.float32)).astype(u.dtype)


def hyena_mixer(zd, conv_w, conv_b, filt, skip):
    u = dwconv3(zd, conv_w, conv_b)
    y, x1, x2 = jnp.split(u, 3, axis=-1)
    for o, gate in enumerate((x1, x2)):
        y = gate * bidir_long_conv(y, filt[:, o, 0], filt[:, o, 1], skip[o])
    return y


def trunk_layer(x, cond, P, l, ctx):
    bsz, n_tok = x.shape[:2]
    mod = (dense(jax.nn.silu(cond), P['w_mod'][l]) + P['b_mod'][l])[:, None, :]
    sh1, sc1, g1, sh2, sc2, g2 = jnp.split(mod, 6, axis=-1)
    h = rms_norm(x, P['norm1_g'][l]) * (1.0 + sc1) + sh1
    z = dense(h, P['w_in'][l])
    o1, o2, o3 = N_A, N_A + N_B, N_A + N_B + N_C
    za, zb, zc, zd = z[..., :o1], z[..., o1:o2], z[..., o2:o3], z[..., o3:]
    if ctx is None:
        m_state = (jnp.zeros((bsz, 2, H_A, DH_A, DH_A), jnp.float32),
                   jnp.zeros((bsz, 2, H_A, DH_A), jnp.float32),
                   jnp.zeros((bsz, 2, H_A), jnp.float32))
        diff_ctx, swa_ctx = None, None
    else:
        m_state = (ctx[4], ctx[5], ctx[6])
        diff_ctx, swa_ctx = (ctx[0], ctx[1]), (ctx[2], ctx[3])
    a_out, m_new = mlstm_mixer(za, P['mlstm_ig_b'][l], P['mlstm_fg_b'][l], P['mlstm_norm_g'][l], m_state)
    lam_init = 0.8 - 0.6 * math.exp(-0.3 * l)
    b_out, diff_kv = diff_mixer(zb, P['diff_qn_g'][l], P['diff_kn_g'][l], P['diff_lam'][l],
                                P['diff_out_g'][l], lam_init, diff_ctx)
    c_out, swa_kv = swa_mixer(zc, P['swa_qn_g'][l], P['swa_kn_g'][l], P['swa_sink'][l], swa_ctx)
    filt = hyena_filters(n_tok, P['hy_w1'][l], P['hy_b1'][l], P['hy_freq'][l], P['hy_w2'][l],
                         P['hy_b2'][l], P['hy_w3'][l], P['hy_b3'][l])
    d_out = hyena_mixer(zd, P['hy_conv_w'][l], P['hy_conv_b'][l], filt, P['hy_bias'][l])
    mix = dense(jnp.concatenate([a_out, b_out, c_out, d_out], axis=-1), P['w_out'][l])
    x = x + g1 * mix
    h = rms_norm(x, P['norm2_g'][l]) * (1.0 + sc2) + sh2
    u = dwconv3(dense(h, P['ffn_w_up'][l]), P['ffn_conv_w'][l], P['ffn_conv_b'][l])
    ua, ub = jnp.split(u, 2, axis=-1)
    x = x + g2 * dense(jax.nn.silu(ua) * ub, P['ffn_w_down'][l])
    return x, (diff_kv[0], diff_kv[1], swa_kv[0], swa_kv[1], m_new[0], m_new[1], m_new[2])


def kernel(x_prompt, x_sample, cache_diff_k, cache_diff_v, cache_swa_k, cache_swa_v, state_mlstm_C, state_mlstm_n, state_mlstm_m, c, c_ctx, w_mod, b_mod, norm1_g, norm2_g, w_in, mlstm_ig_b, mlstm_fg_b, mlstm_norm_g, diff_qn_g, diff_kn_g, diff_lam, diff_out_g, swa_qn_g, swa_kn_g, swa_sink, hy_conv_w, hy_conv_b, hy_w1, hy_b1, hy_freq, hy_w2, hy_b2, hy_w3, hy_b3, hy_bias, w_out, ffn_w_up, ffn_conv_w, ffn_conv_b, ffn_w_down):
    P = {
        'w_mod': w_mod, 'b_mod': b_mod, 'norm1_g': norm1_g, 'norm2_g': norm2_g, 'w_in': w_in,
        'mlstm_ig_b': mlstm_ig_b, 'mlstm_fg_b': mlstm_fg_b, 'mlstm_norm_g': mlstm_norm_g,
        'diff_qn_g': diff_qn_g, 'diff_kn_g': diff_kn_g, 'diff_lam': diff_lam, 'diff_out_g': diff_out_g,
        'swa_qn_g': swa_qn_g, 'swa_kn_g': swa_kn_g, 'swa_sink': swa_sink,
        'hy_conv_w': hy_conv_w, 'hy_conv_b': hy_conv_b, 'hy_w1': hy_w1, 'hy_b1': hy_b1,
        'hy_freq': hy_freq, 'hy_w2': hy_w2, 'hy_b2': hy_b2, 'hy_w3': hy_w3, 'hy_b3': hy_b3,
        'hy_bias': hy_bias, 'w_out': w_out, 'ffn_w_up': ffn_w_up, 'ffn_conv_w': ffn_conv_w,
        'ffn_conv_b': ffn_conv_b, 'ffn_w_down': ffn_w_down,
    }
    y_prompt = x_prompt
    ctx_out = []
    for l in range(DEPTH):
        y_prompt, st = trunk_layer(y_prompt, c_ctx[None, :], P, l, None)
        ctx_out.append(st)
    outs = [jnp.stack([s[i] for s in ctx_out], axis=1) for i in range(7)]
    y_sample = x_sample
    for l in range(DEPTH):
        ctx = (cache_diff_k[:, l], cache_diff_v[:, l], cache_swa_k[:, l], cache_swa_v[:, l],
               state_mlstm_C[:, l], state_mlstm_n[:, l], state_mlstm_m[:, l])
        y_sample, _ = trunk_layer(y_sample, c, P, l, ctx)
    return (y_prompt, y_sample, *outs)
```

```python
import functools
import math

import jax
import jax.numpy as jnp
from jax import lax
from jax.experimental import pallas as pl
from jax.experimental.pallas import tpu as pltpu

D_MODEL = 2048
DEPTH = 2
GRID_W = 64
D_GRP = 512
H_A = 4
DH_A = 128
H_B = 4
DV_B = 128
DK_B = 64
H_C = 4
KV_C = 2
G_C = 2
DH_C = 128
WINDOW = 128
HY_ORDER = 2
HY_POS_DIM = 33
HY_BANDS = (HY_POS_DIM - 1) // 2
HY_FF = 64
HY_FAST = math.log(1e-2) / 0.3
HY_SLOW = math.log(1e-2) / 1.5
D_FF = 5632
ROPE_BASE = 10000.0
EPS = 1e-6
N_A = 4 * D_GRP + 4 * H_A

LANES = 128
V7X_VMEM_LIMIT_BYTES = 56 * 1024 * 1024
Z_COLS = 12 * D_GRP
ZA_Q, ZA_K, ZA_V, ZA_O = 0, 4, 8, 12
ZB_Q, ZB_K, ZB_V = 16, 20, 24
ZC_Q, ZC_K, ZC_V = 28, 32, 34
ZD_Y, ZD_X1, ZD_X2 = 36, 40, 44

BF16 = jnp.bfloat16
F32 = jnp.float32


def _cparams(sem):
    return pltpu.CompilerParams(dimension_semantics=sem, vmem_limit_bytes=V7X_VMEM_LIMIT_BYTES)


def _dot(a, b):
    return jnp.dot(a.astype(BF16), b.astype(BF16), preferred_element_type=F32)


def _dot_nt(a, b):
    return lax.dot_general(a.astype(BF16), b.astype(BF16), (((1,), (1,)), ((), ())),
                           preferred_element_type=F32)


def _bf16_head(a):
    bits = lax.bitcast_convert_type(a, jnp.int32) & jnp.int32(-65536)
    return lax.bitcast_convert_type(bits, F32)


def _split2(a):
    head = _bf16_head(a)
    return head.astype(BF16), (a - head).astype(BF16)


def _split3(a):
    head = _bf16_head(a)
    rest = a - head
    mid = _bf16_head(rest)
    return head.astype(BF16), mid.astype(BF16), (rest - mid).astype(BF16)


def _dot3(a, b):
    ah, al = _split2(a)
    bh, bl = _split2(b)
    return (jnp.dot(ah, bh, preferred_element_type=F32) + jnp.dot(al, bh, preferred_element_type=F32)
            + jnp.dot(ah, bl, preferred_element_type=F32))


def _iota(shape, dim):
    return lax.broadcasted_iota(jnp.int32, shape, dim)


def _sigmoid(x):
    return 1.0 / (1.0 + jnp.exp(-x))


def _log_sigmoid(x):
    return jnp.minimum(x, 0.0) - jnp.log(1.0 + jnp.exp(-jnp.abs(x)))


def _alias_prev(in_specs, args, prev):
    if prev is None:
        return {}
    in_specs.append(pl.BlockSpec(memory_space=pl.ANY))
    args.append(prev)
    return {len(args) - 1: 0}


def _mod_kernel(c_ref, w_ref, b_ref, o_ref):
    c = c_ref[...]
    s = c * _sigmoid(c)
    o_ref[0] = _dot(s, w_ref[0]) + b_ref[0]


def mod_vectors(cond8, w_mod, b_mod):
    tn = 1536
    n = w_mod.shape[2]
    out = pl.pallas_call(
        _mod_kernel,
        out_shape=jax.ShapeDtypeStruct((DEPTH, 8, n), F32),
        grid=(DEPTH, n // tn),
        in_specs=[pl.BlockSpec((8, D_MODEL), lambda l, j: (0, 0)),
                  pl.BlockSpec((1, D_MODEL, tn), lambda l, j: (l, 0, j)),
                  pl.BlockSpec((1, 1, tn), lambda l, j: (l, 0, j))],
        out_specs=pl.BlockSpec((1, 8, tn), lambda l, j: (l, 0, j)),
        compiler_params=_cparams(("arbitrary", "arbitrary")),
        name="mod_vectors",
    )(cond8, w_mod, b_mod.reshape(DEPTH, 1, n))
    return out.reshape(DEPTH * 8 * 6, 1, D_MODEL)


def _mod_row_index(layer, part, tile_rows, n_ctx_tok, dec_seq):
    def index(i, *_):
        start = i * tile_rows
        row = jnp.where(start < n_ctx_tok, 0, 1 + (start - n_ctx_tok) // dec_seq)
        return (layer * 48 + row * 6 + part, 0, 0)
    return index


def _ada_norm(x, ng, sc, sh):
    y = x * lax.rsqrt(jnp.mean(x * x, axis=-1, keepdims=True) + EPS)
    return y * ng * (1.0 + sc) + sh


def _inproj_kernel(*refs, has_delta):
    if has_delta:
        (x_ref, y_ref, g2_ref, ng_ref, sc_ref, sh_ref, w_ref, wgh_ref, wgl_ref,
         xn_ref, z_ref, gate_ref, hb_ref) = refs
    else:
        (x_ref, ng_ref, sc_ref, sh_ref, w_ref, wgh_ref, wgl_ref,
         z_ref, gate_ref, hb_ref) = refs

    @pl.when(pl.program_id(1) == 0)
    def _():
        x = x_ref[...]
        if has_delta:
            x = x + g2_ref[0] * y_ref[...]
            xn_ref[...] = x
        h = _ada_norm(x, ng_ref[...], sc_ref[0], sh_ref[0])
        hb_ref[...] = h.astype(BF16)
        hh, hl = _split2(h)
        wgh = wgh_ref[...]
        gate_ref[...] = (jnp.dot(hh, wgh, preferred_element_type=F32)
                         + jnp.dot(hl, wgh, preferred_element_type=F32)
                         + jnp.dot(hh, wgl_ref[...], preferred_element_type=F32))

    z_ref[...] = jnp.dot(hb_ref[...], w_ref[...], preferred_element_type=F32)


def in_projection(x, delta, mod, layer, norm_g, w_main, wg_hi, wg_lo, n_ctx_tok, dec_seq):
    m = x.shape[0]
    tm, tn = 512, 1024
    has_delta = delta is not None
    row = pl.BlockSpec((tm, D_MODEL), lambda i, j: (i, 0))
    modspec = lambda part: pl.BlockSpec((1, 1, D_MODEL), _mod_row_index(layer, part, tm, n_ctx_tok, dec_seq))
    vec = pl.BlockSpec((1, D_MODEL), lambda i, j: (0, 0))
    in_specs, args = [row], [x]
    if has_delta:
        g2spec = pl.BlockSpec((1, 1, D_MODEL), _mod_row_index(layer - 1, 5, tm, n_ctx_tok, dec_seq))
        in_specs += [row, g2spec]
        args += [delta, mod]
    in_specs += [vec, modspec(1), modspec(0),
                 pl.BlockSpec((D_MODEL, tn), lambda i, j: (0, j)),
                 pl.BlockSpec((D_MODEL, LANES), lambda i, j: (0, 0)),
                 pl.BlockSpec((D_MODEL, LANES), lambda i, j: (0, 0))]
    args += [norm_g.reshape(1, D_MODEL), mod, mod, w_main, wg_hi, wg_lo]
    out_shape, out_specs = [], []
    if has_delta:
        out_shape.append(jax.ShapeDtypeStruct((m, D_MODEL), F32))
        out_specs.append(row)
    out_shape += [jax.ShapeDtypeStruct((m, Z_COLS), F32), jax.ShapeDtypeStruct((m, LANES), F32)]
    out_specs += [pl.BlockSpec((tm, tn), lambda i, j: (i, j)), pl.BlockSpec((tm, LANES), lambda i, j: (i, 0))]
    outs = pl.pallas_call(
        functools.partial(_inproj_kernel, has_delta=has_delta),
        out_shape=out_shape,
        grid=(m // tm, Z_COLS // tn),
        in_specs=in_specs,
        out_specs=out_specs,
        scratch_shapes=[pltpu.VMEM((tm, D_MODEL), BF16)],
        compiler_params=_cparams(("arbitrary", "arbitrary")),
        name="in_projection",
    )(*args)
    if has_delta:
        return outs
    return (x, *outs)


def _mlstm_kernel(*refs, n_tok, chunk, has_state, want_state, has_prev):
    refs = list(refs)
    q_ref, k_ref, v_ref, o_ref, gate_ref, gb_ref, ng_ref = refs[:7]
    pos = 7
    if has_state:
        c0_ref, n0_ref, m0_ref = refs[pos:pos + 3]
        pos += 3
    pos += int(has_prev)
    out_ref = refs[pos]
    pos += 1
    if want_state:
        cn_ref, nn_ref, mn_ref = refs[pos:pos + 3]
        pos += 3
    hacc_ref = refs[pos]

    head = pl.program_id(1)
    L = chunk
    nc = n_tok // L
    scale = DH_A ** -0.5
    lane_l = _iota((L, LANES), 1)
    sub_l = _iota((LANES, L), 0)
    r_idx = _iota((L, L), 0)
    c_idx = _iota((L, L), 1)

    g = gate_ref[...] + gb_ref[...]
    vals = jnp.where(_iota((n_tok, LANES), 1) < 2 * H_A, g, _log_sigmoid(g))

    for d in range(2):
        col_i = d * H_A + head
        col_f = 2 * H_A + d * H_A + head
        causal = (c_idx <= r_idx) if d == 0 else (c_idx >= r_idx)
        tri = jnp.where(causal, 1.0, 0.0).astype(BF16)
        if has_state:
            c_mem = c0_ref[0, d, 0]
            n_mem = n0_ref[0, d, 0]
            m_prev = m0_ref[0, d, 0]
        else:
            c_mem = jnp.zeros((DH_A, DH_A), F32)
            n_mem = jnp.zeros((1, DH_A), F32)
            m_prev = jnp.zeros((1, 1), F32)
        order = range(nc) if d == 0 else range(nc - 1, -1, -1)
        for ci, c in enumerate(order):
            rows = slice(c * L, (c + 1) * L)
            v_c = vals[rows]
            p0, p1, p2 = _split3(v_c)
            cum =(jnp.dot(tri, p0, preferred_element_type=F32) + jnp.dot(tri, p1, preferred_element_type=F32)
                   + jnp.dot(tri, p2, preferred_element_type=F32))
            v_t = v_c.T
            cum_t = cum.T
            c_col = jnp.sum(jnp.where(lane_l == col_f, cum, 0.0), axis=1, keepdims=True)
            ig_col = jnp.sum(jnp.where(lane_l == col_i, v_c, 0.0), axis=1, keepdims=True)
            c_row = jnp.sum(jnp.where(sub_l == col_f, cum_t, 0.0), axis=0, keepdims=True)
            ig_row = jnp.sum(jnp.where(sub_l == col_i, v_t, 0.0), axis=0, keepdims=True)

            dmat = jnp.where(causal, c_col - c_row + ig_row, -jnp.inf)
            inter = c_col + m_prev
            m_col = jnp.maximum(inter, jnp.max(dmat, axis=1, keepdims=True))
            dexp = jnp.exp(dmat - m_col)
            qs = q_ref[rows, :] * scale
            qb = qs.astype(BF16)
            kf = k_ref[rows, :]
            vb = v_ref[rows, :].astype(BF16)
            p = _dot_nt(qb, kf) * dexp
            w_inter = jnp.exp(inter - m_col)
            num = _dot(p, vb) + w_inter * _dot(qb, c_mem)
            den = (jnp.sum(p, axis=1, keepdims=True)
                   + w_inter * jnp.sum(qs * n_mem, axis=1, keepdims=True))
            hh = num / jnp.maximum(jnp.abs(den), jnp.exp(-m_col))
            if d == 0:
                hacc_ref[rows, :] = hh
            else:
                hacc_ref[rows, :] += hh

            if want_state or ci < nc - 1:
                total = c_col[L - 1:L, :] if d == 0 else c_col[0:1, :]
                w_row = total - c_row + ig_row
                m_new = jnp.maximum(total + m_prev, jnp.max(w_row, axis=1, keepdims=True))
                decay = jnp.exp(total + m_prev - m_new)
                wk = jnp.exp(total - c_col + ig_col - m_new)
                kw = kf * wk
                c_mem = decay * c_mem + _dot(kw.T, vb)
                n_mem = decay * n_mem + jnp.sum(kw, axis=0, keepdims=True)
                m_prev = m_new
        if want_state:
            cn_ref[0, d, 0] = c_mem
            nn_ref[0, d, 0] = n_mem
            mn_ref[0, d, 0] = m_prev

    hs = hacc_ref[...]
    y = hs * lax.rsqrt(jnp.mean(hs * hs, axis=1, keepdims=True) + EPS) * ng_ref[...]
    out_ref[...] = (y * _sigmoid(o_ref[...])).astype(BF16)


def mlstm_mixer(z, gates, gate_bias, norm_g, state, n_seq, n_tok, row_blk0, chunk, want_state, prev=None):
    has_state = state is not None
    col = lambda base: pl.BlockSpec((n_tok, LANES), lambda b, h: (row_blk0 + b, base + h))
    in_specs = [col(ZA_Q), col(ZA_K), col(ZA_V), col(ZA_O),
                pl.BlockSpec((n_tok, LANES), lambda b, h: (row_blk0 + b, 0)),
                pl.BlockSpec((1, LANES), lambda b, h: (0, 0)),
                pl.BlockSpec((1, LANES), lambda b, h: (0, h))]
    args = [z, z, z, z, gates, gate_bias, norm_g.reshape(1, D_GRP)]
    st_specs = [pl.BlockSpec((1, 2, 1, DH_A, DH_A), lambda b, h: (b, 0, h, 0, 0)),
                pl.BlockSpec((1, 2, 1, 1, DH_A), lambda b, h: (b, 0, h, 0, 0)),
                pl.BlockSpec((1, 2, 1, 1, 1), lambda b, h: (b, 0, h, 0, 0))]
    if has_state:
        c0, n0, m0 = state
        in_specs += st_specs
        args += [c0, n0.reshape(n_seq, 2, H_A, 1, DH_A), m0.reshape(n_seq, 2, H_A, 1, 1)]
    aliases = _alias_prev(in_specs, args, prev)
    out_shape = [jax.ShapeDtypeStruct((z.shape[0], D_GRP), BF16)]
    out_specs = [pl.BlockSpec((n_tok, LANES), lambda b, h: (row_blk0 + b, h))]
    if want_state:
        out_shape += [jax.ShapeDtypeStruct((n_seq, 2, H_A, DH_A, DH_A), F32),
                      jax.ShapeDtypeStruct((n_seq, 2, H_A, 1, DH_A), F32),
                      jax.ShapeDtypeStruct((n_seq, 2, H_A, 1, 1), F32)]
        out_specs += st_specs
    outs = pl.pallas_call(
        functools.partial(_mlstm_kernel, n_tok=n_tok, chunk=chunk, has_state=has_state, want_state=want_state,
                          has_prev=prev is not None),
        out_shape=out_shape,
        grid=(n_seq, H_A),
        in_specs=in_specs,
        out_specs=out_specs,
        input_output_aliases=aliases,
        scratch_shapes=[pltpu.VMEM((n_tok, DH_A), F32)],
        compiler_params=_cparams(("arbitrary", "arbitrary")),
        name="mlstm_mixer",
    )(*args)
    return outs


def _rope_tables(n_tok, dh):
    rows = n_tok // GRID_W
    t_row = jnp.repeat(jnp.arange(rows, dtype=F32), GRID_W)
    t_col = jnp.tile(jnp.arange(GRID_W, dtype=F32), rows)
    n_freq = dh // 4
    inv = ROPE_BASE ** (-jnp.arange(n_freq, dtype=F32) / n_freq)
    ang = jnp.concatenate([t_row[:, None] * inv, t_col[:, None] * inv], axis=-1)
    cos, sin = jnp.cos(ang), jnp.sin(ang)
    reps = LANES // dh
    cos_t = jnp.tile(jnp.concatenate([cos, cos], axis=-1), (1, reps))
    sin_t = jnp.tile(jnp.concatenate([-sin, sin], axis=-1), (1, reps))
    return cos_t, sin_t


def _rope(x, cos_t, sin_t, dh):
    half = dh // 2
    if dh == LANES:
        partner = pltpu.roll(x, half, 1)
    else:
        first = (_iota(x.shape, 1) & half) == 0
        partner = jnp.where(first, pltpu.roll(x, LANES - half, 1), pltpu.roll(x, half, 1))
    return x * cos_t + partner * sin_t


def _rms(x, g):
    return x * lax.rsqrt(jnp.mean(x * x, axis=1, keepdims=True) + EPS) * g


def _diff_kernel(*refs, n_tok, q_blk, lam_init, has_ctx, has_prev):
    refs = list(refs)
    q_ref, k_ref, v_ref, qg_ref, kg_ref, lam_ref, og_ref = refs[:7]
    pos = 7
    if has_ctx:
        ck_ref, cv_ref, cos_ref, sin_ref = refs[pos:pos + 4]
        pos += 4
    pos += int(has_prev)
    out_ref = refs[pos]
    pos += 1
    if not has_ctx:
        kout_ref = refs[pos]

    grp0 = _iota((n_tok, LANES), 1) < DK_B

    def group_norm(x, g):
        x2 = x * x
        s0 = jnp.sum(jnp.where(grp0, x2, 0.0), axis=1, keepdims=True)
        s1 = jnp.sum(jnp.where(grp0, 0.0, x2), axis=1, keepdims=True)
        ms = jnp.where(grp0, s0, s1) * (1.0 / DK_B)
        return x * lax.rsqrt(ms + EPS) * g

    qn = group_norm(q_ref[...], qg_ref[...])
    kn = group_norm(k_ref[...], kg_ref[...])
    if has_ctx:
        cos_t, sin_t = cos_ref[...], sin_ref[...]
        qn = _rope(qn, cos_t, sin_t, DK_B)
        kn = _rope(kn, cos_t, sin_t, DK_B)
    else:
        kout_ref[...] = kn
    kb = kn.astype(BF16)
    vb = v_ref[...].astype(BF16)
    if has_ctx:
        ckb = ck_ref[0].astype(BF16)
        cvb = cv_ref[0].astype(BF16)

    lp = lam_ref[...]
    lam = (jnp.exp(jnp.sum(lp[0:1] * lp[1:2], axis=1, keepdims=True))
           - jnp.exp(jnp.sum(lp[2:3] * lp[3:4], axis=1, keepdims=True)) + lam_init)
    scale = DK_B ** -0.5
    grp0_q = _iota((q_blk, LANES), 1) < DK_B

    for qi in range(n_tok // q_blk):
        rows = slice(qi * q_blk, (qi + 1) * q_blk)
        q_rows = qn[rows]
        outs = []
        for comp in range(2):
            keep = grp0_q if comp == 0 else jnp.logical_not(grp0_q)
            qm = jnp.where(keep, q_rows, 0.0).astype(BF16)
            s = _dot_nt(qm, kb) * scale
            mx = jnp.max(s, axis=1, keepdims=True)
            if has_ctx:
                sc = _dot_nt(qm, ckb) * scale
                mx = jnp.maximum(mx, jnp.max(sc, axis=1, keepdims=True))
            p = jnp.exp(s - mx)
            den = jnp.sum(p, axis=1, keepdims=True)
            acc = _dot(p, vb)
            if has_ctx:
                pc = jnp.exp(sc - mx)
                den = den + jnp.sum(pc, axis=1, keepdims=True)
                acc = acc + _dot(pc, cvb)
            outs.append(acc / den)
        a = outs[0] - lam * outs[1]
        out_ref[rows, :] = (_rms(a, og_ref[...]) * (1.0 - lam_init)).astype(BF16)


def diff_mixer(z, qn_g, kn_g, lam, out_g, lam_init, ctx, n_seq, n_tok, row_blk0, prev=None):
    has_ctx = ctx is not None
    col = lambda base: pl.BlockSpec((n_tok, LANES), lambda b, h: (row_blk0 + b, base + h))
    vec = pl.BlockSpec((1, LANES), lambda b, h: (0, 0))
    in_specs = [col(ZB_Q), col(ZB_K), col(ZB_V), vec, vec,
                pl.BlockSpec((4, LANES), lambda b, h: (0, 0)), vec]
    args = [z, z, z, jnp.tile(qn_g, 2).reshape(1, LANES), jnp.tile(kn_g, 2).reshape(1, LANES),
            jnp.pad(lam, ((0, 0), (0, LANES - DK_B))), out_g.reshape(1, LANES)]
    if has_ctx:
        ck, cv = ctx
        s_ctx = ck.shape[1]
        cos_t, sin_t = _rope_tables(n_tok, DK_B)
        tab = pl.BlockSpec((n_tok, LANES), lambda b, h: (0, 0))
        cspec = pl.BlockSpec((1, s_ctx, LANES), lambda b, h: (b, 0, h))
        in_specs += [cspec, cspec, tab, tab]
        args += [ck.reshape(n_seq, s_ctx, H_B * 2 * DK_B), cv.reshape(n_seq, s_ctx, H_B * DV_B), cos_t, sin_t]
    aliases = _alias_prev(in_specs, args, prev)
    out_shape = [jax.ShapeDtypeStruct((z.shape[0], D_GRP), BF16)]
    out_specs = [pl.BlockSpec((n_tok, LANES), lambda b, h: (row_blk0 + b, h))]
    if not has_ctx:
        out_shape.append(jax.ShapeDtypeStruct((n_seq * n_tok, D_GRP), F32))
        out_specs.append(pl.BlockSpec((n_tok, LANES), lambda b, h: (b, h)))
    return pl.pallas_call(
        functools.partial(_diff_kernel, n_tok=n_tok, q_blk=min(n_tok, 256), lam_init=lam_init, has_ctx=has_ctx,
                          has_prev=prev is not None),
        out_shape=out_shape,
        grid=(n_seq, H_B),
        in_specs=in_specs,
        out_specs=out_specs,
        input_output_aliases=aliases,
        compiler_params=_cparams(("arbitrary", "arbitrary")),
        name="diff_mixer",
    )(*args)


def _swa_kernel(*refs, n_tok, q_blk, has_ctx, has_prev):
    refs = list(refs)
    sink_ref, q_ref, k_ref, v_ref, qg_ref, kg_ref = refs[:6]
    pos = 6
    if has_ctx:
        ck_ref, cv_ref, cos_ref, sin_ref = refs[pos:pos + 4]
        pos += 4
    pos += int(has_prev)
    out_ref = refs[pos]
    pos += 1
    if not has_ctx:
        kout_ref = refs[pos]

    kv = pl.program_id(1)
    kn = _rms(k_ref[...], kg_ref[...])
    if has_ctx:
        cos_t, sin_t = cos_ref[...], sin_ref[...]
        kn = _rope(kn, cos_t, sin_t, DH_C)
        ckb = ck_ref[0].astype(BF16)
        cvb = cv_ref[0].astype(BF16)
    else:
        kout_ref[...] = kn
    kb = kn.astype(BF16)
    vb = v_ref[...].astype(BF16)
    scale = DH_C ** -0.5

    for g in range(G_C):
        sink = sink_ref[kv * G_C + g]
        qn = _rms(q_ref[:, g * DH_C:(g + 1) * DH_C], qg_ref[...])
        if has_ctx:
            qn = _rope(qn, cos_t, sin_t, DH_C)
        for qi in range(n_tok // q_blk):
            rows = slice(qi * q_blk, (qi + 1) * q_blk)
            qb = qn[rows].astype(BF16)
            s = _dot_nt(qb, kb) * scale
            if has_ctx:
                qpos = qi * q_blk + _iota((q_blk, n_tok), 0)
                kpos = _iota((q_blk, n_tok), 1)
                s = jnp.where(jnp.abs(kpos - qpos) <= WINDOW, s, -jnp.inf)
            mx = jnp.maximum(jnp.max(s, axis=1, keepdims=True), sink)
            if has_ctx:
                sc = _dot_nt(qb, ckb) * scale
                mx = jnp.maximum(mx, jnp.max(sc, axis=1, keepdims=True))
            p = jnp.exp(s - mx)
            den = jnp.sum(p, axis=1, keepdims=True) + jnp.exp(sink - mx)
            acc = _dot(p, vb)
            if has_ctx:
                pc = jnp.exp(sc - mx)
                den = den + jnp.sum(pc, axis=1, keepdims=True)
                acc = acc + _dot(pc, cvb)
            out_ref[rows, g * DH_C:(g + 1) * DH_C] = (acc / den).astype(BF16)


def swa_mixer(z, qn_g, kn_g, sink, ctx, n_seq, n_tok, row_blk0, prev=None):
    has_ctx = ctx is not None
    vec = pl.BlockSpec((1, LANES), lambda b, kv: (0, 0))
    in_specs = [pl.BlockSpec(memory_space=pltpu.SMEM),
                pl.BlockSpec((n_tok, G_C * DH_C), lambda b, kv: (row_blk0 + b, ZC_Q // G_C + kv)),
                pl.BlockSpec((n_tok, LANES), lambda b, kv: (row_blk0 + b, ZC_K + kv)),
                pl.BlockSpec((n_tok, LANES), lambda b, kv: (row_blk0 + b, ZC_V + kv)),
                vec, vec]
    args = [sink.astype(F32), z, z, z, qn_g.reshape(1, LANES), kn_g.reshape(1, LANES)]
    if has_ctx:
        ck, cv = ctx
        s_ctx = ck.shape[1]
        cos_t, sin_t = _rope_tables(n_tok, DH_C)
        tab = pl.BlockSpec((n_tok, LANES), lambda b, kv: (0, 0))
        cspec = pl.BlockSpec((1, s_ctx, LANES), lambda b, kv: (b, 0, kv))
        in_specs += [cspec, cspec, tab, tab]
        args += [ck.reshape(n_seq, s_ctx, KV_C * DH_C), cv.reshape(n_seq, s_ctx, KV_C * DH_C), cos_t, sin_t]
    aliases = _alias_prev(in_specs, args, prev)
    out_shape = [jax.ShapeDtypeStruct((z.shape[0], D_GRP), BF16)]
    out_specs = [pl.BlockSpec((n_tok, G_C * DH_C), lambda b, kv: (row_blk0 + b, kv))]
    if not has_ctx:
        out_shape.append(jax.ShapeDtypeStruct((n_seq * n_tok, KV_C * DH_C), F32))
        out_specs.append(pl.BlockSpec((n_tok, LANES), lambda b, kv: (b, kv)))
    return pl.pallas_call(
        functools.partial(_swa_kernel, n_tok=n_tok, q_blk=min(n_tok, 256), has_ctx=has_ctx,
                          has_prev=prev is not None),
        out_shape=out_shape,
        grid=(n_seq, KV_C),
        in_specs=in_specs,
        out_specs=out_specs,
        input_output_aliases=aliases,
        compiler_params=_cparams(("arbitrary", "arbitrary")),
        name="swa_mixer",
    )(*args)


def _dft_matrices(n_tok):
    k = jnp.arange(n_tok, dtype=jnp.int32)[:, None]
    s = jnp.arange(n_tok, dtype=jnp.int32)[None, :]
    ang = (math.pi / n_tok) * ((k * s) % (2 * n_tok)).astype(F32)
    cos, sin = jnp.cos(ang), jnp.sin(ang)
    alt_s = jnp.where(s % 2 == 0, 1.0, -1.0).astype(F32)
    fwd = jnp.concatenate([cos, jnp.where(k == 0, alt_s, -sin)], axis=0)
    inv_re = jnp.where(s == 0, 1.0, 2.0 * cos) / (2 * n_tok)
    alt_t = jnp.where(k % 2 == 0, 1.0, -1.0).astype(F32)
    inv_im = jnp.where(s == 0, alt_t, -2.0 * sin) / (2 * n_tok)
    inv = jnp.concatenate([inv_re, inv_im], axis=1)
    return fwd, inv


def _hyena_filter_kernel(feat_ref, w1_ref, b1_ref, fr_ref, w2_ref, b2_ref, w3_ref, b3_ref, win_ref,
                         fh_ref, fl_ref, g_ref, *, n_tok):
    fr = fr_ref[...]
    h = jnp.sin(fr * (_dot3(feat_ref[...], w1_ref[...]) + b1_ref[...]))
    h = jnp.sin(fr * (_dot3(h, w2_ref[...]) + b2_ref[...]))
    f = _dot3(h, w3_ref[...]) + b3_ref[...]
    win = win_ref[...]
    h_fwd = f[:, :D_GRP] * win
    h_bwd = jnp.where(_iota((n_tok, D_GRP), 0) == 0, 0.0, f[:, D_GRP:] * win)
    fh, fl = fh_ref[...], fl_ref[...]

    def spectrum(x):
        xh, xl = _split2(x)
        return (jnp.dot(fh, xh, preferred_element_type=F32) + jnp.dot(fl, xh, preferred_element_type=F32)
                + jnp.dot(fh, xl, preferred_element_type=F32))

    conj_sign = jnp.where(_iota((2 * n_tok, D_GRP), 0) <= n_tok, 1.0, -1.0)
    g_ref[0] = spectrum(h_fwd) + conj_sign * spectrum(h_bwd)


def hyena_filter_spectra(n_tok, w1, b1, freq, w2, b2, w3, b3, fwd_hi, fwd_lo):
    t = jnp.linspace(0.0, 1.0, n_tok, dtype=F32)[:, None]
    w = (2.0 * math.pi / n_tok) * jnp.arange(n_tok, dtype=F32)[:, None]
    bands = jnp.linspace(1e-4, HY_BANDS - 1, HY_BANDS, dtype=F32)[None, :]
    feats = jnp.concatenate([t, jnp.cos(bands * w), -jnp.sin(bands * w)], axis=-1)
    feats = jnp.pad(feats, ((0, 0), (0, LANES - HY_POS_DIM)))
    rates = jnp.abs(jnp.linspace(HY_FAST, HY_SLOW, D_GRP, dtype=F32))
    window = jnp.exp(-t * rates)
    padv = lambda a: jnp.pad(a, (0, LANES - HY_FF)).reshape(1, LANES)
    w1p = jnp.pad(w1, ((0, LANES - HY_POS_DIM), (0, LANES - HY_FF)))
    w2p = jnp.pad(w2, ((0, LANES - HY_FF), (0, LANES - HY_FF)))
    w3p = jnp.pad(w3, ((0, LANES - HY_FF), (0, 0)))
    full = lambda shape: pl.BlockSpec(shape, lambda o: (0,) * len(shape))
    return pl.pallas_call(
        functools.partial(_hyena_filter_kernel, n_tok=n_tok),
        out_shape=jax.ShapeDtypeStruct((HY_ORDER, 2 * n_tok, D_GRP), F32),
        grid=(HY_ORDER,),
        in_specs=[full((n_tok, LANES)), full((LANES, LANES)), full((1, LANES)), full((1, LANES)),
                  full((LANES, LANES)), full((1, LANES)),
                  pl.BlockSpec((LANES, 2 * D_GRP), lambda o: (0, o)),
                  pl.BlockSpec((1, 2 * D_GRP), lambda o: (0, o)),
                  full((n_tok, D_GRP)), full((2 * n_tok, n_tok)), full((2 * n_tok, n_tok))],
        out_specs=pl.BlockSpec((1, 2 * n_tok, D_GRP), lambda o: (o, 0, 0)),
        compiler_params=_cparams(("arbitrary",)),
        name="hyena_filter_spectra",
    )(feats, w1p, padv(b1), padv(freq), w2p, padv(b2), w3p, b3.reshape(1, -1), window, fwd_hi, fwd_lo)


def _hyena_conv_kernel(zy_ref, zx1_ref, zx2_ref, cwy_ref, cw1_ref, cw2_ref, cby_ref, cb1_ref, cb2_ref,
                       skip_ref, g_ref, fwd_ref, inv_ref, *rest, n_tok):
    out_ref = rest[-1]
    width = zy_ref.shape[1]
    row = _iota((n_tok, width), 0)
    first, last = row == 0, row == n_tok - 1

    def dwconv(x_ref, w_ref, b_ref):
        x = x_ref[...]
        w = w_ref[...]
        prev = jnp.where(first, 0.0, pltpu.roll(x, 1, 0))
        nxt = jnp.where(last, 0.0, pltpu.roll(x, n_tok - 1, 0))
        return prev * w[0:1] + x * w[1:2] + nxt * w[2:3] + b_ref[...]

    y = dwconv(zy_ref, cwy_ref, cby_ref)
    gates = (dwconv(zx1_ref, cw1_ref, cb1_ref), dwconv(zx2_ref, cw2_ref, cb2_ref))
    fwd = fwd_ref[...]
    for o in range(HY_ORDER):
        u = jnp.dot(fwd, y.astype(BF16), preferred_element_type=F32)
        a, b = u[:n_tok], u[n_tok:]
        c, d = g_ref[o, :n_tok, :], g_ref[o, n_tok:, :]
        bd = b * d
        y_re = a * c - jnp.where(first, 0.0, bd)
        y_im = jnp.where(first, bd, a * d + b * c)
        conv = (jnp.dot(inv_ref[:, :n_tok], y_re.astype(BF16), preferred_element_type=F32)
                + jnp.dot(inv_ref[:, n_tok:], y_im.astype(BF16), preferred_element_type=F32))
        y = gates[o] * (conv + skip_ref[o:o + 1, :] * y)
    out_ref[...] = y.astype(BF16)


def hyena_mixer(z, conv_w, conv_b, skip, spectra, fwd_hi, inv_hi, n_seq, n_tok, row_blk0, prev=None):
    width = 256
    nw = D_GRP // width
    zcol = lambda base: pl.BlockSpec((n_tok, width), lambda b, j: (row_blk0 + b, base * LANES // width + j))
    wcol = lambda part: pl.BlockSpec((3, width), lambda b, j: (0, part * nw + j))
    bcol = lambda part: pl.BlockSpec((1, width), lambda b, j: (0, part * nw + j))
    cb = conv_b.reshape(1, 3 * D_GRP)
    in_specs = [zcol(ZD_Y), zcol(ZD_X1), zcol(ZD_X2), wcol(0), wcol(1), wcol(2), bcol(0), bcol(1), bcol(2),
                pl.BlockSpec((HY_ORDER, width), lambda b, j: (0, j)),
                pl.BlockSpec((HY_ORDER, 2 * n_tok, width), lambda b, j: (0, 0, j)),
                pl.BlockSpec((2 * n_tok, n_tok), lambda b, j: (0, 0)),
                pl.BlockSpec((n_tok, 2 * n_tok), lambda b, j: (0, 0))]
    args = [z, z, z, conv_w, conv_w, conv_w, cb, cb, cb, skip, spectra, fwd_hi, inv_hi]
    aliases = _alias_prev(in_specs, args, prev)
    return pl.pallas_call(
        functools.partial(_hyena_conv_kernel, n_tok=n_tok),
        out_shape=jax.ShapeDtypeStruct((z.shape[0], D_GRP), BF16),
        grid=(n_seq, nw),
        in_specs=in_specs,
        out_specs=pl.BlockSpec((n_tok, width), lambda b, j: (row_blk0 + b, j)),
        input_output_aliases=aliases,
        compiler_params=_cparams(("arbitrary", "arbitrary")),
        name="hyena_mixer",
    )(*args)


def _outproj_kernel(a_ref, b_ref, c_ref, d_ref, w_ref, x_ref, g1_ref, ng_ref, sc_ref, sh_ref, x1_ref, h2_ref):
    acc = jnp.dot(a_ref[...], w_ref[0:D_GRP, :], preferred_element_type=F32)
    acc += jnp.dot(b_ref[...], w_ref[D_GRP:2 * D_GRP, :], preferred_element_type=F32)
    acc += jnp.dot(c_ref[...], w_ref[2 * D_GRP:3 * D_GRP, :], preferred_element_type=F32)
    acc += jnp.dot(d_ref[...], w_ref[3 * D_GRP:4 * D_GRP, :], preferred_element_type=F32)
    x1 = x_ref[...] + g1_ref[0] * acc
    x1_ref[...] = x1
    h2_ref[...] = _ada_norm(x1, ng_ref[...], sc_ref[0], sh_ref[0]).astype(BF16)


def out_projection(mix_parts, w_out_bf16, x, mod, layer, norm_g, n_ctx_tok, dec_seq):
    m = x.shape[0]
    tm = 256
    modspec = lambda part: pl.BlockSpec((1, 1, D_MODEL), _mod_row_index(layer, part, tm, n_ctx_tok, dec_seq))
    part = pl.BlockSpec((tm, D_GRP), lambda i: (i, 0))
    row = pl.BlockSpec((tm, D_MODEL), lambda i: (i, 0))
    return pl.pallas_call(
        _outproj_kernel,
        out_shape=[jax.ShapeDtypeStruct((m, D_MODEL), F32), jax.ShapeDtypeStruct((m, D_MODEL), BF16)],
        grid=(m // tm,),
        in_specs=[part, part, part, part, pl.BlockSpec((D_MODEL, D_MODEL), lambda i: (0, 0)), row,
                  modspec(2), pl.BlockSpec((1, D_MODEL), lambda i: (0, 0)), modspec(4), modspec(3)],
        out_specs=[row, row],
        compiler_params=_cparams(("arbitrary",)),
        name="out_projection",
    )(*mix_parts, w_out_bf16, x, mod, norm_g.reshape(1, D_MODEL), mod, mod)


def _ffn_kernel(h_ref, wa_ref, wb_ref, cwa_ref, cwb_ref, cba_ref, cbb_ref, wd_ref, y_ref, *,
                tile_rows, n_ctx_tok, ctx_seq, dec_seq):
    i = pl.program_id(0)
    j = pl.program_id(1)
    h = h_ref[...]
    tf = wa_ref.shape[1]
    seq_mask = jnp.where(i * tile_rows < n_ctx_tok, ctx_seq - 1, dec_seq - 1)
    pos = _iota((tile_rows, tf), 0) & seq_mask
    first, last = pos == 0, pos == seq_mask

    def branch(w_ref, cw_ref, cb_ref):
        u = jnp.dot(h, w_ref[...].astype(BF16), preferred_element_type=F32)
        cw = cw_ref[...]
        prev = jnp.where(first, 0.0, pltpu.roll(u, 1, 0))
        nxt = jnp.where(last, 0.0, pltpu.roll(u, tile_rows - 1, 0))
        return prev * cw[0:1] + u * cw[1:2] + nxt * cw[2:3] + cb_ref[...]

    ua = branch(wa_ref, cwa_ref, cba_ref)
    ub = branch(wb_ref, cwb_ref, cbb_ref)
    act = (ua * _sigmoid(ua) * ub).astype(BF16)
    contrib = jnp.dot(act, wd_ref[...].astype(BF16), preferred_element_type=F32)

    @pl.when(j == 0)
    def _():
        y_ref[...] = contrib

    @pl.when(j > 0)
    def _():
        y_ref[...] += contrib


def conv_ffn(h2, w_up, conv_w, conv_b, w_down, n_ctx_tok, ctx_seq, dec_seq):
    m = h2.shape[0]
    tm, tf = 1024, 256
    nf = D_FF // tf
    cb = conv_b.reshape(1, 2 * D_FF)
    return pl.pallas_call(
        functools.partial(_ffn_kernel, tile_rows=tm, n_ctx_tok=n_ctx_tok, ctx_seq=ctx_seq, dec_seq=dec_seq),
        out_shape=jax.ShapeDtypeStruct((m, D_MODEL), F32),
        grid=(m // tm, nf),
        in_specs=[pl.BlockSpec((tm, D_MODEL), lambda i, j: (i, 0)),
                  pl.BlockSpec((D_MODEL, tf), lambda i, j: (0, j)),
                  pl.BlockSpec((D_MODEL, tf), lambda i, j: (0, nf + j)),
                  pl.BlockSpec((3, tf), lambda i, j: (0, j)),
                  pl.BlockSpec((3, tf), lambda i, j: (0, nf + j)),
                  pl.BlockSpec((1, tf), lambda i, j: (0, j)),
                  pl.BlockSpec((1, tf), lambda i, j: (0, nf + j)),
                  pl.BlockSpec((tf, D_MODEL), lambda i, j: (j, 0))],
        out_specs=pl.BlockSpec((tm, D_MODEL), lambda i, j: (i, 0)),
        compiler_params=_cparams(("arbitrary", "arbitrary")),
        name="conv_ffn",
    )(h2, w_up, w_up, conv_w, conv_w, cb, cb, w_down)


def _residual_kernel(x_ref, y_ref, g_ref, o_ref):
    o_ref[...] = x_ref[...] + g_ref[0] * y_ref[...]


def final_residual(x1, y, mod, layer, row0, n_rows, n_ctx_tok, dec_seq):
    tm = 512
    blk0 = row0 // tm

    def mod_index(i):
        start = (blk0 + i) * tm
        row = jnp.where(start < n_ctx_tok, 0, 1 + (start - n_ctx_tok) // dec_seq)
        return (layer * 48 + row * 6 + 5, 0, 0)

    return pl.pallas_call(
        _residual_kernel,
        out_shape=jax.ShapeDtypeStruct((n_rows, D_MODEL), F32),
        grid=(n_rows // tm,),
        in_specs=[pl.BlockSpec((tm, D_MODEL), lambda i: (blk0 + i, 0)),
                  pl.BlockSpec((tm, D_MODEL), lambda i: (blk0 + i, 0)),
                  pl.BlockSpec((1, 1, D_MODEL), mod_index)],
        out_specs=pl.BlockSpec((tm, D_MODEL), lambda i: (i, 0)),
        compiler_params=_cparams(("arbitrary",)),
        name="final_residual",
    )(x1, y, mod)


def kernel(x_prompt, x_sample, cache_diff_k, cache_diff_v, cache_swa_k, cache_swa_v, state_mlstm_C, state_mlstm_n, state_mlstm_m, c, c_ctx, w_mod, b_mod, norm1_g, norm2_g, w_in, mlstm_ig_b, mlstm_fg_b, mlstm_norm_g, diff_qn_g, diff_kn_g, diff_lam, diff_out_g, swa_qn_g, swa_kn_g, swa_sink, hy_conv_w, hy_conv_b, hy_w1, hy_b1, hy_freq, hy_w2, hy_b2, hy_w3, hy_b3, hy_bias, w_out, ffn_w_up, ffn_conv_w, ffn_conv_b, ffn_w_down):
    n_ctx, ctx_seq = x_prompt.shape[:2]
    n_dec, dec_seq = x_sample.shape[:2]
    n_ctx_tok = n_ctx * ctx_seq
    n_dec_tok = n_dec * dec_seq
    dec_blk0 = n_ctx_tok // dec_seq

    x = jnp.concatenate([x_prompt.reshape(n_ctx_tok, D_MODEL), x_sample.reshape(n_dec_tok, D_MODEL)], axis=0)
    cond8 = jnp.concatenate([c_ctx[None, :], c, jnp.zeros((8 - 1 - n_dec, D_MODEL), F32)], axis=0)
    mod = mod_vectors(cond8, w_mod, b_mod)

    dft = {}
    for t in (ctx_seq, dec_seq):
        fwd, inv = _dft_matrices(t)
        dft[t] = dict(fwd=fwd.astype(BF16), inv=inv.astype(BF16), split=_split2(fwd))

    delta = None
    caches = []
    for l in range(DEPTH):
        w_l = w_in[l]
        w_main = jnp.concatenate([w_l[:, :4 * D_GRP], w_l[:, N_A:]], axis=1).astype(BF16)
        wg = jnp.pad(w_l[:, 4 * D_GRP:N_A], ((0, 0), (0, LANES - 4 * H_A)))
        wg_hi, wg_lo = _split2(wg)
        x, z, gates = in_projection(x, delta, mod, l, norm1_g[l], w_main, wg_hi, wg_lo, n_ctx_tok, dec_seq)

        gate_bias = jnp.pad(jnp.concatenate([mlstm_ig_b[l].reshape(-1), mlstm_fg_b[l].reshape(-1)]),
                            (0, LANES - 4 * H_A)).reshape(1, LANES)
        lam_init = 0.8 - 0.6 * math.exp(-0.3 * l)

        a_ctx, c_new, n_new, m_new = mlstm_mixer(z, gates, gate_bias, mlstm_norm_g[l], None,
                                                 n_ctx, ctx_seq, 0, ctx_seq, True)
        b_ctx, diff_k = diff_mixer(z, diff_qn_g[l], diff_kn_g[l], diff_lam[l], diff_out_g[l], lam_init,
                                   None, n_ctx, ctx_seq, 0)
        c_ctx_out, swa_k = swa_mixer(z, swa_qn_g[l], swa_kn_g[l], swa_sink[l], None, n_ctx, ctx_seq, 0)
        spec_ctx = hyena_filter_spectra(ctx_seq, hy_w1[l], hy_b1[l], hy_freq[l], hy_w2[l], hy_b2[l],
                                        hy_w3[l], hy_b3[l], *dft[ctx_seq]["split"])
        d_ctx = hyena_mixer(z, hy_conv_w[l], hy_conv_b[l], hy_bias[l], spec_ctx, dft[ctx_seq]["fwd"],
                            dft[ctx_seq]["inv"], n_ctx, ctx_seq, 0)
        a_mix, = mlstm_mixer(z, gates, gate_bias, mlstm_norm_g[l],
                             (state_mlstm_C[:, l], state_mlstm_n[:, l], state_mlstm_m[:, l]),
                             n_dec, dec_seq, dec_blk0, 512, False, prev=a_ctx)
        b_mix, = diff_mixer(z, diff_qn_g[l], diff_kn_g[l], diff_lam[l], diff_out_g[l], lam_init,
                            (cache_diff_k[:, l], cache_diff_v[:, l]), n_dec, dec_seq, dec_blk0, prev=b_ctx)
        c_mix, = swa_mixer(z, swa_qn_g[l], swa_kn_g[l], swa_sink[l],
                           (cache_swa_k[:, l], cache_swa_v[:, l]), n_dec, dec_seq, dec_blk0, prev=c_ctx_out)
        spec_dec = hyena_filter_spectra(dec_seq, hy_w1[l], hy_b1[l], hy_freq[l], hy_w2[l], hy_b2[l],
                                        hy_w3[l], hy_b3[l], *dft[dec_seq]["split"])
        d_mix = hyena_mixer(z, hy_conv_w[l], hy_conv_b[l], hy_bias[l], spec_dec, dft[dec_seq]["fwd"],
                            dft[dec_seq]["inv"], n_dec, dec_seq, dec_blk0, prev=d_ctx)

        x, h2 = out_projection((a_mix, b_mix, c_mix, d_mix), w_out[l].astype(BF16), x, mod, l, norm2_g[l],
                               n_ctx_tok, dec_seq)
        delta = conv_ffn(h2, ffn_w_up[l], ffn_conv_w[l], ffn_conv_b[l], ffn_w_down[l],
                         n_ctx_tok, ctx_seq, dec_seq)

        zc = z[:n_ctx_tok]
        caches.append((
            diff_k.reshape(n_ctx, ctx_seq, H_B, 2, DK_B),
            zc[:, ZB_V * LANES:(ZB_V + 4) * LANES].reshape(n_ctx, ctx_seq, H_B, DV_B),
            swa_k.reshape(n_ctx, ctx_seq, KV_C, DH_C),
            zc[:, ZC_V * LANES:(ZC_V + 2) * LANES].reshape(n_ctx, ctx_seq, KV_C, DH_C),
            c_new,
            n_new.reshape(n_ctx, 2, H_A, DH_A),
            m_new.reshape(n_ctx, 2, H_A),
        ))

    y_prompt = final_residual(x, delta, mod, DEPTH - 1, 0, n_ctx_tok, n_ctx_tok, dec_seq)
    y_sample = final_residual(x, delta, mod, DEPTH - 1, n_ctx_tok, n_dec_tok, n_ctx_tok, dec_seq)
    outs = [jnp.stack([s[i] for s in caches], axis=1) for i in range(7)]
    return (y_prompt.reshape(n_ctx, ctx_seq, D_MODEL), y_sample.reshape(n_dec, dec_seq, D_MODEL), *outs)
```

```python
import functools
import math

import jax
import jax.numpy as jnp
from jax import lax
from jax.experimental import pallas as pl
from jax.experimental.pallas import tpu as pltpu

D_MODEL = 2048
DEPTH = 2
GRID_W = 64
D_GRP = 512
H_A = 4
DH_A = 128
H_B = 4
DV_B = 128
DK_B = 64
H_C = 4
KV_C = 2
G_C = 2
DH_C = 128
WINDOW = 128
HY_ORDER = 2
HY_POS_DIM = 33
HY_BANDS = (HY_POS_DIM - 1) // 2
HY_FF = 64
HY_FAST = math.log(1e-2) / 0.3
HY_SLOW = math.log(1e-2) / 1.5
D_FF = 5632
ROPE_BASE = 10000.0
EPS = 1e-6
N_A = 4 * D_GRP + 4 * H_A

LANES = 128
V7X_VMEM_LIMIT_BYTES = 56 * 1024 * 1024
Z_COLS = 12 * D_GRP
ZA_Q, ZA_K, ZA_V, ZA_O = 0, 4, 8, 12
ZB_Q, ZB_K, ZB_V = 16, 20, 24
ZC_Q, ZC_K, ZC_V = 28, 32, 34
ZD_Y, ZD_X1, ZD_X2 = 36, 40, 44

BF16 = jnp.bfloat16
F32 = jnp.float32


def _cparams(sem):
    return pltpu.CompilerParams(dimension_semantics=sem, vmem_limit_bytes=V7X_VMEM_LIMIT_BYTES)


def _dot(a, b):
    return jnp.dot(a.astype(BF16), b.astype(BF16), preferred_element_type=F32)


def _dot_nt(a, b):
    return lax.dot_general(a.astype(BF16), b.astype(BF16), (((1,), (1,)), ((), ())),
                           preferred_element_type=F32)


def _bf16_head(a):
    bits = lax.bitcast_convert_type(a, jnp.int32) & jnp.int32(-65536)
    return lax.bitcast_convert_type(bits, F32)


def _split2(a):
    head = _bf16_head(a)
    return head.astype(BF16), (a - head).astype(BF16)


def _split3(a):
    head = _bf16_head(a)
    rest = a - head
    mid = _bf16_head(rest)
    return head.astype(BF16), mid.astype(BF16), (rest - mid).astype(BF16)


def _dot3(a, b):
    ah, al = _split2(a)
    bh, bl = _split2(b)
    return (jnp.dot(ah, bh, preferred_element_type=F32) + jnp.dot(al, bh, preferred_element_type=F32)
            + jnp.dot(ah, bl, preferred_element_type=F32))


def _iota(shape, dim):
    return lax.broadcasted_iota(jnp.int32, shape, dim)


def _sigmoid(x):
    return 1.0 / (1.0 + jnp.exp(-x))


def _log_sigmoid(x):
    return jnp.minimum(x, 0.0) - jnp.log(1.0 + jnp.exp(-jnp.abs(x)))


def _alias_prev(in_specs, args, prev):
    if prev is None:
        return {}
    in_specs.append(pl.BlockSpec(memory_space=pl.ANY))
    args.append(prev)
    return {len(args) - 1: 0}


def _mod_kernel(c_ref, w_ref, b_ref, o_ref):
    k = pl.program_id(1)
    c = c_ref[...]
    part = _dot(c * _sigmoid(c), w_ref[0])

    @pl.when(k == 0)
    def _():
        o_ref[0] = part + b_ref[0]

    @pl.when(k > 0)
    def _():
        o_ref[0] += part


def mod_vectors(cond8, w_mod, b_mod):
    tk = 256
    n = w_mod.shape[2]
    out = pl.pallas_call(
        _mod_kernel,
        out_shape=jax.ShapeDtypeStruct((DEPTH, 8, n), F32),
        grid=(DEPTH, D_MODEL // tk),
        in_specs=[pl.BlockSpec((8, tk), lambda l, k: (0, k)),
                  pl.BlockSpec((1, tk, n), lambda l, k: (l, k, 0)),
                  pl.BlockSpec((1, 1, n), lambda l, k: (l, 0, 0))],
        out_specs=pl.BlockSpec((1, 8, n), lambda l, k: (l, 0, 0)),
        compiler_params=_cparams(("arbitrary", "arbitrary")),
        name="mod_vectors",
    )(cond8, w_mod, b_mod.reshape(DEPTH, 1, n))
    return out.reshape(DEPTH * 8 * 6, 1, D_MODEL)


def _mod_row_index(layer, part, tile_rows, n_ctx_tok, dec_seq):
    def index(i, *_):
        start = i * tile_rows
        row = jnp.where(start < n_ctx_tok, 0, 1 + (start - n_ctx_tok) // dec_seq)
        return (layer * 48 + row * 6 + part, 0, 0)
    return index


def _ada_norm(x, ng, sc, sh):
    y = x * lax.rsqrt(jnp.mean(x * x, axis=-1, keepdims=True) + EPS)
    return y * ng * (1.0 + sc) + sh


def _pack_in_kernel(a_ref, b_ref, w_ref, gh_ref, gl_ref):
    c = pl.program_id(1)
    n_aligned = 4 * D_GRP // a_ref.shape[2]
    shift = LANES - 4 * H_A
    lane = _iota((D_MODEL, LANES), 1)

    @pl.when(c < n_aligned)
    def _():
        w_ref[0] = a_ref[0].astype(BF16)

    @pl.when(c >= n_aligned)
    def _():
        a = a_ref[0]
        pieces = [a[:, k * LANES:(k + 1) * LANES] for k in range(a.shape[1] // LANES)] + [b_ref[0]]
        rolled = [pltpu.roll(p, shift, 1) for p in pieces]
        out = [jnp.where(lane < shift, rolled[k], rolled[k + 1]) for k in range(len(pieces) - 1)]
        w_ref[0] = jnp.concatenate(out, axis=1).astype(BF16)

    @pl.when(c == n_aligned)
    def _():
        hi, lo = _split2(jnp.where(lane < 4 * H_A, a_ref[0, :, 0:LANES], 0.0))
        gh_ref[0] = hi
        gl_ref[0] = lo


def pack_in_weights(w_in):
    tc = 512
    n_blk = Z_COLS // tc
    gate = pl.BlockSpec((1, D_MODEL, LANES), lambda l, c: (l, 0, 0))
    return pl.pallas_call(
        _pack_in_kernel,
        out_shape=[jax.ShapeDtypeStruct((DEPTH, D_MODEL, Z_COLS), BF16),
                   jax.ShapeDtypeStruct((DEPTH, D_MODEL, LANES), BF16),
                   jax.ShapeDtypeStruct((DEPTH, D_MODEL, LANES), BF16)],
        grid=(DEPTH, n_blk),
        in_specs=[pl.BlockSpec((1, D_MODEL, tc), lambda l, c: (l, 0, c)),
                  pl.BlockSpec((1, D_MODEL, LANES), lambda l, c: (l, 0, (c + 1) * (tc // LANES)))],
        out_specs=[pl.BlockSpec((1, D_MODEL, tc), lambda l, c: (l, 0, c)), gate, gate],
        compiler_params=_cparams(("arbitrary", "arbitrary")),
        name="pack_in_weights",
    )(w_in, w_in)


def _inproj_kernel(*refs, has_delta, n_ctx_blk):
    if has_delta:
        (x_ref, y_ref, g2_ref, ng_ref, sc_ref, sh_ref, w_ref, wgh_ref, wgl_ref,
         xn_ref, z_ref, gate_ref, hb_ref) = refs
    else:
        (xc_ref, xd_ref, ng_ref, sc_ref, sh_ref, w_ref, wgh_ref, wgl_ref,
         xn_ref, z_ref, gate_ref, hb_ref) = refs

    i = pl.program_id(0)

    @pl.when(pl.program_id(1) == 0)
    def _():
        if has_delta:
            x = x_ref[...] + g2_ref[0] * y_ref[...]
        else:
            x = jnp.where(i < n_ctx_blk, xc_ref[...], xd_ref[...])
        xn_ref[...] = x
        h = _ada_norm(x, ng_ref[...], sc_ref[0], sh_ref[0])
        hb_ref[...] = h.astype(BF16)
        hh, hl = _split2(h)
        wgh = wgh_ref[0]
        gate_ref[...] = (jnp.dot(hh, wgh, preferred_element_type=F32)
                         + jnp.dot(hl, wgh, preferred_element_type=F32)
                         + jnp.dot(hh, wgl_ref[0], preferred_element_type=F32))

    z_ref[...] = jnp.dot(hb_ref[...], w_ref[0], preferred_element_type=F32)


def in_projection(x, delta, mod, layer, norm_g, w_pack, wg_hi, wg_lo, n_ctx_tok, dec_seq):
    tm, tn = 512, 1024
    has_delta = delta is not None
    n_ctx_blk = n_ctx_tok // tm
    row = pl.BlockSpec((tm, D_MODEL), lambda i, j: (i, 0))
    modspec = lambda part: pl.BlockSpec((1, 1, D_MODEL), _mod_row_index(layer, part, tm, n_ctx_tok, dec_seq))
    vec = pl.BlockSpec((1, D_MODEL), lambda i, j: (0, 0))
    if has_delta:
        m = x.shape[0]
        g2spec = pl.BlockSpec((1, 1, D_MODEL), _mod_row_index(layer - 1, 5, tm, n_ctx_tok, dec_seq))
        in_specs = [row, row, g2spec]
        args = [x, delta, mod]
    else:
        x_ctx, x_dec = x
        m = x_ctx.shape[0] + x_dec.shape[0]
        in_specs = [pl.BlockSpec((tm, D_MODEL), lambda i, j: (jnp.minimum(i, n_ctx_blk - 1), 0)),
                    pl.BlockSpec((tm, D_MODEL), lambda i, j: (jnp.maximum(i - n_ctx_blk, 0), 0))]
        args = [x_ctx, x_dec]
    gate_w = pl.BlockSpec((1, D_MODEL, LANES), lambda i, j: (layer, 0, 0))
    in_specs += [vec, modspec(1), modspec(0),
                 pl.BlockSpec((1, D_MODEL, tn), lambda i, j: (layer, 0, j)), gate_w, gate_w]
    args += [norm_g.reshape(1, D_MODEL), mod, mod, w_pack, wg_hi, wg_lo]
    return pl.pallas_call(
        functools.partial(_inproj_kernel, has_delta=has_delta, n_ctx_blk=n_ctx_blk),
        out_shape=[jax.ShapeDtypeStruct((m, D_MODEL), F32), jax.ShapeDtypeStruct((m, Z_COLS), F32),
                   jax.ShapeDtypeStruct((m, LANES), F32)],
        grid=(m // tm, Z_COLS // tn),
        in_specs=in_specs,
        out_specs=[row, pl.BlockSpec((tm, tn), lambda i, j: (i, j)), pl.BlockSpec((tm, LANES), lambda i, j: (i, 0))],
        scratch_shapes=[pltpu.VMEM((tm, D_MODEL), BF16)],
        compiler_params=_cparams(("arbitrary", "arbitrary")),
        name="in_projection",
    )(*args)


def _mlstm_kernel(*refs, n_tok, chunk, has_state, want_state, has_prev):
    refs = list(refs)
    q_ref, k_ref, v_ref, o_ref, gate_ref, gb_ref, ng_ref = refs[:7]
    pos = 7
    if has_state:
        c0_ref, n0_ref, m0_ref = refs[pos:pos + 3]
        pos += 3
    pos += int(has_prev)
    out_ref = refs[pos]
    pos += 1
    if want_state:
        cn_ref, nn_ref, mn_ref = refs[pos:pos + 3]
        pos += 3
    hacc_ref = refs[pos]

    L = chunk
    nc = n_tok // L
    scale = DH_A ** -0.5
    r_idx = _iota((L, L), 0)
    c_idx = _iota((L, L), 1)

    g = gate_ref[...] + gb_ref[...]
    vals = jnp.where(_iota((n_tok, LANES), 1) < 2 * H_A, g, _log_sigmoid(g))

    for d in range(2):
        causal = (c_idx <= r_idx) if d == 0 else (c_idx >= r_idx)
        tri = jnp.where(causal, 1.0, 0.0).astype(BF16)
        if has_state:
            c_mem = [c0_ref[0, d, h] for h in range(H_A)]
            n_mem = [n0_ref[0, d, h] for h in range(H_A)]
            m_prev = [m0_ref[0, d, h] for h in range(H_A)]
        else:
            c_mem = [jnp.zeros((DH_A, DH_A), F32)] * H_A
            n_mem = [jnp.zeros((1, DH_A), F32)] * H_A
            m_prev = [jnp.zeros((1, 1), F32)] * H_A
        order = range(nc) if d == 0 else range(nc - 1, -1, -1)
        for ci, c in enumerate(order):
            rows = slice(c * L, (c + 1) * L)
            v_c = vals[rows]
            p0, p1, p2 = _split3(v_c)
            cum = (jnp.dot(tri, p0, preferred_element_type=F32) + jnp.dot(tri, p1, preferred_element_type=F32)
                   + jnp.dot(tri, p2, preferred_element_type=F32))
            v_t = v_c.T
            cum_t = cum.T
            for h in range(H_A):
                col_i = d * H_A + h
                col_f = 2 * H_A + d * H_A + h
                lanes = slice(h * DH_A, (h + 1) * DH_A)
                c_col = cum[:, col_f:col_f + 1]
                ig_col = v_c[:, col_i:col_i + 1]
                c_row = cum_t[col_f:col_f + 1, :]
                ig_row = v_t[col_i:col_i + 1, :]

                dmat = jnp.where(causal, c_col - c_row + ig_row, -jnp.inf)
                inter = c_col + m_prev[h]
                m_col = jnp.maximum(inter, jnp.max(dmat, axis=1, keepdims=True))
                dexp = jnp.exp(dmat - m_col)
                qs = q_ref[rows, lanes] * scale
                qb = qs.astype(BF16)
                kf = k_ref[rows, lanes]
                vb = v_ref[rows, lanes].astype(BF16)
                p = _dot_nt(qb, kf) * dexp
                w_inter = jnp.exp(inter - m_col)
                num = _dot(p, vb) + w_inter * _dot(qb, c_mem[h])
                den = (jnp.sum(p, axis=1, keepdims=True)
                       + w_inter * jnp.sum(qs * n_mem[h], axis=1, keepdims=True))
                hh = num / jnp.maximum(jnp.abs(den), jnp.exp(-m_col))
                if d == 0:
                    hacc_ref[rows, lanes] = hh
                else:
                    hacc_ref[rows, lanes] += hh

                if want_state or ci < nc - 1:
                    total = c_col[L - 1:L, :] if d == 0 else c_col[0:1, :]
                    w_row = total - c_row + ig_row
                    m_new = jnp.maximum(total + m_prev[h], jnp.max(w_row, axis=1, keepdims=True))
                    decay = jnp.exp(total + m_prev[h] - m_new)
                    wk = jnp.exp(total - c_col + ig_col - m_new)
                    kw = kf * wk
                    c_mem[h] = decay * c_mem[h] + _dot(kw.T, vb)
                    n_mem[h] = decay * n_mem[h] + jnp.sum(kw, axis=0, keepdims=True)
                    m_prev[h] = m_new
        if want_state:
            for h in range(H_A):
                cn_ref[0, d, h] = c_mem[h]
                nn_ref[0, d, h] = n_mem[h]
                mn_ref[0, d, h] = m_prev[h]

    for h in range(H_A):
        lanes = slice(h * DH_A, (h + 1) * DH_A)
        hs = hacc_ref[:, lanes]
        y = hs * lax.rsqrt(jnp.mean(hs * hs, axis=1, keepdims=True) + EPS) * ng_ref[:, lanes]
        out_ref[:, lanes] = (y * _sigmoid(o_ref[:, lanes])).astype(BF16)


def mlstm_mixer(z, gates, gate_bias, norm_g, state, n_seq, n_tok, row_blk0, chunk, want_state, prev=None):
    has_state = state is not None
    col = lambda base: pl.BlockSpec((n_tok, D_GRP), lambda b: (row_blk0 + b, base * LANES // D_GRP))
    in_specs = [col(ZA_Q), col(ZA_K), col(ZA_V), col(ZA_O),
                pl.BlockSpec((n_tok, LANES), lambda b: (row_blk0 + b, 0)),
                pl.BlockSpec((1, LANES), lambda b: (0, 0)),
                pl.BlockSpec((1, D_GRP), lambda b: (0, 0))]
    args = [z, z, z, z, gates, gate_bias, norm_g.reshape(1, D_GRP)]
    st_specs = [pl.BlockSpec((1, 2, H_A, DH_A, DH_A), lambda b: (b, 0, 0, 0, 0)),
                pl.BlockSpec((1, 2, H_A, 1, DH_A), lambda b: (b, 0, 0, 0, 0)),
                pl.BlockSpec((1, 2, H_A, 1, 1), lambda b: (b, 0, 0, 0, 0))]
    if has_state:
        c0, n0, m0 = state
        in_specs += st_specs
        args += [c0, n0.reshape(n_seq, 2, H_A, 1, DH_A), m0.reshape(n_seq, 2, H_A, 1, 1)]
    aliases = _alias_prev(in_specs, args, prev)
    out_shape = [jax.ShapeDtypeStruct((z.shape[0], D_GRP), BF16)]
    out_specs = [pl.BlockSpec((n_tok, D_GRP), lambda b: (row_blk0 + b, 0))]
    if want_state:
        out_shape += [jax.ShapeDtypeStruct((n_seq, 2, H_A, DH_A, DH_A), F32),
                      jax.ShapeDtypeStruct((n_seq, 2, H_A, 1, DH_A), F32),
                      jax.ShapeDtypeStruct((n_seq, 2, H_A, 1, 1), F32)]
        out_specs += st_specs
    outs = pl.pallas_call(
        functools.partial(_mlstm_kernel, n_tok=n_tok, chunk=chunk, has_state=has_state, want_state=want_state,
                          has_prev=prev is not None),
        out_shape=out_shape,
        grid=(n_seq,),
        in_specs=in_specs,
        out_specs=out_specs,
        input_output_aliases=aliases,
        scratch_shapes=[pltpu.VMEM((n_tok, D_GRP), F32)],
        compiler_params=_cparams(("arbitrary",)),
        name="mlstm_mixer",
    )(*args)
    return outs


def _rope_tables(n_tok, dh):
    rows = n_tok // GRID_W
    t_row = jnp.repeat(jnp.arange(rows, dtype=F32), GRID_W)
    t_col = jnp.tile(jnp.arange(GRID_W, dtype=F32), rows)
    n_freq = dh // 4
    inv = ROPE_BASE ** (-jnp.arange(n_freq, dtype=F32) / n_freq)
    ang = jnp.concatenate([t_row[:, None] * inv, t_col[:, None] * inv], axis=-1)
    cos, sin = jnp.cos(ang), jnp.sin(ang)
    reps = LANES // dh
    cos_t = jnp.tile(jnp.concatenate([cos, cos], axis=-1), (1, reps))
    sin_t = jnp.tile(jnp.concatenate([-sin, sin], axis=-1), (1, reps))
    return cos_t, sin_t


def _rope(x, cos_t, sin_t, dh):
    half = dh // 2
    if dh == LANES:
        partner = pltpu.roll(x, half, 1)
    else:
        first = (_iota(x.shape, 1) & half) == 0
        partner = jnp.where(first, pltpu.roll(x, LANES - half, 1), pltpu.roll(x, half, 1))
    return x * cos_t + partner * sin_t


def _rms(x, g):
    return x * lax.rsqrt(jnp.mean(x * x, axis=1, keepdims=True) + EPS) * g


def _diff_kernel(*refs, n_tok, q_blk, lam_init, has_ctx, has_prev):
    refs = list(refs)
    q_ref, k_ref, v_ref, qg_ref, kg_ref, lam_ref, og_ref = refs[:7]
    pos = 7
    if has_ctx:
        ck_ref, cv_ref, cos_ref, sin_ref = refs[pos:pos + 4]
        pos += 4
    pos += int(has_prev)
    out_ref = refs[pos]
    pos += 1
    if not has_ctx:
        kout_ref = refs[pos]

    grp0 = _iota((n_tok, LANES), 1) < DK_B

    def group_norm(x, g):
        x2 = x * x
        s0 = jnp.sum(jnp.where(grp0, x2, 0.0), axis=1, keepdims=True)
        s1 = jnp.sum(jnp.where(grp0, 0.0, x2), axis=1, keepdims=True)
        ms = jnp.where(grp0, s0, s1) * (1.0 / DK_B)
        return x * lax.rsqrt(ms + EPS) * g

    qn = group_norm(q_ref[...], qg_ref[...])
    kn = group_norm(k_ref[...], kg_ref[...])
    if has_ctx:
        cos_t, sin_t = cos_ref[...], sin_ref[...]
        qn = _rope(qn, cos_t, sin_t, DK_B)
        kn = _rope(kn, cos_t, sin_t, DK_B)
    else:
        kout_ref[...] = kn
    kb = kn.astype(BF16)
    vb = v_ref[...].astype(BF16)
    if has_ctx:
        ckb = ck_ref[0].astype(BF16)
        cvb = cv_ref[0].astype(BF16)

    lp = lam_ref[...]
    lam = (jnp.exp(jnp.sum(lp[0:1] * lp[1:2], axis=1, keepdims=True))
           - jnp.exp(jnp.sum(lp[2:3] * lp[3:4], axis=1, keepdims=True)) + lam_init)
    scale = DK_B ** -0.5
    grp0_q = _iota((q_blk, LANES), 1) < DK_B

    for qi in range(n_tok // q_blk):
        rows = slice(qi * q_blk, (qi + 1) * q_blk)
        q_rows = qn[rows]
        outs = []
        for comp in range(2):
            keep = grp0_q if comp == 0 else jnp.logical_not(grp0_q)
            qm = jnp.where(keep, q_rows, 0.0).astype(BF16)
            s = _dot_nt(qm, kb) * scale
            mx = jnp.max(s, axis=1, keepdims=True)
            if has_ctx:
                sc = _dot_nt(qm, ckb) * scale
                mx = jnp.maximum(mx, jnp.max(sc, axis=1, keepdims=True))
            p = jnp.exp(s - mx)
            den = jnp.sum(p, axis=1, keepdims=True)
            acc = _dot(p, vb)
            if has_ctx:
                pc = jnp.exp(sc - mx)
                den = den + jnp.sum(pc, axis=1, keepdims=True)
                acc = acc + _dot(pc, cvb)
            outs.append(acc / den)
        a = outs[0] - lam * outs[1]
        out_ref[rows, :] = (_rms(a, og_ref[...]) * (1.0 - lam_init)).astype(BF16)


def diff_mixer(z, qn_g, kn_g, lam, out_g, lam_init, ctx, n_seq, n_tok, row_blk0, prev=None):
    has_ctx = ctx is not None
    col = lambda base: pl.BlockSpec((n_tok, LANES), lambda b, h: (row_blk0 + b, base + h))
    vec = pl.BlockSpec((1, LANES), lambda b, h: (0, 0))
    in_specs = [col(ZB_Q), col(ZB_K), col(ZB_V), vec, vec,
                pl.BlockSpec((4, LANES), lambda b, h: (0, 0)), vec]
    args = [z, z, z, jnp.tile(qn_g, 2).reshape(1, LANES), jnp.tile(kn_g, 2).reshape(1, LANES),
            jnp.pad(lam, ((0, 0), (0, LANES - DK_B))), out_g.reshape(1, LANES)]
    if has_ctx:
        ck, cv = ctx
        s_ctx = ck.shape[1]
        cos_t, sin_t = _rope_tables(n_tok, DK_B)
        tab = pl.BlockSpec((n_tok, LANES), lambda b, h: (0, 0))
        cspec = pl.BlockSpec((1, s_ctx, LANES), lambda b, h: (b, 0, h))
        in_specs += [cspec, cspec, tab, tab]
        args += [ck.reshape(n_seq, s_ctx, H_B * 2 * DK_B), cv.reshape(n_seq, s_ctx, H_B * DV_B), cos_t, sin_t]
    aliases = _alias_prev(in_specs, args, prev)
    out_shape = [jax.ShapeDtypeStruct((z.shape[0], D_GRP), BF16)]
    out_specs = [pl.BlockSpec((n_tok, LANES), lambda b, h: (row_blk0 + b, h))]
    if not has_ctx:
        out_shape.append(jax.ShapeDtypeStruct((n_seq * n_tok, D_GRP), F32))
        out_specs.append(pl.BlockSpec((n_tok, LANES), lambda b, h: (b, h)))
    return pl.pallas_call(
        functools.partial(_diff_kernel, n_tok=n_tok, q_blk=min(n_tok, 256), lam_init=lam_init, has_ctx=has_ctx,
                          has_prev=prev is not None),
        out_shape=out_shape,
        grid=(n_seq, H_B),
        in_specs=in_specs,
        out_specs=out_specs,
        input_output_aliases=aliases,
        compiler_params=_cparams(("arbitrary", "arbitrary")),
        name="diff_mixer",
    )(*args)


def _swa_kernel(*refs, n_tok, q_blk, has_ctx, has_prev):
    refs = list(refs)
    sink_ref, q_ref, k_ref, v_ref, qg_ref, kg_ref = refs[:6]
    pos = 6
    if has_ctx:
        ck_ref, cv_ref, cos_ref, sin_ref = refs[pos:pos + 4]
        pos += 4
    pos += int(has_prev)
    out_ref = refs[pos]
    pos += 1
    if not has_ctx:
        kout_ref = refs[pos]

    kv = pl.program_id(1)
    kn = _rms(k_ref[...], kg_ref[...])
    if has_ctx:
        cos_t, sin_t = cos_ref[...], sin_ref[...]
        kn = _rope(kn, cos_t, sin_t, DH_C)
        ckb = ck_ref[0].astype(BF16)
        cvb = cv_ref[0].astype(BF16)
    else:
        kout_ref[...] = kn
    kb = kn.astype(BF16)
    vb = v_ref[...].astype(BF16)
    scale = DH_C ** -0.5

    for g in range(G_C):
        sink = sink_ref[kv * G_C + g]
        qn = _rms(q_ref[:, g * DH_C:(g + 1) * DH_C], qg_ref[...])
        if has_ctx:
            qn = _rope(qn, cos_t, sin_t, DH_C)
        for qi in range(n_tok // q_blk):
            rows = slice(qi * q_blk, (qi + 1) * q_blk)
            qb = qn[rows].astype(BF16)
            s = _dot_nt(qb, kb) * scale
            if has_ctx:
                qpos = qi * q_blk + _iota((q_blk, n_tok), 0)
                kpos = _iota((q_blk, n_tok), 1)
                s = jnp.where(jnp.abs(kpos - qpos) <= WINDOW, s, -jnp.inf)
            mx = jnp.maximum(jnp.max(s, axis=1, keepdims=True), sink)
            if has_ctx:
                sc = _dot_nt(qb, ckb) * scale
                mx = jnp.maximum(mx, jnp.max(sc, axis=1, keepdims=True))
            p = jnp.exp(s - mx)
            den = jnp.sum(p, axis=1, keepdims=True) + jnp.exp(sink - mx)
            acc = _dot(p, vb)
            if has_ctx:
                pc = jnp.exp(sc - mx)
                den = den + jnp.sum(pc, axis=1, keepdims=True)
                acc = acc + _dot(pc, cvb)
            out_ref[rows, g * DH_C:(g + 1) * DH_C] = (acc / den).astype(BF16)


def swa_mixer(z, qn_g, kn_g, sink, ctx, n_seq, n_tok, row_blk0, prev=None):
    has_ctx = ctx is not None
    vec = pl.BlockSpec((1, LANES), lambda b, kv: (0, 0))
    in_specs = [pl.BlockSpec(memory_space=pltpu.SMEM),
                pl.BlockSpec((n_tok, G_C * DH_C), lambda b, kv: (row_blk0 + b, ZC_Q // G_C + kv)),
                pl.BlockSpec((n_tok, LANES), lambda b, kv: (row_blk0 + b, ZC_K + kv)),
                pl.BlockSpec((n_tok, LANES), lambda b, kv: (row_blk0 + b, ZC_V + kv)),
                vec, vec]
    args = [sink.astype(F32), z, z, z, qn_g.reshape(1, LANES), kn_g.reshape(1, LANES)]
    if has_ctx:
        ck, cv = ctx
        s_ctx = ck.shape[1]
        cos_t, sin_t = _rope_tables(n_tok, DH_C)
        tab = pl.BlockSpec((n_tok, LANES), lambda b, kv: (0, 0))
        cspec = pl.BlockSpec((1, s_ctx, LANES), lambda b, kv: (b, 0, kv))
        in_specs += [cspec, cspec, tab, tab]
        args += [ck.reshape(n_seq, s_ctx, KV_C * DH_C), cv.reshape(n_seq, s_ctx, KV_C * DH_C), cos_t, sin_t]
    aliases = _alias_prev(in_specs, args, prev)
    out_shape = [jax.ShapeDtypeStruct((z.shape[0], D_GRP), BF16)]
    out_specs = [pl.BlockSpec((n_tok, G_C * DH_C), lambda b, kv: (row_blk0 + b, kv))]
    if not has_ctx:
        out_shape.append(jax.ShapeDtypeStruct((n_seq * n_tok, KV_C * DH_C), F32))
        out_specs.append(pl.BlockSpec((n_tok, LANES), lambda b, kv: (b, kv)))
    return pl.pallas_call(
        functools.partial(_swa_kernel, n_tok=n_tok, q_blk=min(n_tok, 256), has_ctx=has_ctx,
                          has_prev=prev is not None),
        out_shape=out_shape,
        grid=(n_seq, KV_C),
        in_specs=in_specs,
        out_specs=out_specs,
        input_output_aliases=aliases,
        compiler_params=_cparams(("arbitrary", "arbitrary")),
        name="swa_mixer",
    )(*args)


def _dft_matrices(n_tok):
    k = jnp.arange(n_tok, dtype=jnp.int32)[:, None]
    s = jnp.arange(n_tok, dtype=jnp.int32)[None, :]
    ang = (math.pi / n_tok) * ((k * s) % (2 * n_tok)).astype(F32)
    cos, sin = jnp.cos(ang), jnp.sin(ang)
    alt_s = jnp.where(s % 2 == 0, 1.0, -1.0).astype(F32)
    fwd = jnp.concatenate([cos, jnp.where(k == 0, alt_s, -sin)], axis=0)
    inv_re = jnp.where(s == 0, 1.0, 2.0 * cos) / (2 * n_tok)
    alt_t = jnp.where(k % 2 == 0, 1.0, -1.0).astype(F32)
    inv_im = jnp.where(s == 0, alt_t, -2.0 * sin) / (2 * n_tok)
    inv = jnp.concatenate([inv_re, inv_im], axis=1)
    return fwd, inv


def _hyena_filter_kernel(feat_ref, w1_ref, b1_ref, fr_ref, w2_ref, b2_ref, w3_ref, b3_ref, win_ref,
                         fh_ref, fl_ref, g_ref, *, n_tok):
    fr = fr_ref[...]
    h = jnp.sin(fr * (_dot3(feat_ref[...], w1_ref[...]) + b1_ref[...]))
    h = jnp.sin(fr * (_dot3(h, w2_ref[...]) + b2_ref[...]))
    f = _dot3(h, w3_ref[...]) + b3_ref[...]
    win = win_ref[...]
    h_fwd = f[:, :D_GRP] * win
    h_bwd = jnp.where(_iota((n_tok, D_GRP), 0) == 0, 0.0, f[:, D_GRP:] * win)
    fh, fl = fh_ref[...], fl_ref[...]

    def spectrum(x):
        xh, xl = _split2(x)
        return (jnp.dot(fh, xh, preferred_element_type=F32) + jnp.dot(fl, xh, preferred_element_type=F32)
                + jnp.dot(fh, xl, preferred_element_type=F32))

    conj_sign = jnp.where(_iota((2 * n_tok, D_GRP), 0) <= n_tok, 1.0, -1.0)
    g_ref[0] = spectrum(h_fwd) + conj_sign * spectrum(h_bwd)


def hyena_filter_spectra(n_tok, w1, b1, freq, w2, b2, w3, b3, fwd_hi, fwd_lo):
    t = jnp.linspace(0.0, 1.0, n_tok, dtype=F32)[:, None]
    w = (2.0 * math.pi / n_tok) * jnp.arange(n_tok, dtype=F32)[:, None]
    bands = jnp.linspace(1e-4, HY_BANDS - 1, HY_BANDS, dtype=F32)[None, :]
    feats = jnp.concatenate([t, jnp.cos(bands * w), -jnp.sin(bands * w)], axis=-1)
    feats = jnp.pad(feats, ((0, 0), (0, LANES - HY_POS_DIM)))
    rates = jnp.abs(jnp.linspace(HY_FAST, HY_SLOW, D_GRP, dtype=F32))
    window = jnp.exp(-t * rates)
    padv = lambda a: jnp.pad(a, (0, LANES - HY_FF)).reshape(1, LANES)
    w1p = jnp.pad(w1, ((0, LANES - HY_POS_DIM), (0, LANES - HY_FF)))
    w2p = jnp.pad(w2, ((0, LANES - HY_FF), (0, LANES - HY_FF)))
    w3p = jnp.pad(w3, ((0, LANES - HY_FF), (0, 0)))
    full = lambda shape: pl.BlockSpec(shape, lambda o: (0,) * len(shape))
    return pl.pallas_call(
        functools.partial(_hyena_filter_kernel, n_tok=n_tok),
        out_shape=jax.ShapeDtypeStruct((HY_ORDER, 2 * n_tok, D_GRP), F32),
        grid=(HY_ORDER,),
        in_specs=[full((n_tok, LANES)), full((LANES, LANES)), full((1, LANES)), full((1, LANES)),
                  full((LANES, LANES)), full((1, LANES)),
                  pl.BlockSpec((LANES, 2 * D_GRP), lambda o: (0, o)),
                  pl.BlockSpec((1, 2 * D_GRP), lambda o: (0, o)),
                  full((n_tok, D_GRP)), full((2 * n_tok, n_tok)), full((2 * n_tok, n_tok))],
        out_specs=pl.BlockSpec((1, 2 * n_tok, D_GRP), lambda o: (o, 0, 0)),
        compiler_params=_cparams(("arbitrary",)),
        name="hyena_filter_spectra",
    )(feats, w1p, padv(b1), padv(freq), w2p, padv(b2), w3p, b3.reshape(1, -1), window, fwd_hi, fwd_lo)


def _hyena_conv_kernel(zy_ref, zx1_ref, zx2_ref, cwy_ref, cw1_ref, cw2_ref, cby_ref, cb1_ref, cb2_ref,
                       skip_ref, g_ref, fwd_ref, inv_ref, *rest, n_tok):
    out_ref = rest[-1]
    width = zy_ref.shape[1]
    row = _iota((n_tok, width), 0)
    first, last = row == 0, row == n_tok - 1

    def dwconv(x_ref, w_ref, b_ref):
        x = x_ref[...]
        w = w_ref[...]
        prev = jnp.where(first, 0.0, pltpu.roll(x, 1, 0))
        nxt = jnp.where(last, 0.0, pltpu.roll(x, n_tok - 1, 0))
        return prev * w[0:1] + x * w[1:2] + nxt * w[2:3] + b_ref[...]

    y = dwconv(zy_ref, cwy_ref, cby_ref)
    gates = (dwconv(zx1_ref, cw1_ref, cb1_ref), dwconv(zx2_ref, cw2_ref, cb2_ref))
    fwd = fwd_ref[...]
    for o in range(HY_ORDER):
        u = jnp.dot(fwd, y.astype(BF16), preferred_element_type=F32)
        a, b = u[:n_tok], u[n_tok:]
        c, d = g_ref[o, :n_tok, :], g_ref[o, n_tok:, :]
        bd = b * d
        y_re = a * c - jnp.where(first, 0.0, bd)
        y_im = jnp.where(first, bd, a * d + b * c)
        conv = (jnp.dot(inv_ref[:, :n_tok], y_re.astype(BF16), preferred_element_type=F32)
                + jnp.dot(inv_ref[:, n_tok:], y_im.astype(BF16), preferred_element_type=F32))
        y = gates[o] * (conv + skip_ref[o:o + 1, :] * y)
    out_ref[...] = y.astype(BF16)


def hyena_mixer(z, conv_w, conv_b, skip, spectra, fwd_hi, inv_hi, n_seq, n_tok, row_blk0, prev=None):
    width = 256
    nw = D_GRP // width
    zcol = lambda base: pl.BlockSpec((n_tok, width), lambda b, j: (row_blk0 + b, base * LANES // width + j))
    wcol = lambda part: pl.BlockSpec((3, width), lambda b, j: (0, part * nw + j))
    bcol = lambda part: pl.BlockSpec((1, width), lambda b, j: (0, part * nw + j))
    cb = conv_b.reshape(1, 3 * D_GRP)
    in_specs = [zcol(ZD_Y), zcol(ZD_X1), zcol(ZD_X2), wcol(0), wcol(1), wcol(2), bcol(0), bcol(1), bcol(2),
                pl.BlockSpec((HY_ORDER, width), lambda b, j: (0, j)),
                pl.BlockSpec((HY_ORDER, 2 * n_tok, width), lambda b, j: (0, 0, j)),
                pl.BlockSpec((2 * n_tok, n_tok), lambda b, j: (0, 0)),
                pl.BlockSpec((n_tok, 2 * n_tok), lambda b, j: (0, 0))]
    args = [z, z, z, conv_w, conv_w, conv_w, cb, cb, cb, skip, spectra, fwd_hi, inv_hi]
    aliases = _alias_prev(in_specs, args, prev)
    return pl.pallas_call(
        functools.partial(_hyena_conv_kernel, n_tok=n_tok),
        out_shape=jax.ShapeDtypeStruct((z.shape[0], D_GRP), BF16),
        grid=(n_seq, nw),
        in_specs=in_specs,
        out_specs=pl.BlockSpec((n_tok, width), lambda b, j: (row_blk0 + b, j)),
        input_output_aliases=aliases,
        compiler_params=_cparams(("arbitrary", "arbitrary")),
        name="hyena_mixer",
    )(*args)


def _outproj_kernel(a_ref, b_ref, c_ref, d_ref, w_ref, x_ref, g1_ref, ng_ref, sc_ref, sh_ref, x1_ref, h2_ref):
    acc = jnp.dot(a_ref[...], w_ref[0:D_GRP, :], preferred_element_type=F32)
    acc += jnp.dot(b_ref[...], w_ref[D_GRP:2 * D_GRP, :], preferred_element_type=F32)
    acc += jnp.dot(c_ref[...], w_ref[2 * D_GRP:3 * D_GRP, :], preferred_element_type=F32)
    acc += jnp.dot(d_ref[...], w_ref[3 * D_GRP:4 * D_GRP, :], preferred_element_type=F32)
    x1 = x_ref[...] + g1_ref[0] * acc
    x1_ref[...] = x1
    h2_ref[...] = _ada_norm(x1, ng_ref[...], sc_ref[0], sh_ref[0]).astype(BF16)


def out_projection(mix_parts, w_out_bf16, x, mod, layer, norm_g, n_ctx_tok, dec_seq):
    m = x.shape[0]
    tm = 256
    modspec = lambda part: pl.BlockSpec((1, 1, D_MODEL), _mod_row_index(layer, part, tm, n_ctx_tok, dec_seq))
    part = pl.BlockSpec((tm, D_GRP), lambda i: (i, 0))
    row = pl.BlockSpec((tm, D_MODEL), lambda i: (i, 0))
    return pl.pallas_call(
        _outproj_kernel,
        out_shape=[jax.ShapeDtypeStruct((m, D_MODEL), F32), jax.ShapeDtypeStruct((m, D_MODEL), BF16)],
        grid=(m // tm,),
        in_specs=[part, part, part, part, pl.BlockSpec((D_MODEL, D_MODEL), lambda i: (0, 0)), row,
                  modspec(2), pl.BlockSpec((1, D_MODEL), lambda i: (0, 0)), modspec(4), modspec(3)],
        out_specs=[row, row],
        compiler_params=_cparams(("arbitrary",)),
        name="out_projection",
    )(*mix_parts, w_out_bf16, x, mod, norm_g.reshape(1, D_MODEL), mod, mod)


def _ffn_kernel(h_ref, wa_ref, wb_ref, cwa_ref, cwb_ref, cba_ref, cbb_ref, wd_ref, y_ref,
                ua0_ref, ub0_ref, ua1_ref, ub1_ref, *, tile_rows, n_ctx_tok, ctx_seq, dec_seq):
    i = pl.program_id(0)
    j = pl.program_id(1)
    tf = wa_ref.shape[2]

    @pl.when(j == 0)
    def _():
        y_ref[...] = jnp.zeros_like(y_ref)

    @pl.when((i == 0) & (j == 0))
    def _():
        ua1_ref[...] = jnp.zeros_like(ua1_ref)
        ub1_ref[...] = jnp.zeros_like(ub1_ref)

    def step(park_a, park_b, ready_a, ready_b):
        h = h_ref[...]
        park_a[...] = jnp.dot(h, wa_ref[0].astype(BF16), preferred_element_type=F32)
        park_b[...] = jnp.dot(h, wb_ref[0].astype(BF16), preferred_element_type=F32)

        seq_mask = jnp.where(i * tile_rows < n_ctx_tok, ctx_seq - 1, dec_seq - 1)
        pos = _iota((tile_rows, tf), 0) & seq_mask
        first, last = pos == 0, pos == seq_mask

        def dwconv(u_ref, cw_ref, cb_ref):
            u = u_ref[...]
            cw = cw_ref[0]
            prev = jnp.where(first, 0.0, pltpu.roll(u, 1, 0))
            nxt = jnp.where(last, 0.0, pltpu.roll(u, tile_rows - 1, 0))
            return prev * cw[0:1] + u * cw[1:2] + nxt * cw[2:3] + cb_ref[0]

        ua = dwconv(ready_a, cwa_ref, cba_ref)
        ub = dwconv(ready_b, cwb_ref, cbb_ref)
        act = (ua * _sigmoid(ua) * ub).astype(BF16)
        contrib = jnp.dot(act, wd_ref[0].astype(BF16), preferred_element_type=F32)
        y_ref[...] += jnp.where(j >= 1, contrib, 0.0)

    @pl.when(j % 2 == 0)
    def _():
        step(ua0_ref, ub0_ref, ua1_ref, ub1_ref)

    @pl.when(j % 2 == 1)
    def _():
        step(ua1_ref, ub1_ref, ua0_ref, ub0_ref)


def conv_ffn(h2, layer, w_up, conv_w, conv_b, w_down, n_ctx_tok, ctx_seq, dec_seq):
    m = h2.shape[0]
    tm, tf = 1024, 256
    nf = D_FF // tf
    cb = conv_b.reshape(DEPTH, 1, 2 * D_FF)
    up = lambda j: jnp.minimum(j, nf - 1)
    dn = lambda j: jnp.maximum(j - 1, 0)
    park = pltpu.VMEM((tm, tf), F32)
    return pl.pallas_call(
        functools.partial(_ffn_kernel, tile_rows=tm, n_ctx_tok=n_ctx_tok, ctx_seq=ctx_seq, dec_seq=dec_seq),
        out_shape=jax.ShapeDtypeStruct((m, D_MODEL), F32),
        grid=(m // tm, nf + 1),
        in_specs=[pl.BlockSpec((tm, D_MODEL), lambda i, j: (i, 0)),
                  pl.BlockSpec((1, D_MODEL, tf), lambda i, j: (layer, 0, up(j))),
                  pl.BlockSpec((1, D_MODEL, tf), lambda i, j: (layer, 0, nf + up(j))),
                  pl.BlockSpec((1, 3, tf), lambda i, j: (layer, 0, dn(j))),
                  pl.BlockSpec((1, 3, tf), lambda i, j: (layer, 0, nf + dn(j))),
                  pl.BlockSpec((1, 1, tf), lambda i, j: (layer, 0, dn(j))),
                  pl.BlockSpec((1, 1, tf), lambda i, j: (layer, 0, nf + dn(j))),
                  pl.BlockSpec((1, tf, D_MODEL), lambda i, j: (layer, dn(j), 0))],
        out_specs=pl.BlockSpec((tm, D_MODEL), lambda i, j: (i, 0)),
        scratch_shapes=[park, park, park, park],
        compiler_params=_cparams(("arbitrary", "arbitrary")),
        name="conv_ffn",
    )(h2, w_up, w_up, conv_w, conv_w, cb, cb, w_down)


def _residual_kernel(x_ref, y_ref, g_ref, o_ref):
    o_ref[...] = x_ref[...] + g_ref[0] * y_ref[...]


def final_residual(x1, y, mod, layer, row0, n_rows, n_ctx_tok, dec_seq):
    tm = 512
    blk0 = row0 // tm

    def mod_index(i):
        start = (blk0 + i) * tm
        row = jnp.where(start < n_ctx_tok, 0, 1 + (start - n_ctx_tok) // dec_seq)
        return (layer * 48 + row * 6 + 5, 0, 0)

    return pl.pallas_call(
        _residual_kernel,
        out_shape=jax.ShapeDtypeStruct((n_rows, D_MODEL), F32),
        grid=(n_rows // tm,),
        in_specs=[pl.BlockSpec((tm, D_MODEL), lambda i: (blk0 + i, 0)),
                  pl.BlockSpec((tm, D_MODEL), lambda i: (blk0 + i, 0)),
                  pl.BlockSpec((1, 1, D_MODEL), mod_index)],
        out_specs=pl.BlockSpec((tm, D_MODEL), lambda i: (i, 0)),
        compiler_params=_cparams(("arbitrary",)),
        name="final_residual",
    )(x1, y, mod)


def kernel(x_prompt, x_sample, cache_diff_k, cache_diff_v, cache_swa_k, cache_swa_v, state_mlstm_C, state_mlstm_n, state_mlstm_m, c, c_ctx, w_mod, b_mod, norm1_g, norm2_g, w_in, mlstm_ig_b, mlstm_fg_b, mlstm_norm_g, diff_qn_g, diff_kn_g, diff_lam, diff_out_g, swa_qn_g, swa_kn_g, swa_sink, hy_conv_w, hy_conv_b, hy_w1, hy_b1, hy_freq, hy_w2, hy_b2, hy_w3, hy_b3, hy_bias, w_out, ffn_w_up, ffn_conv_w, ffn_conv_b, ffn_w_down):
    n_ctx, ctx_seq = x_prompt.shape[:2]
    n_dec, dec_seq = x_sample.shape[:2]
    n_ctx_tok = n_ctx * ctx_seq
    n_dec_tok = n_dec * dec_seq
    dec_blk0 = n_ctx_tok // dec_seq

    x = (x_prompt.reshape(n_ctx_tok, D_MODEL), x_sample.reshape(n_dec_tok, D_MODEL))
    w_pack, wg_hi, wg_lo = pack_in_weights(w_in)
    cond8 = jnp.concatenate([c_ctx[None, :], c, jnp.zeros((8 - 1 - n_dec, D_MODEL), F32)], axis=0)
    mod = mod_vectors(cond8, w_mod, b_mod)

    dft = {}
    for t in (ctx_seq, dec_seq):
        fwd, inv = _dft_matrices(t)
        dft[t] = dict(fwd=fwd.astype(BF16), inv=inv.astype(BF16), split=_split2(fwd))

    delta = None
    caches = []
    for l in range(DEPTH):
        x, z, gates = in_projection(x, delta, mod, l, norm1_g[l], w_pack, wg_hi, wg_lo, n_ctx_tok, dec_seq)

        gate_bias = jnp.pad(jnp.concatenate([mlstm_ig_b[l].reshape(-1), mlstm_fg_b[l].reshape(-1)]),
                            (0, LANES - 4 * H_A)).reshape(1, LANES)
        lam_init = 0.8 - 0.6 * math.exp(-0.3 * l)

        a_ctx, c_new, n_new, m_new = mlstm_mixer(z, gates, gate_bias, mlstm_norm_g[l], None,
                                                 n_ctx, ctx_seq, 0, ctx_seq, True)
        b_ctx, diff_k = diff_mixer(z, diff_qn_g[l], diff_kn_g[l], diff_lam[l], diff_out_g[l], lam_init,
                                   None, n_ctx, ctx_seq, 0)
        c_ctx_out, swa_k = swa_mixer(z, swa_qn_g[l], swa_kn_g[l], swa_sink[l], None, n_ctx, ctx_seq, 0)
        spec_ctx = hyena_filter_spectra(ctx_seq, hy_w1[l], hy_b1[l], hy_freq[l], hy_w2[l], hy_b2[l],
                                        hy_w3[l], hy_b3[l], *dft[ctx_seq]["split"])
        d_ctx = hyena_mixer(z, hy_conv_w[l], hy_conv_b[l], hy_bias[l], spec_ctx, dft[ctx_seq]["fwd"],
                            dft[ctx_seq]["inv"], n_ctx, ctx_seq, 0)
        a_mix, = mlstm_mixer(z, gates, gate_bias, mlstm_norm_g[l],
                             (state_mlstm_C[:, l], state_mlstm_n[:, l], state_mlstm_m[:, l]),
                             n_dec, dec_seq, dec_blk0, 512, False, prev=a_ctx)
        b_mix, = diff_mixer(z, diff_qn_g[l], diff_kn_g[l], diff_lam[l], diff_out_g[l], lam_init,
                            (cache_diff_k[:, l], cache_diff_v[:, l]), n_dec, dec_seq, dec_blk0, prev=b_ctx)
        c_mix, = swa_mixer(z, swa_qn_g[l], swa_kn_g[l], swa_sink[l],
                           (cache_swa_k[:, l], cache_swa_v[:, l]), n_dec, dec_seq, dec_blk0, prev=c_ctx_out)
        spec_dec = hyena_filter_spectra(dec_seq, hy_w1[l], hy_b1[l], hy_freq[l], hy_w2[l], hy_b2[l],
                                        hy_w3[l], hy_b3[l], *dft[dec_seq]["split"])
        d_mix = hyena_mixer(z, hy_conv_w[l], hy_conv_b[l], hy_bias[l], spec_dec, dft[dec_seq]["fwd"],
                            dft[dec_seq]["inv"], n_dec, dec_seq, dec_blk0, prev=d_ctx)

        x, h2 = out_projection((a_mix, b_mix, c_mix, d_mix), w_out[l].astype(BF16), x, mod, l, norm2_g[l],
                               n_ctx_tok, dec_seq)
        delta = conv_ffn(h2, l, ffn_w_up, ffn_conv_w, ffn_conv_b, ffn_w_down, n_ctx_tok, ctx_seq, dec_seq)

        zc = z[:n_ctx_tok]
        caches.append((
            diff_k.reshape(n_ctx, ctx_seq, H_B, 2, DK_B),
            zc[:, ZB_V * LANES:(ZB_V + 4) * LANES].reshape(n_ctx, ctx_seq, H_B, DV_B),
            swa_k.reshape(n_ctx, ctx_seq, KV_C, DH_C),
            zc[:, ZC_V * LANES:(ZC_V + 2) * LANES].reshape(n_ctx, ctx_seq, KV_C, DH_C),
            c_new,
            n_new.reshape(n_ctx, 2, H_A, DH_A),
            m_new.reshape(n_ctx, 2, H_A),
        ))

    y_prompt = final_residual(x, delta, mod, DEPTH - 1, 0, n_ctx_tok, n_ctx_tok, dec_seq)
    y_sample = final_residual(x, delta, mod, DEPTH - 1, n_ctx_tok, n_dec_tok, n_ctx_tok, dec_seq)
    outs = [jnp.stack([s[i] for s in caches], axis=1) for i in range(7)]
    return (y_prompt.reshape(n_ctx, ctx_seq, D_MODEL), y_sample.reshape(n_dec, dec_seq, D_MODEL), *outs)
```

```python
import functools
import math

import jax
import jax.numpy as jnp
from jax import lax
from jax.experimental import pallas as pl
from jax.experimental.pallas import tpu as pltpu

D_MODEL = 2048
DEPTH = 2
GRID_W = 64
D_GRP = 512
H_A = 4
DH_A = 128
H_B = 4
DV_B = 128
DK_B = 64
H_C = 4
KV_C = 2
G_C = 2
DH_C = 128
WINDOW = 128
HY_ORDER = 2
HY_POS_DIM = 33
HY_BANDS = (HY_POS_DIM - 1) // 2
HY_FF = 64
HY_FAST = math.log(1e-2) / 0.3
HY_SLOW = math.log(1e-2) / 1.5
D_FF = 5632
ROPE_BASE = 10000.0
EPS = 1e-6
N_A = 4 * D_GRP + 4 * H_A

LANES = 128
V7X_VMEM_LIMIT_BYTES = 56 * 1024 * 1024
Z_COLS = 12 * D_GRP
ZA_Q, ZA_K, ZA_V, ZA_O = 0, 4, 8, 12
ZB_Q, ZB_K, ZB_V = 16, 20, 24
ZC_Q, ZC_K, ZC_V = 28, 32, 34
ZD_Y, ZD_X1, ZD_X2 = 36, 40, 44

BF16 = jnp.bfloat16
F32 = jnp.float32


def _cparams(sem):
    return pltpu.CompilerParams(dimension_semantics=sem, vmem_limit_bytes=V7X_VMEM_LIMIT_BYTES)


def _dot(a, b):
    return jnp.dot(a.astype(BF16), b.astype(BF16), preferred_element_type=F32)


def _dot_nt(a, b):
    return lax.dot_general(a.astype(BF16), b.astype(BF16), (((1,), (1,)), ((), ())),
                           preferred_element_type=F32)


def _bf16_head(a):
    bits = lax.bitcast_convert_type(a, jnp.int32) & jnp.int32(-65536)
    return lax.bitcast_convert_type(bits, F32)


def _split2(a):
    head = _bf16_head(a)
    return head.astype(BF16), (a - head).astype(BF16)


def _split3(a):
    head = _bf16_head(a)
    rest = a - head
    mid = _bf16_head(rest)
    return head.astype(BF16), mid.astype(BF16), (rest - mid).astype(BF16)


def _dot3(a, b):
    ah, al = _split2(a)
    bh, bl = _split2(b)
    return (jnp.dot(ah, bh, preferred_element_type=F32) + jnp.dot(al, bh, preferred_element_type=F32)
            + jnp.dot(ah, bl, preferred_element_type=F32))


def _iota(shape, dim):
    return lax.broadcasted_iota(jnp.int32, shape, dim)


def _sigmoid(x):
    return 1.0 / (1.0 + jnp.exp(-x))


def _log_sigmoid(x):
    return jnp.minimum(x, 0.0) - jnp.log(1.0 + jnp.exp(-jnp.abs(x)))


def _alias_prev(in_specs, args, prev):
    if prev is None:
        return {}
    in_specs.append(pl.BlockSpec(memory_space=pl.ANY))
    args.append(prev)
    return {len(args) - 1: 0}


def _mod_kernel(c_ref, w_ref, b_ref, o_ref):
    k = pl.program_id(1)
    c = c_ref[...]
    part = _dot(c * _sigmoid(c), w_ref[0])

    @pl.when(k == 0)
    def _():
        o_ref[0] = part + b_ref[0]

    @pl.when(k > 0)
    def _():
        o_ref[0] += part


def mod_vectors(cond8, w_mod, b_mod):
    tk = 256
    n = w_mod.shape[2]
    out = pl.pallas_call(
        _mod_kernel,
        out_shape=jax.ShapeDtypeStruct((DEPTH, 8, n), F32),
        grid=(DEPTH, D_MODEL // tk),
        in_specs=[pl.BlockSpec((8, tk), lambda l, k: (0, k)),
                  pl.BlockSpec((1, tk, n), lambda l, k: (l, k, 0)),
                  pl.BlockSpec((1, 1, n), lambda l, k: (l, 0, 0))],
        out_specs=pl.BlockSpec((1, 8, n), lambda l, k: (l, 0, 0)),
        compiler_params=_cparams(("arbitrary", "arbitrary")),
        name="mod_vectors",
    )(cond8, w_mod, b_mod.reshape(DEPTH, 1, n))
    return out.reshape(DEPTH * 8 * 6, 1, D_MODEL)


def _mod_row_index(layer, part, tile_rows, n_ctx_tok, dec_seq):
    def index(i, *_):
        start = i * tile_rows
        row = jnp.where(start < n_ctx_tok, 0, 1 + (start - n_ctx_tok) // dec_seq)
        return (layer * 48 + row * 6 + part, 0, 0)
    return index


def _ada_norm(x, ng, sc, sh):
    y = x * lax.rsqrt(jnp.mean(x * x, axis=-1, keepdims=True) + EPS)
    return y * ng * (1.0 + sc) + sh


def _pack_in_kernel(a_ref, b_ref, w_ref, gh_ref, gl_ref):
    c = pl.program_id(1)
    n_aligned = 4 * D_GRP // a_ref.shape[1]
    gap = 4 * H_A

    @pl.when(c < n_aligned)
    def _():
        w_ref[0] = a_ref[0].T.astype(BF16)

    @pl.when(c >= n_aligned)
    def _():
        w_ref[0] = jnp.concatenate([a_ref[0, gap:, :], b_ref[0]], axis=0).T.astype(BF16)

    @pl.when(c == n_aligned)
    def _():
        rows = a_ref[0, 0:LANES, :]
        hi, lo = _split2(jnp.where(_iota(rows.shape, 0) < gap, rows, 0.0).T)
        gh_ref[0] = hi
        gl_ref[0] = lo


def pack_in_weights(w_in):
    tc = 512
    gap = 4 * H_A
    n_blk = Z_COLS // tc
    w_t = jnp.swapaxes(w_in, 1, 2)
    gate = pl.BlockSpec((1, D_MODEL, LANES), lambda l, c: (l, 0, 0))
    return pl.pallas_call(
        _pack_in_kernel,
        out_shape=[jax.ShapeDtypeStruct((DEPTH, D_MODEL, Z_COLS), BF16),
                   jax.ShapeDtypeStruct((DEPTH, D_MODEL, LANES), BF16),
                   jax.ShapeDtypeStruct((DEPTH, D_MODEL, LANES), BF16)],
        grid=(DEPTH, n_blk),
        in_specs=[pl.BlockSpec((1, tc, D_MODEL), lambda l, c: (l, c, 0)),
                  pl.BlockSpec((1, gap, D_MODEL), lambda l, c: (l, (c + 1) * (tc // gap), 0))],
        out_specs=[pl.BlockSpec((1, D_MODEL, tc), lambda l, c: (l, 0, c)), gate, gate],
        compiler_params=_cparams(("arbitrary", "arbitrary")),
        name="pack_in_weights",
    )(w_t, w_t)


def _inproj_kernel(*refs, has_delta, n_ctx_blk):
    if has_delta:
        (x_ref, y_ref, g2_ref, ng_ref, sc_ref, sh_ref, w_ref, wgh_ref, wgl_ref,
         xn_ref, z_ref, gate_ref, hb_ref) = refs
    else:
        (xc_ref, xd_ref, ng_ref, sc_ref, sh_ref, w_ref, wgh_ref, wgl_ref,
         xn_ref, z_ref, gate_ref, hb_ref) = refs

    i = pl.program_id(0)

    @pl.when(pl.program_id(1) == 0)
    def _():
        if has_delta:
            x = x_ref[...] + g2_ref[0] * y_ref[...]
        else:
            x = jnp.where(i < n_ctx_blk, xc_ref[...], xd_ref[...])
        xn_ref[...] = x
        h = _ada_norm(x, ng_ref[...], sc_ref[0], sh_ref[0])
        hb_ref[...] = h.astype(BF16)
        hh, hl = _split2(h)
        wgh = wgh_ref[0]
        gate_ref[...] = (jnp.dot(hh, wgh, preferred_element_type=F32)
                         + jnp.dot(hl, wgh, preferred_element_type=F32)
                         + jnp.dot(hh, wgl_ref[0], preferred_element_type=F32))

    z_ref[...] = jnp.dot(hb_ref[...], w_ref[0], preferred_element_type=F32)


def in_projection(x, delta, mod, layer, norm_g, w_pack, wg_hi, wg_lo, n_ctx_tok, dec_seq):
    tm, tn = 512, 1536
    has_delta = delta is not None
    n_ctx_blk = n_ctx_tok // tm
    row = pl.BlockSpec((tm, D_MODEL), lambda i, j: (i, 0))
    modspec = lambda part: pl.BlockSpec((1, 1, D_MODEL), _mod_row_index(layer, part, tm, n_ctx_tok, dec_seq))
    vec = pl.BlockSpec((1, D_MODEL), lambda i, j: (0, 0))
    if has_delta:
        m = x.shape[0]
        g2spec = pl.BlockSpec((1, 1, D_MODEL), _mod_row_index(layer - 1, 5, tm, n_ctx_tok, dec_seq))
        in_specs = [row, row, g2spec]
        args = [x, delta, mod]
    else:
        x_ctx, x_dec = x
        m = x_ctx.shape[0] + x_dec.shape[0]
        in_specs = [pl.BlockSpec((tm, D_MODEL), lambda i, j: (jnp.minimum(i, n_ctx_blk - 1), 0)),
                    pl.BlockSpec((tm, D_MODEL), lambda i, j: (jnp.maximum(i - n_ctx_blk, 0), 0))]
        args = [x_ctx, x_dec]
    gate_w = pl.BlockSpec((1, D_MODEL, LANES), lambda i, j: (layer, 0, 0))
    in_specs += [vec, modspec(1), modspec(0),
                 pl.BlockSpec((1, D_MODEL, tn), lambda i, j: (layer, 0, j)), gate_w, gate_w]
    args += [norm_g.reshape(1, D_MODEL), mod, mod, w_pack, wg_hi, wg_lo]
    return pl.pallas_call(
        functools.partial(_inproj_kernel, has_delta=has_delta, n_ctx_blk=n_ctx_blk),
        out_shape=[jax.ShapeDtypeStruct((m, D_MODEL), F32), jax.ShapeDtypeStruct((m, Z_COLS), F32),
                   jax.ShapeDtypeStruct((m, LANES), F32)],
        grid=(m // tm, Z_COLS // tn),
        in_specs=in_specs,
        out_specs=[row, pl.BlockSpec((tm, tn), lambda i, j: (i, j)), pl.BlockSpec((tm, LANES), lambda i, j: (i, 0))],
        scratch_shapes=[pltpu.VMEM((tm, D_MODEL), BF16)],
        compiler_params=_cparams(("arbitrary", "arbitrary")),
        name="in_projection",
    )(*args)


def _mlstm_kernel(*refs, n_tok, chunk, has_state, want_state, has_prev):
    refs = list(refs)
    q_ref, k_ref, v_ref, o_ref, gate_ref, gb_ref, ng_ref = refs[:7]
    pos = 7
    if has_state:
        c0_ref, n0_ref, m0_ref = refs[pos:pos + 3]
        pos += 3
    pos += int(has_prev)
    out_ref = refs[pos]
    pos += 1
    if want_state:
        cn_ref, nn_ref, mn_ref = refs[pos:pos + 3]
        pos += 3
    hacc_ref = refs[pos]

    L = chunk
    nc = n_tok // L
    scale = DH_A ** -0.5
    r_idx = _iota((L, L), 0)
    c_idx = _iota((L, L), 1)

    g = gate_ref[...] + gb_ref[...]
    vals = jnp.where(_iota((n_tok, LANES), 1) < 2 * H_A, g, _log_sigmoid(g))

    for d in range(2):
        causal = (c_idx <= r_idx) if d == 0 else (c_idx >= r_idx)
        tri = jnp.where(causal, 1.0, 0.0).astype(BF16)
        if has_state:
            c_mem = [c0_ref[0, d, h] for h in range(H_A)]
            n_mem = [n0_ref[0, d, h] for h in range(H_A)]
            m_prev = [m0_ref[0, d, h] for h in range(H_A)]
        else:
            c_mem = [jnp.zeros((DH_A, DH_A), F32)] * H_A
            n_mem = [jnp.zeros((1, DH_A), F32)] * H_A
            m_prev = [jnp.zeros((1, 1), F32)] * H_A
        order = range(nc) if d == 0 else range(nc - 1, -1, -1)
        for ci, c in enumerate(order):
            rows = slice(c * L, (c + 1) * L)
            v_c = vals[rows]
            p0, p1, p2 = _split3(v_c)
            cum = (jnp.dot(tri, p0, preferred_element_type=F32) + jnp.dot(tri, p1, preferred_element_type=F32)
                   + jnp.dot(tri, p2, preferred_element_type=F32))
            v_t = v_c.T
            cum_t = cum.T
            for h in range(H_A):
                col_i = d * H_A + h
                col_f = 2 * H_A + d * H_A + h
                lanes = slice(h * DH_A, (h + 1) * DH_A)
                c_col = cum[:, col_f:col_f + 1]
                ig_col = v_c[:, col_i:col_i + 1]
                c_row = cum_t[col_f:col_f + 1, :]
                ig_row = v_t[col_i:col_i + 1, :]

                dmat = jnp.where(causal, c_col - c_row + ig_row, -jnp.inf)
                inter = c_col + m_prev[h]
                m_col = jnp.maximum(inter, jnp.max(dmat, axis=1, keepdims=True))
                dexp = jnp.exp(dmat - m_col)
                qs = q_ref[rows, lanes] * scale
                qb = qs.astype(BF16)
                kf = k_ref[rows, lanes]
                vb = v_ref[rows, lanes].astype(BF16)
                p = _dot_nt(qb, kf) * dexp
                w_inter = jnp.exp(inter - m_col)
                num = _dot(p, vb) + w_inter * _dot(qb, c_mem[h])
                den = (jnp.sum(p, axis=1, keepdims=True)
                       + w_inter * jnp.sum(qs * n_mem[h], axis=1, keepdims=True))
                hh = num / jnp.maximum(jnp.abs(den), jnp.exp(-m_col))
                if d == 0:
                    hacc_ref[rows, lanes] = hh
                else:
                    hacc_ref[rows, lanes] += hh

                if want_state or ci < nc - 1:
                    total = c_col[L - 1:L, :] if d == 0 else c_col[0:1, :]
                    w_row = total - c_row + ig_row
                    m_new = jnp.maximum(total + m_prev[h], jnp.max(w_row, axis=1, keepdims=True))
                    decay = jnp.exp(total + m_prev[h] - m_new)
                    wk = jnp.exp(total - c_col + ig_col - m_new)
                    kw = kf * wk
                    c_mem[h] = decay * c_mem[h] + _dot(kw.T, vb)
                    n_mem[h] = decay * n_mem[h] + jnp.sum(kw, axis=0, keepdims=True)
                    m_prev[h] = m_new
        if want_state:
            for h in range(H_A):
                cn_ref[0, d, h] = c_mem[h]
                nn_ref[0, d, h] = n_mem[h]
                mn_ref[0, d, h] = m_prev[h]

    for h in range(H_A):
        lanes = slice(h * DH_A, (h + 1) * DH_A)
        hs = hacc_ref[:, lanes]
        y = hs * lax.rsqrt(jnp.mean(hs * hs, axis=1, keepdims=True) + EPS) * ng_ref[:, lanes]
        out_ref[:, lanes] = (y * _sigmoid(o_ref[:, lanes])).astype(BF16)


def mlstm_mixer(z, gates, gate_bias, norm_g, state, n_seq, n_tok, row_blk0, chunk, want_state, prev=None):
    has_state = state is not None
    col = lambda base: pl.BlockSpec((n_tok, D_GRP), lambda b: (row_blk0 + b, base * LANES // D_GRP))
    in_specs = [col(ZA_Q), col(ZA_K), col(ZA_V), col(ZA_O),
                pl.BlockSpec((n_tok, LANES), lambda b: (row_blk0 + b, 0)),
                pl.BlockSpec((1, LANES), lambda b: (0, 0)),
                pl.BlockSpec((1, D_GRP), lambda b: (0, 0))]
    args = [z, z, z, z, gates, gate_bias, norm_g.reshape(1, D_GRP)]
    st_specs = [pl.BlockSpec((1, 2, H_A, DH_A, DH_A), lambda b: (b, 0, 0, 0, 0)),
                pl.BlockSpec((1, 2, H_A, 1, DH_A), lambda b: (b, 0, 0, 0, 0)),
                pl.BlockSpec((1, 2, H_A, 1, 1), lambda b: (b, 0, 0, 0, 0))]
    if has_state:
        c0, n0, m0 = state
        in_specs += st_specs
        args += [c0, n0.reshape(n_seq, 2, H_A, 1, DH_A), m0.reshape(n_seq, 2, H_A, 1, 1)]
    aliases = _alias_prev(in_specs, args, prev)
    out_shape = [jax.ShapeDtypeStruct((z.shape[0], D_GRP), BF16)]
    out_specs = [pl.BlockSpec((n_tok, D_GRP), lambda b: (row_blk0 + b, 0))]
    if want_state:
        out_shape += [jax.ShapeDtypeStruct((n_seq, 2, H_A, DH_A, DH_A), F32),
                      jax.ShapeDtypeStruct((n_seq, 2, H_A, 1, DH_A), F32),
                      jax.ShapeDtypeStruct((n_seq, 2, H_A, 1, 1), F32)]
        out_specs += st_specs
    outs = pl.pallas_call(
        functools.partial(_mlstm_kernel, n_tok=n_tok, chunk=chunk, has_state=has_state, want_state=want_state,
                          has_prev=prev is not None),
        out_shape=out_shape,
        grid=(n_seq,),
        in_specs=in_specs,
        out_specs=out_specs,
        input_output_aliases=aliases,
        scratch_shapes=[pltpu.VMEM((n_tok, D_GRP), F32)],
        compiler_params=_cparams(("arbitrary",)),
        name="mlstm_mixer",
    )(*args)
    return outs


def _rope_tables(n_tok, dh):
    rows = n_tok // GRID_W
    t_row = jnp.repeat(jnp.arange(rows, dtype=F32), GRID_W)
    t_col = jnp.tile(jnp.arange(GRID_W, dtype=F32), rows)
    n_freq = dh // 4
    inv = ROPE_BASE ** (-jnp.arange(n_freq, dtype=F32) / n_freq)
    ang = jnp.concatenate([t_row[:, None] * inv, t_col[:, None] * inv], axis=-1)
    cos, sin = jnp.cos(ang), jnp.sin(ang)
    reps = LANES // dh
    cos_t = jnp.tile(jnp.concatenate([cos, cos], axis=-1), (1, reps))
    sin_t = jnp.tile(jnp.concatenate([-sin, sin], axis=-1), (1, reps))
    return cos_t, sin_t


def _rope(x, cos_t, sin_t, dh):
    half = dh // 2
    if dh == LANES:
        partner = pltpu.roll(x, half, 1)
    else:
        first = (_iota(x.shape, 1) & half) == 0
        partner = jnp.where(first, pltpu.roll(x, LANES - half, 1), pltpu.roll(x, half, 1))
    return x * cos_t + partner * sin_t


def _rms(x, g):
    return x * lax.rsqrt(jnp.mean(x * x, axis=1, keepdims=True) + EPS) * g


def _diff_kernel(*refs, n_tok, q_blk, lam_init, has_ctx, has_prev):
    refs = list(refs)
    q_ref, k_ref, v_ref, qg_ref, kg_ref, lam_ref, og_ref = refs[:7]
    pos = 7
    if has_ctx:
        ck_ref, cv_ref, cos_ref, sin_ref = refs[pos:pos + 4]
        pos += 4
    pos += int(has_prev)
    out_ref = refs[pos]
    pos += 1
    if not has_ctx:
        kout_ref = refs[pos]

    grp0 = _iota((n_tok, LANES), 1) < DK_B

    def group_norm(x, g):
        x2 = x * x
        s0 = jnp.sum(jnp.where(grp0, x2, 0.0), axis=1, keepdims=True)
        s1 = jnp.sum(jnp.where(grp0, 0.0, x2), axis=1, keepdims=True)
        ms = jnp.where(grp0, s0, s1) * (1.0 / DK_B)
        return x * lax.rsqrt(ms + EPS) * g

    qn = group_norm(q_ref[...], qg_ref[...])
    kn = group_norm(k_ref[...], kg_ref[...])
    if has_ctx:
        cos_t, sin_t = cos_ref[...], sin_ref[...]
        qn = _rope(qn, cos_t, sin_t, DK_B)
        kn = _rope(kn, cos_t, sin_t, DK_B)
    else:
        kout_ref[...] = kn
    kb = kn.astype(BF16)
    vb = v_ref[...].astype(BF16)
    if has_ctx:
        ckb = ck_ref[0].astype(BF16)
        cvb = cv_ref[0].astype(BF16)

    lp = lam_ref[...]
    lam = (jnp.exp(jnp.sum(lp[0:1] * lp[1:2], axis=1, keepdims=True))
           - jnp.exp(jnp.sum(lp[2:3] * lp[3:4], axis=1, keepdims=True)) + lam_init)
    scale = DK_B ** -0.5
    grp0_q = _iota((q_blk, LANES), 1) < DK_B

    for qi in range(n_tok // q_blk):
        rows = slice(qi * q_blk, (qi + 1) * q_blk)
        q_rows = qn[rows]
        outs = []
        for comp in range(2):
            keep = grp0_q if comp == 0 else jnp.logical_not(grp0_q)
            qm = jnp.where(keep, q_rows, 0.0).astype(BF16)
            s = _dot_nt(qm, kb) * scale
            mx = jnp.max(s, axis=1, keepdims=True)
            if has_ctx:
                sc = _dot_nt(qm, ckb) * scale
                mx = jnp.maximum(mx, jnp.max(sc, axis=1, keepdims=True))
            p = jnp.exp(s - mx)
            den = jnp.sum(p, axis=1, keepdims=True)
            acc = _dot(p, vb)
            if has_ctx:
                pc = jnp.exp(sc - mx)
                den = den + jnp.sum(pc, axis=1, keepdims=True)
                acc = acc + _dot(pc, cvb)
            outs.append(acc / den)
        a = outs[0] - lam * outs[1]
        out_ref[rows, :] = (_rms(a, og_ref[...]) * (1.0 - lam_init)).astype(BF16)


def diff_mixer(z, qn_g, kn_g, lam, out_g, lam_init, ctx, n_seq, n_tok, row_blk0, prev=None):
    has_ctx = ctx is not None
    col = lambda base: pl.BlockSpec((n_tok, LANES), lambda b, h: (row_blk0 + b, base + h))
    vec = pl.BlockSpec((1, LANES), lambda b, h: (0, 0))
    in_specs = [col(ZB_Q), col(ZB_K), col(ZB_V), vec, vec,
                pl.BlockSpec((4, LANES), lambda b, h: (0, 0)), vec]
    args = [z, z, z, jnp.tile(qn_g, 2).reshape(1, LANES), jnp.tile(kn_g, 2).reshape(1, LANES),
            jnp.pad(lam, ((0, 0), (0, LANES - DK_B))), out_g.reshape(1, LANES)]
    if has_ctx:
        ck, cv = ctx
        s_ctx = ck.shape[1]
        cos_t, sin_t = _rope_tables(n_tok, DK_B)
        tab = pl.BlockSpec((n_tok, LANES), lambda b, h: (0, 0))
        cspec = pl.BlockSpec((1, s_ctx, LANES), lambda b, h: (b, 0, h))
        in_specs += [cspec, cspec, tab, tab]
        args += [ck.reshape(n_seq, s_ctx, H_B * 2 * DK_B), cv.reshape(n_seq, s_ctx, H_B * DV_B), cos_t, sin_t]
    aliases = _alias_prev(in_specs, args, prev)
    out_shape = [jax.ShapeDtypeStruct((z.shape[0], D_GRP), BF16)]
    out_specs = [pl.BlockSpec((n_tok, LANES), lambda b, h: (row_blk0 + b, h))]
    if not has_ctx:
        out_shape.append(jax.ShapeDtypeStruct((n_seq * n_tok, D_GRP), F32))
        out_specs.append(pl.BlockSpec((n_tok, LANES), lambda b, h: (b, h)))
    return pl.pallas_call(
        functools.partial(_diff_kernel, n_tok=n_tok, q_blk=min(n_tok, 256), lam_init=lam_init, has_ctx=has_ctx,
                          has_prev=prev is not None),
        out_shape=out_shape,
        grid=(n_seq, H_B),
        in_specs=in_specs,
        out_specs=out_specs,
        input_output_aliases=aliases,
        compiler_params=_cparams(("arbitrary", "arbitrary")),
        name="diff_mixer",
    )(*args)


def _swa_kernel(*refs, n_tok, q_blk, has_ctx, has_prev):
    refs = list(refs)
    sink_ref, q_ref, k_ref, v_ref, qg_ref, kg_ref = refs[:6]
    pos = 6
    if has_ctx:
        ck_ref, cv_ref, cos_ref, sin_ref = refs[pos:pos + 4]
        pos += 4
    pos += int(has_prev)
    out_ref = refs[pos]
    pos += 1
    if not has_ctx:
        kout_ref = refs[pos]

    kv = pl.program_id(1)
    kn = _rms(k_ref[...], kg_ref[...])
    if has_ctx:
        cos_t, sin_t = cos_ref[...], sin_ref[...]
        kn = _rope(kn, cos_t, sin_t, DH_C)
        ckb = ck_ref[0].astype(BF16)
        cvb = cv_ref[0].astype(BF16)
    else:
        kout_ref[...] = kn
    kb = kn.astype(BF16)
    vb = v_ref[...].astype(BF16)
    scale = DH_C ** -0.5

    for g in range(G_C):
        sink = sink_ref[kv * G_C + g]
        qn = _rms(q_ref[:, g * DH_C:(g + 1) * DH_C], qg_ref[...])
        if has_ctx:
            qn = _rope(qn, cos_t, sin_t, DH_C)
        for qi in range(n_tok // q_blk):
            rows = slice(qi * q_blk, (qi + 1) * q_blk)
            qb = qn[rows].astype(BF16)
            s = _dot_nt(qb, kb) * scale
            if has_ctx:
                qpos = qi * q_blk + _iota((q_blk, n_tok), 0)
                kpos = _iota((q_blk, n_tok), 1)
                s = jnp.where(jnp.abs(kpos - qpos) <= WINDOW, s, -jnp.inf)
            mx = jnp.maximum(jnp.max(s, axis=1, keepdims=True), sink)
            if has_ctx:
                sc = _dot_nt(qb, ckb) * scale
                mx = jnp.maximum(mx, jnp.max(sc, axis=1, keepdims=True))
            p = jnp.exp(s - mx)
            den = jnp.sum(p, axis=1, keepdims=True) + jnp.exp(sink - mx)
            acc = _dot(p, vb)
            if has_ctx:
                pc = jnp.exp(sc - mx)
                den = den + jnp.sum(pc, axis=1, keepdims=True)
                acc = acc + _dot(pc, cvb)
            out_ref[rows, g * DH_C:(g + 1) * DH_C] = (acc / den).astype(BF16)


def swa_mixer(z, qn_g, kn_g, sink, ctx, n_seq, n_tok, row_blk0, prev=None):
    has_ctx = ctx is not None
    vec = pl.BlockSpec((1, LANES), lambda b, kv: (0, 0))
    in_specs = [pl.BlockSpec(memory_space=pltpu.SMEM),
                pl.BlockSpec((n_tok, G_C * DH_C), lambda b, kv: (row_blk0 + b, ZC_Q // G_C + kv)),
                pl.BlockSpec((n_tok, LANES), lambda b, kv: (row_blk0 + b, ZC_K + kv)),
                pl.BlockSpec((n_tok, LANES), lambda b, kv: (row_blk0 + b, ZC_V + kv)),
                vec, vec]
    args = [sink.astype(F32), z, z, z, qn_g.reshape(1, LANES), kn_g.reshape(1, LANES)]
    if has_ctx:
        ck, cv = ctx
        s_ctx = ck.shape[1]
        cos_t, sin_t = _rope_tables(n_tok, DH_C)
        tab = pl.BlockSpec((n_tok, LANES), lambda b, kv: (0, 0))
        cspec = pl.BlockSpec((1, s_ctx, LANES), lambda b, kv: (b, 0, kv))
        in_specs += [cspec, cspec, tab, tab]
        args += [ck.reshape(n_seq, s_ctx, KV_C * DH_C), cv.reshape(n_seq, s_ctx, KV_C * DH_C), cos_t, sin_t]
    aliases = _alias_prev(in_specs, args, prev)
    out_shape = [jax.ShapeDtypeStruct((z.shape[0], D_GRP), BF16)]
    out_specs = [pl.BlockSpec((n_tok, G_C * DH_C), lambda b, kv: (row_blk0 + b, kv))]
    if not has_ctx:
        out_shape.append(jax.ShapeDtypeStruct((n_seq * n_tok, KV_C * DH_C), F32))
        out_specs.append(pl.BlockSpec((n_tok, LANES), lambda b, kv: (b, kv)))
    return pl.pallas_call(
        functools.partial(_swa_kernel, n_tok=n_tok, q_blk=min(n_tok, 256), has_ctx=has_ctx,
                          has_prev=prev is not None),
        out_shape=out_shape,
        grid=(n_seq, KV_C),
        in_specs=in_specs,
        out_specs=out_specs,
        input_output_aliases=aliases,
        compiler_params=_cparams(("arbitrary", "arbitrary")),
        name="swa_mixer",
    )(*args)


def _dft_matrices(n_tok):
    k = jnp.arange(n_tok, dtype=jnp.int32)[:, None]
    s = jnp.arange(n_tok, dtype=jnp.int32)[None, :]
    ang = (math.pi / n_tok) * ((k * s) % (2 * n_tok)).astype(F32)
    cos, sin = jnp.cos(ang), jnp.sin(ang)
    alt_s = jnp.where(s % 2 == 0, 1.0, -1.0).astype(F32)
    fwd = jnp.concatenate([cos, jnp.where(k == 0, alt_s, -sin)], axis=0)
    inv_re = jnp.where(s == 0, 1.0, 2.0 * cos) / (2 * n_tok)
    alt_t = jnp.where(k % 2 == 0, 1.0, -1.0).astype(F32)
    inv_im = jnp.where(s == 0, alt_t, -2.0 * sin) / (2 * n_tok)
    inv = jnp.concatenate([inv_re, inv_im], axis=1)
    return fwd, inv


def _hyena_filter_kernel(feat_ref, w1_ref, b1_ref, fr_ref, w2_ref, b2_ref, w3_ref, b3_ref, win_ref,
                         fh_ref, fl_ref, g_ref, *, n_tok):
    fr = fr_ref[...]
    h = jnp.sin(fr * (_dot3(feat_ref[...], w1_ref[...]) + b1_ref[...]))
    h = jnp.sin(fr * (_dot3(h, w2_ref[...]) + b2_ref[...]))
    f = _dot3(h, w3_ref[...]) + b3_ref[...]
    win = win_ref[...]
    h_fwd = f[:, :D_GRP] * win
    h_bwd = jnp.where(_iota((n_tok, D_GRP), 0) == 0, 0.0, f[:, D_GRP:] * win)
    fh, fl = fh_ref[...], fl_ref[...]

    def spectrum(x):
        xh, xl = _split2(x)
        return (jnp.dot(fh, xh, preferred_element_type=F32) + jnp.dot(fl, xh, preferred_element_type=F32)
                + jnp.dot(fh, xl, preferred_element_type=F32))

    conj_sign = jnp.where(_iota((2 * n_tok, D_GRP), 0) <= n_tok, 1.0, -1.0)
    g_ref[0] = spectrum(h_fwd) + conj_sign * spectrum(h_bwd)


def hyena_filter_spectra(n_tok, w1, b1, freq, w2, b2, w3, b3, fwd_hi, fwd_lo):
    t = jnp.linspace(0.0, 1.0, n_tok, dtype=F32)[:, None]
    w = (2.0 * math.pi / n_tok) * jnp.arange(n_tok, dtype=F32)[:, None]
    bands = jnp.linspace(1e-4, HY_BANDS - 1, HY_BANDS, dtype=F32)[None, :]
    feats = jnp.concatenate([t, jnp.cos(bands * w), -jnp.sin(bands * w)], axis=-1)
    feats = jnp.pad(feats, ((0, 0), (0, LANES - HY_POS_DIM)))
    rates = jnp.abs(jnp.linspace(HY_FAST, HY_SLOW, D_GRP, dtype=F32))
    window = jnp.exp(-t * rates)
    padv = lambda a: jnp.pad(a, (0, LANES - HY_FF)).reshape(1, LANES)
    w1p = jnp.pad(w1, ((0, LANES - HY_POS_DIM), (0, LANES - HY_FF)))
    w2p = jnp.pad(w2, ((0, LANES - HY_FF), (0, LANES - HY_FF)))
    w3p = jnp.pad(w3, ((0, LANES - HY_FF), (0, 0)))
    full = lambda shape: pl.BlockSpec(shape, lambda o: (0,) * len(shape))
    return pl.pallas_call(
        functools.partial(_hyena_filter_kernel, n_tok=n_tok),
        out_shape=jax.ShapeDtypeStruct((HY_ORDER, 2 * n_tok, D_GRP), F32),
        grid=(HY_ORDER,),
        in_specs=[full((n_tok, LANES)), full((LANES, LANES)), full((1, LANES)), full((1, LANES)),
                  full((LANES, LANES)), full((1, LANES)),
                  pl.BlockSpec((LANES, 2 * D_GRP), lambda o: (0, o)),
                  pl.BlockSpec((1, 2 * D_GRP), lambda o: (0, o)),
                  full((n_tok, D_GRP)), full((2 * n_tok, n_tok)), full((2 * n_tok, n_tok))],
        out_specs=pl.BlockSpec((1, 2 * n_tok, D_GRP), lambda o: (o, 0, 0)),
        compiler_params=_cparams(("arbitrary",)),
        name="hyena_filter_spectra",
    )(feats, w1p, padv(b1), padv(freq), w2p, padv(b2), w3p, b3.reshape(1, -1), window, fwd_hi, fwd_lo)


def _hyena_conv_kernel(zy_ref, zx1_ref, zx2_ref, cwy_ref, cw1_ref, cw2_ref, cby_ref, cb1_ref, cb2_ref,
                       skip_ref, g_ref, fwd_ref, inv_ref, *rest, n_tok):
    out_ref = rest[-1]
    width = zy_ref.shape[1]
    row = _iota((n_tok, width), 0)
    first, last = row == 0, row == n_tok - 1

    def dwconv(x_ref, w_ref, b_ref):
        x = x_ref[...]
        w = w_ref[...]
        prev = jnp.where(first, 0.0, pltpu.roll(x, 1, 0))
        nxt = jnp.where(last, 0.0, pltpu.roll(x, n_tok - 1, 0))
        return prev * w[0:1] + x * w[1:2] + nxt * w[2:3] + b_ref[...]

    y = dwconv(zy_ref, cwy_ref, cby_ref)
    gates = (dwconv(zx1_ref, cw1_ref, cb1_ref), dwconv(zx2_ref, cw2_ref, cb2_ref))
    fwd = fwd_ref[...]
    for o in range(HY_ORDER):
        u = jnp.dot(fwd, y.astype(BF16), preferred_element_type=F32)
        a, b = u[:n_tok], u[n_tok:]
        c, d = g_ref[o, :n_tok, :], g_ref[o, n_tok:, :]
        bd = b * d
        y_re = a * c - jnp.where(first, 0.0, bd)
        y_im = jnp.where(first, bd, a * d + b * c)
        conv = (jnp.dot(inv_ref[:, :n_tok], y_re.astype(BF16), preferred_element_type=F32)
                + jnp.dot(inv_ref[:, n_tok:], y_im.astype(BF16), preferred_element_type=F32))
        y = gates[o] * (conv + skip_ref[o:o + 1, :] * y)
    out_ref[...] = y.astype(BF16)


def hyena_mixer(z, conv_w, conv_b, skip, spectra, fwd_hi, inv_hi, n_seq, n_tok, row_blk0, prev=None):
    width = 256
    nw = D_GRP // width
    zcol = lambda base: pl.BlockSpec((n_tok, width), lambda b, j: (row_blk0 + b, base * LANES // width + j))
    wcol = lambda part: pl.BlockSpec((3, width), lambda b, j: (0, part * nw + j))
    bcol = lambda part: pl.BlockSpec((1, width), lambda b, j: (0, part * nw + j))
    cb = conv_b.reshape(1, 3 * D_GRP)
    in_specs = [zcol(ZD_Y), zcol(ZD_X1), zcol(ZD_X2), wcol(0), wcol(1), wcol(2), bcol(0), bcol(1), bcol(2),
                pl.BlockSpec((HY_ORDER, width), lambda b, j: (0, j)),
                pl.BlockSpec((HY_ORDER, 2 * n_tok, width), lambda b, j: (0, 0, j)),
                pl.BlockSpec((2 * n_tok, n_tok), lambda b, j: (0, 0)),
                pl.BlockSpec((n_tok, 2 * n_tok), lambda b, j: (0, 0))]
    args = [z, z, z, conv_w, conv_w, conv_w, cb, cb, cb, skip, spectra, fwd_hi, inv_hi]
    aliases = _alias_prev(in_specs, args, prev)
    return pl.pallas_call(
        functools.partial(_hyena_conv_kernel, n_tok=n_tok),
        out_shape=jax.ShapeDtypeStruct((z.shape[0], D_GRP), BF16),
        grid=(n_seq, nw),
        in_specs=in_specs,
        out_specs=pl.BlockSpec((n_tok, width), lambda b, j: (row_blk0 + b, j)),
        input_output_aliases=aliases,
        compiler_params=_cparams(("arbitrary", "arbitrary")),
        name="hyena_mixer",
    )(*args)


def _outproj_kernel(a_ref, b_ref, c_ref, d_ref, w_ref, x_ref, g1_ref, ng_ref, sc_ref, sh_ref, x1_ref, h2_ref):
    acc = jnp.dot(a_ref[...], w_ref[0:D_GRP, :], preferred_element_type=F32)
    acc += jnp.dot(b_ref[...], w_ref[D_GRP:2 * D_GRP, :], preferred_element_type=F32)
    acc += jnp.dot(c_ref[...], w_ref[2 * D_GRP:3 * D_GRP, :], preferred_element_type=F32)
    acc += jnp.dot(d_ref[...], w_ref[3 * D_GRP:4 * D_GRP, :], preferred_element_type=F32)
    x1 = x_ref[...] + g1_ref[0] * acc
    x1_ref[...] = x1
    h2_ref[...] = _ada_norm(x1, ng_ref[...], sc_ref[0], sh_ref[0]).astype(BF16)


def out_projection(mix_parts, w_out_bf16, x, mod, layer, norm_g, n_ctx_tok, dec_seq):
    m = x.shape[0]
    tm = 512
    modspec = lambda part: pl.BlockSpec((1, 1, D_MODEL), _mod_row_index(layer, part, tm, n_ctx_tok, dec_seq))
    part = pl.BlockSpec((tm, D_GRP), lambda i: (i, 0))
    row = pl.BlockSpec((tm, D_MODEL), lambda i: (i, 0))
    return pl.pallas_call(
        _outproj_kernel,
        out_shape=[jax.ShapeDtypeStruct((m, D_MODEL), F32), jax.ShapeDtypeStruct((m, D_MODEL), BF16)],
        grid=(m // tm,),
        in_specs=[part, part, part, part, pl.BlockSpec((D_MODEL, D_MODEL), lambda i: (0, 0)), row,
                  modspec(2), pl.BlockSpec((1, D_MODEL), lambda i: (0, 0)), modspec(4), modspec(3)],
        out_specs=[row, row],
        compiler_params=_cparams(("arbitrary",)),
        name="out_projection",
    )(*mix_parts, w_out_bf16, x, mod, norm_g.reshape(1, D_MODEL), mod, mod)


def _ffn_kernel(h_ref, wa_ref, wb_ref, cwa_ref, cwb_ref, cba_ref, cbb_ref, wd_ref, y_ref,
                ua0_ref, ub0_ref, ua1_ref, ub1_ref, *, tile_rows, n_ctx_tok, ctx_seq, dec_seq):
    i = pl.program_id(0)
    j = pl.program_id(1)
    tf = wa_ref.shape[2]

    @pl.when(j == 0)
    def _():
        y_ref[...] = jnp.zeros_like(y_ref)

    @pl.when((i == 0) & (j == 0))
    def _():
        ua1_ref[...] = jnp.zeros_like(ua1_ref)
        ub1_ref[...] = jnp.zeros_like(ub1_ref)

    def step(park_a, park_b, ready_a, ready_b):
        h = h_ref[...]
        park_a[...] = jnp.dot(h, wa_ref[0].astype(BF16), preferred_element_type=F32)
        park_b[...] = jnp.dot(h, wb_ref[0].astype(BF16), preferred_element_type=F32)

        seq_mask = jnp.where(i * tile_rows < n_ctx_tok, ctx_seq - 1, dec_seq - 1)
        pos = _iota((tile_rows, tf), 0) & seq_mask
        first, last = pos == 0, pos == seq_mask

        def dwconv(u_ref, cw_ref, cb_ref):
            u = u_ref[...]
            cw = cw_ref[0]
            prev = jnp.where(first, 0.0, pltpu.roll(u, 1, 0))
            nxt = jnp.where(last, 0.0, pltpu.roll(u, tile_rows - 1, 0))
            return prev * cw[0:1] + u * cw[1:2] + nxt * cw[2:3] + cb_ref[0]

        ua = dwconv(ready_a, cwa_ref, cba_ref)
        ub = dwconv(ready_b, cwb_ref, cbb_ref)
        act = (ua * _sigmoid(ua) * ub).astype(BF16)
        contrib = jnp.dot(act, wd_ref[0].astype(BF16), preferred_element_type=F32)
        y_ref[...] += jnp.where(j >= 1, contrib, 0.0)

    @pl.when(j % 2 == 0)
    def _():
        step(ua0_ref, ub0_ref, ua1_ref, ub1_ref)

    @pl.when(j % 2 == 1)
    def _():
        step(ua1_ref, ub1_ref, ua0_ref, ub0_ref)


def conv_ffn(h2, layer, w_up, conv_w, conv_b, w_down, n_ctx_tok, ctx_seq, dec_seq):
    m = h2.shape[0]
    tm, tf = 1024, 256
    nf = D_FF // tf
    cb = conv_b.reshape(DEPTH, 1, 2 * D_FF)
    up = lambda j: jnp.minimum(j, nf - 1)
    dn = lambda j: jnp.maximum(j - 1, 0)
    park = pltpu.VMEM((tm, tf), F32)
    return pl.pallas_call(
        functools.partial(_ffn_kernel, tile_rows=tm, n_ctx_tok=n_ctx_tok, ctx_seq=ctx_seq, dec_seq=dec_seq),
        out_shape=jax.ShapeDtypeStruct((m, D_MODEL), F32),
        grid=(m // tm, nf + 1),
        in_specs=[pl.BlockSpec((tm, D_MODEL), lambda i, j: (i, 0)),
                  pl.BlockSpec((1, D_MODEL, tf), lambda i, j: (layer, 0, up(j))),
                  pl.BlockSpec((1, D_MODEL, tf), lambda i, j: (layer, 0, nf + up(j))),
                  pl.BlockSpec((1, 3, tf), lambda i, j: (layer, 0, dn(j))),
                  pl.BlockSpec((1, 3, tf), lambda i, j: (layer, 0, nf + dn(j))),
                  pl.BlockSpec((1, 1, tf), lambda i, j: (layer, 0, dn(j))),
                  pl.BlockSpec((1, 1, tf), lambda i, j: (layer, 0, nf + dn(j))),
                  pl.BlockSpec((1, tf, D_MODEL), lambda i, j: (layer, dn(j), 0))],
        out_specs=pl.BlockSpec((tm, D_MODEL), lambda i, j: (i, 0)),
        scratch_shapes=[park, park, park, park],
        compiler_params=_cparams(("arbitrary", "arbitrary")),
        name="conv_ffn",
    )(h2, w_up, w_up, conv_w, conv_w, cb, cb, w_down)


def _residual_kernel(x_ref, y_ref, g_ref, o_ref):
    o_ref[...] = x_ref[...] + g_ref[0] * y_ref[...]


def final_residual(x1, y, mod, layer, row0, n_rows, n_ctx_tok, dec_seq):
    tm = 512
    blk0 = row0 // tm

    def mod_index(i):
        start = (blk0 + i) * tm
        row = jnp.where(start < n_ctx_tok, 0, 1 + (start - n_ctx_tok) // dec_seq)
        return (layer * 48 + row * 6 + 5, 0, 0)

    return pl.pallas_call(
        _residual_kernel,
        out_shape=jax.ShapeDtypeStruct((n_rows, D_MODEL), F32),
        grid=(n_rows // tm,),
        in_specs=[pl.BlockSpec((tm, D_MODEL), lambda i: (blk0 + i, 0)),
                  pl.BlockSpec((tm, D_MODEL), lambda i: (blk0 + i, 0)),
                  pl.BlockSpec((1, 1, D_MODEL), mod_index)],
        out_specs=pl.BlockSpec((tm, D_MODEL), lambda i: (i, 0)),
        compiler_params=_cparams(("arbitrary",)),
        name="final_residual",
    )(x1, y, mod)


def kernel(x_prompt, x_sample, cache_diff_k, cache_diff_v, cache_swa_k, cache_swa_v, state_mlstm_C, state_mlstm_n, state_mlstm_m, c, c_ctx, w_mod, b_mod, norm1_g, norm2_g, w_in, mlstm_ig_b, mlstm_fg_b, mlstm_norm_g, diff_qn_g, diff_kn_g, diff_lam, diff_out_g, swa_qn_g, swa_kn_g, swa_sink, hy_conv_w, hy_conv_b, hy_w1, hy_b1, hy_freq, hy_w2, hy_b2, hy_w3, hy_b3, hy_bias, w_out, ffn_w_up, ffn_conv_w, ffn_conv_b, ffn_w_down):
    n_ctx, ctx_seq = x_prompt.shape[:2]
    n_dec, dec_seq = x_sample.shape[:2]
    n_ctx_tok = n_ctx * ctx_seq
    n_dec_tok = n_dec * dec_seq
    dec_blk0 = n_ctx_tok // dec_seq

    x = (x_prompt.reshape(n_ctx_tok, D_MODEL), x_sample.reshape(n_dec_tok, D_MODEL))
    w_pack, wg_hi, wg_lo = pack_in_weights(w_in)
    cond8 = jnp.concatenate([c_ctx[None, :], c, jnp.zeros((8 - 1 - n_dec, D_MODEL), F32)], axis=0)
    mod = mod_vectors(cond8, w_mod, b_mod)

    dft = {}
    for t in (ctx_seq, dec_seq):
        fwd, inv = _dft_matrices(t)
        dft[t] = dict(fwd=fwd.astype(BF16), inv=inv.astype(BF16), split=_split2(fwd))

    delta = None
    caches = []
    for l in range(DEPTH):
        x, z, gates = in_projection(x, delta, mod, l, norm1_g[l], w_pack, wg_hi, wg_lo, n_ctx_tok, dec_seq)

        gate_bias = jnp.pad(jnp.concatenate([mlstm_ig_b[l].reshape(-1), mlstm_fg_b[l].reshape(-1)]),
                            (0, LANES - 4 * H_A)).reshape(1, LANES)
        lam_init = 0.8 - 0.6 * math.exp(-0.3 * l)

        a_ctx, c_new, n_new, m_new = mlstm_mixer(z, gates, gate_bias, mlstm_norm_g[l], None,
                                                 n_ctx, ctx_seq, 0, ctx_seq, True)
        b_ctx, diff_k = diff_mixer(z, diff_qn_g[l], diff_kn_g[l], diff_lam[l], diff_out_g[l], lam_init,
                                   None, n_ctx, ctx_seq, 0)
        c_ctx_out, swa_k = swa_mixer(z, swa_qn_g[l], swa_kn_g[l], swa_sink[l], None, n_ctx, ctx_seq, 0)
        spec_ctx = hyena_filter_spectra(ctx_seq, hy_w1[l], hy_b1[l], hy_freq[l], hy_w2[l], hy_b2[l],
                                        hy_w3[l], hy_b3[l], *dft[ctx_seq]["split"])
        d_ctx = hyena_mixer(z, hy_conv_w[l], hy_conv_b[l], hy_bias[l], spec_ctx, dft[ctx_seq]["fwd"],
                            dft[ctx_seq]["inv"], n_ctx, ctx_seq, 0)
        a_mix, = mlstm_mixer(z, gates, gate_bias, mlstm_norm_g[l],
                             (state_mlstm_C[:, l], state_mlstm_n[:, l], state_mlstm_m[:, l]),
                             n_dec, dec_seq, dec_blk0, 512, False, prev=a_ctx)
        b_mix, = diff_mixer(z, diff_qn_g[l], diff_kn_g[l], diff_lam[l], diff_out_g[l], lam_init,
                            (cache_diff_k[:, l], cache_diff_v[:, l]), n_dec, dec_seq, dec_blk0, prev=b_ctx)
        c_mix, = swa_mixer(z, swa_qn_g[l], swa_kn_g[l], swa_sink[l],
                           (cache_swa_k[:, l], cache_swa_v[:, l]), n_dec, dec_seq, dec_blk0, prev=c_ctx_out)
        spec_dec = hyena_filter_spectra(dec_seq, hy_w1[l], hy_b1[l], hy_freq[l], hy_w2[l], hy_b2[l],
                                        hy_w3[l], hy_b3[l], *dft[dec_seq]["split"])
        d_mix = hyena_mixer(z, hy_conv_w[l], hy_conv_b[l], hy_bias[l], spec_dec, dft[dec_seq]["fwd"],
                            dft[dec_seq]["inv"], n_dec, dec_seq, dec_blk0, prev=d_ctx)

        x, h2 = out_projection((a_mix, b_mix, c_mix, d_mix), w_out[l].astype(BF16), x, mod, l, norm2_g[l],
                               n_ctx_tok, dec_seq)
        delta = conv_ffn(h2, l, ffn_w_up, ffn_conv_w, ffn_conv_b, ffn_w_down, n_ctx_tok, ctx_seq, dec_seq)

        zc = z[:n_ctx_tok]
        caches.append((
            diff_k.reshape(n_ctx, ctx_seq, H_B, 2, DK_B),
            zc[:, ZB_V * LANES:(ZB_V + 4) * LANES].reshape(n_ctx, ctx_seq, H_B, DV_B),
            swa_k.reshape(n_ctx, ctx_seq, KV_C, DH_C),
            zc[:, ZC_V * LANES:(ZC_V + 2) * LANES].reshape(n_ctx, ctx_seq, KV_C, DH_C),
            c_new,
            n_new.reshape(n_ctx, 2, H_A, DH_A),
            m_new.reshape(n_ctx, 2, H_A),
        ))

    y_prompt = final_residual(x, delta, mod, DEPTH - 1, 0, n_ctx_tok, n_ctx_tok, dec_seq)
    y_sample = final_residual(x, delta, mod, DEPTH - 1, n_ctx_tok, n_dec_tok, n_ctx_tok, dec_seq)
    outs = [jnp.stack([s[i] for s in caches], axis=1) for i in range(7)]
    return (y_prompt.reshape(n_ctx, ctx_seq, D_MODEL), y_sample.reshape(n_dec, dec_seq, D_MODEL), *outs)
```

```python
import functools
import math

import jax
import jax.numpy as jnp
from jax import lax
from jax.experimental import pallas as pl
from jax.experimental.pallas import tpu as pltpu

D_MODEL = 2048
DEPTH = 2
GRID_W = 64
D_GRP = 512
N_MIXERS = D_MODEL // D_GRP
H_A = 4
DH_A = 128
H_B = 4
DV_B = 128
DK_B = 64
H_C = 4
KV_C = 2
G_C = 2
DH_C = 128
WINDOW = 128
HY_ORDER = 2
HY_POS_DIM = 33
HY_BANDS = (HY_POS_DIM - 1) // 2
HY_FF = 64
HY_FAST = math.log(1e-2) / 0.3
HY_SLOW = math.log(1e-2) / 1.5
D_FF = 5632
ROPE_BASE = 10000.0
EPS = 1e-6
N_A = 4 * D_GRP + 4 * H_A

LANES = 128
SUBLANES = 8
V7X_VMEM_LIMIT_BYTES = 56 * 1024 * 1024
Z_COLS = 12 * D_GRP
ZA_Q, ZA_K, ZA_V, ZA_O = 0, 4, 8, 12
ZB_Q, ZB_K, ZB_V = 16, 20, 24
ZC_Q, ZC_K, ZC_V = 28, 32, 34
ZD_Y, ZD_X1, ZD_X2 = 36, 40, 44

BF16 = jnp.bfloat16
F32 = jnp.float32


def _cparams(sem):
    return pltpu.CompilerParams(dimension_semantics=sem, vmem_limit_bytes=V7X_VMEM_LIMIT_BYTES)


def _dot(a, b):
    return jnp.dot(a.astype(BF16), b.astype(BF16), preferred_element_type=F32)


def _dot_nt(a, b):
    return lax.dot_general(a.astype(BF16), b.astype(BF16), (((1,), (1,)), ((), ())),
                           preferred_element_type=F32)


def _bf16_head(a):
    bits = lax.bitcast_convert_type(a, jnp.int32) & jnp.int32(-65536)
    return lax.bitcast_convert_type(bits, F32)


def _split2(a):
    head = _bf16_head(a)
    return head.astype(BF16), (a - head).astype(BF16)


def _split3(a):
    head = _bf16_head(a)
    rest = a - head
    mid = _bf16_head(rest)
    return head.astype(BF16), mid.astype(BF16), (rest - mid).astype(BF16)


def _dot3(a, b):
    ah, al = _split2(a)
    bh, bl = _split2(b)
    return (jnp.dot(ah, bh, preferred_element_type=F32) + jnp.dot(al, bh, preferred_element_type=F32)
            + jnp.dot(ah, bl, preferred_element_type=F32))


def _iota(shape, dim):
    return lax.broadcasted_iota(jnp.int32, shape, dim)


def _sigmoid(x):
    return 1.0 / (1.0 + jnp.exp(-x))


def _log_sigmoid(x):
    return jnp.minimum(x, 0.0) - jnp.log(1.0 + jnp.exp(-jnp.abs(x)))


def _mod_kernel(c_ref, w_ref, b_ref, o_ref):
    k = pl.program_id(1)
    c = c_ref[...]
    part = _dot(c * _sigmoid(c), w_ref[0])

    @pl.when(k == 0)
    def _():
        o_ref[0] = part + b_ref[0]

    @pl.when(k > 0)
    def _():
        o_ref[0] += part


def mod_vectors(cond8, w_mod, b_mod):
    tk = 256
    n = w_mod.shape[2]
    out = pl.pallas_call(
        _mod_kernel,
        out_shape=jax.ShapeDtypeStruct((DEPTH, 8, n), F32),
        grid=(DEPTH, D_MODEL // tk),
        in_specs=[pl.BlockSpec((8, tk), lambda l, k: (0, k)),
                  pl.BlockSpec((1, tk, n), lambda l, k: (l, k, 0)),
                  pl.BlockSpec((1, 1, n), lambda l, k: (l, 0, 0))],
        out_specs=pl.BlockSpec((1, 8, n), lambda l, k: (l, 0, 0)),
        compiler_params=_cparams(("arbitrary", "arbitrary")),
        name="mod_vectors",
    )(cond8, w_mod, b_mod.reshape(DEPTH, 1, n))
    return out.reshape(DEPTH * 8 * 6, 1, D_MODEL)


def _mod_row_index(layer, part, tile_rows, n_ctx_tok, dec_seq):
    def index(i, *_):
        start = i * tile_rows
        row = jnp.where(start < n_ctx_tok, 0, 1 + (start - n_ctx_tok) // dec_seq)
        return (layer * 48 + row * 6 + part, 0, 0)
    return index


def _ada_norm(x, ng, sc, sh):
    y = x * lax.rsqrt(jnp.mean(x * x, axis=-1, keepdims=True) + EPS)
    return y * ng * (1.0 + sc) + sh


def _pack_in_kernel(a_ref, b_ref, w_ref, gh_ref, gl_ref):
    c = pl.program_id(1)
    n_aligned = 4 * D_GRP // a_ref.shape[1]
    gap = 4 * H_A

    @pl.when(c < n_aligned)
    def _():
        w_ref[0] = a_ref[0].T.astype(BF16)

    @pl.when(c >= n_aligned)
    def _():
        w_ref[0] = jnp.concatenate([a_ref[0, gap:, :], b_ref[0]], axis=0).T.astype(BF16)

    @pl.when(c == n_aligned)
    def _():
        rows = a_ref[0, 0:LANES, :]
        hi, lo = _split2(jnp.where(_iota(rows.shape, 0) < gap, rows, 0.0).T)
        gh_ref[0] = hi
        gl_ref[0] = lo


def pack_in_weights(w_in):
    tc = 512
    gap = 4 * H_A
    n_blk = Z_COLS // tc
    w_t = jnp.swapaxes(w_in, 1, 2)
    gate = pl.BlockSpec((1, D_MODEL, LANES), lambda l, c: (l, 0, 0))
    return pl.pallas_call(
        _pack_in_kernel,
        out_shape=[jax.ShapeDtypeStruct((DEPTH, D_MODEL, Z_COLS), BF16),
                   jax.ShapeDtypeStruct((DEPTH, D_MODEL, LANES), BF16),
                   jax.ShapeDtypeStruct((DEPTH, D_MODEL, LANES), BF16)],
        grid=(DEPTH, n_blk),
        in_specs=[pl.BlockSpec((1, tc, D_MODEL), lambda l, c: (l, c, 0)),
                  pl.BlockSpec((1, gap, D_MODEL), lambda l, c: (l, (c + 1) * (tc // gap), 0))],
        out_specs=[pl.BlockSpec((1, D_MODEL, tc), lambda l, c: (l, 0, c)), gate, gate],
        compiler_params=_cparams(("arbitrary", "arbitrary")),
        name="pack_in_weights",
    )(w_t, w_t)


def _inproj_kernel(*refs, has_delta, n_ctx_blk):
    if has_delta:
        (x_ref, y_ref, g2_ref, ng_ref, sc_ref, sh_ref, w_ref, wgh_ref, wgl_ref,
         xn_ref, z_ref, gate_ref, hb_ref) = refs
    else:
        (xc_ref, xd_ref, ng_ref, sc_ref, sh_ref, w_ref, wgh_ref, wgl_ref,
         xn_ref, z_ref, gate_ref, hb_ref) = refs

    i = pl.program_id(0)

    @pl.when(pl.program_id(1) == 0)
    def _():
        if has_delta:
            x = x_ref[...] + g2_ref[0] * y_ref[...]
        else:
            x = jnp.where(i < n_ctx_blk, xc_ref[...], xd_ref[...])
        xn_ref[...] = x
        h = _ada_norm(x, ng_ref[...], sc_ref[0], sh_ref[0])
        hb_ref[...] = h.astype(BF16)
        hh, hl = _split2(h)
        wgh = wgh_ref[0]
        gate_ref[...] = (jnp.dot(hh, wgh, preferred_element_type=F32)
                         + jnp.dot(hl, wgh, preferred_element_type=F32)
                         + jnp.dot(hh, wgl_ref[0], preferred_element_type=F32))

    z_ref[...] = jnp.dot(hb_ref[...], w_ref[0], preferred_element_type=F32)


def in_projection(x, delta, mod, layer, norm_g, w_pack, wg_hi, wg_lo, n_ctx_tok, dec_seq):
    tm, tn = 512, 1536
    has_delta = delta is not None
    n_ctx_blk = n_ctx_tok // tm
    row = pl.BlockSpec((tm, D_MODEL), lambda i, j: (i, 0))
    modspec = lambda part: pl.BlockSpec((1, 1, D_MODEL), _mod_row_index(layer, part, tm, n_ctx_tok, dec_seq))
    vec = pl.BlockSpec((1, D_MODEL), lambda i, j: (0, 0))
    if has_delta:
        m = x.shape[0]
        g2spec = pl.BlockSpec((1, 1, D_MODEL), _mod_row_index(layer - 1, 5, tm, n_ctx_tok, dec_seq))
        in_specs = [row, row, g2spec]
        args = [x, delta, mod]
    else:
        x_ctx, x_dec = x
        m = x_ctx.shape[0] + x_dec.shape[0]
        in_specs = [pl.BlockSpec((tm, D_MODEL), lambda i, j: (jnp.minimum(i, n_ctx_blk - 1), 0)),
                    pl.BlockSpec((tm, D_MODEL), lambda i, j: (jnp.maximum(i - n_ctx_blk, 0), 0))]
        args = [x_ctx, x_dec]
    gate_w = pl.BlockSpec((1, D_MODEL, LANES), lambda i, j: (layer, 0, 0))
    in_specs += [vec, modspec(1), modspec(0),
                 pl.BlockSpec((1, D_MODEL, tn), lambda i, j: (layer, 0, j)), gate_w, gate_w]
    args += [norm_g.reshape(1, D_MODEL), mod, mod, w_pack, wg_hi, wg_lo]
    return pl.pallas_call(
        functools.partial(_inproj_kernel, has_delta=has_delta, n_ctx_blk=n_ctx_blk),
        out_shape=[jax.ShapeDtypeStruct((m, D_MODEL), F32), jax.ShapeDtypeStruct((m, Z_COLS), F32),
                   jax.ShapeDtypeStruct((m, LANES), F32)],
        grid=(m // tm, Z_COLS // tn),
        in_specs=in_specs,
        out_specs=[row, pl.BlockSpec((tm, tn), lambda i, j: (i, j)), pl.BlockSpec((tm, LANES), lambda i, j: (i, 0))],
        scratch_shapes=[pltpu.VMEM((tm, D_MODEL), BF16)],
        compiler_params=_cparams(("arbitrary", "arbitrary")),
        name="in_projection",
    )(*args)


def _mlstm_kernel(*refs, n_tok, chunk, has_state, want_state):
    refs = list(refs)
    q_ref, k_ref, v_ref, o_ref, gate_ref, gb_ref, ng_ref = refs[:7]
    pos = 7
    if has_state:
        c0_ref, n0_ref, m0_ref = refs[pos:pos + 3]
        pos += 3
    out_ref = refs[pos]
    pos += 1
    if want_state:
        cn_ref, nn_ref, mn_ref = refs[pos:pos + 3]
        pos += 3
    hacc_ref = refs[pos]

    L = chunk
    nc = n_tok // L
    scale = DH_A ** -0.5
    r_idx = _iota((L, L), 0)
    c_idx = _iota((L, L), 1)
    ones_b = jnp.ones((L, DH_A), BF16)

    g = gate_ref[...] + gb_ref[...]
    vals = jnp.where(_iota((n_tok, LANES), 1) < 2 * H_A, g, _log_sigmoid(g))

    for d in range(2):
        causal = (c_idx <= r_idx) if d == 0 else (c_idx >= r_idx)
        tri = jnp.where(causal, 1.0, 0.0).astype(BF16)
        if has_state:
            c_mem = [c0_ref[0, d, h] for h in range(H_A)]
            n_mem = [n0_ref[0, d, h] for h in range(H_A)]
            m_prev = [m0_ref[0, d, h] for h in range(H_A)]
            pad = jnp.zeros((SUBLANES - 1, DH_A), F32)
            n_mat = [_dot(jnp.concatenate([n, pad], axis=0).T, jnp.ones((SUBLANES, DH_A), BF16)) for n in n_mem]
        else:
            c_mem = [None] * H_A
            n_mem = [None] * H_A
            n_mat = [None] * H_A
            m_prev = [jnp.zeros((1, 1), F32)] * H_A
        order = range(nc) if d == 0 else range(nc - 1, -1, -1)
        for ci, c in enumerate(order):
            rows = slice(c * L, (c + 1) * L)
            v_c = vals[rows]
            p0, p1, p2 = _split3(v_c)
            cum = (jnp.dot(tri, p0, preferred_element_type=F32) + jnp.dot(tri, p1, preferred_element_type=F32)
                   + jnp.dot(tri, p2, preferred_element_type=F32))
            v_t = v_c.T
            cum_t = cum.T
            for h in range(H_A):
                col_i = d * H_A + h
                col_f = 2 * H_A + d * H_A + h
                lanes = slice(h * DH_A, (h + 1) * DH_A)
                c_col = cum[:, col_f:col_f + 1]
                ig_col = v_c[:, col_i:col_i + 1]
                c_row = cum_t[col_f:col_f + 1, :]
                ig_row = v_t[col_i:col_i + 1, :]

                dmat = jnp.where(causal, c_col - c_row + ig_row, -jnp.inf)
                inter = c_col + m_prev[h]
                m_col = jnp.maximum(inter, jnp.max(dmat, axis=1, keepdims=True))
                dexp = jnp.exp(dmat - m_col)
                qs = q_ref[rows, lanes] * scale
                qb = qs.astype(BF16)
                kf = k_ref[rows, lanes]
                vb = v_ref[rows, lanes].astype(BF16)
                pb = (_dot_nt(qb, kf) * dexp).astype(BF16)
                num = jnp.dot(pb, vb, preferred_element_type=F32)
                den = jnp.dot(pb, ones_b, preferred_element_type=F32)
                carried = c_mem[h] is not None
                if carried:
                    w_inter = jnp.broadcast_to(jnp.exp(inter - m_col), (L, DH_A))
                    num = num + w_inter * _dot(qb, c_mem[h])
                    den = den + w_inter * _dot(qb, n_mat[h])
                hh = num / jnp.maximum(jnp.abs(den), jnp.exp(-m_col))
                if d == 0:
                    hacc_ref[rows, lanes] = hh
                else:
                    hacc_ref[rows, lanes] += hh

                if want_state or ci < nc - 1:
                    total = c_col[L - 1:L, :] if d == 0 else c_col[0:1, :]
                    w_row = total - c_row + ig_row
                    m_new = jnp.maximum(total + m_prev[h], jnp.max(w_row, axis=1, keepdims=True))
                    decay = jnp.exp(total + m_prev[h] - m_new)
                    wk = jnp.exp(total - c_col + ig_col - m_new)
                    kw = kf * wk
                    kw_t = kw.T.astype(BF16)
                    c_upd = jnp.dot(kw_t, vb, preferred_element_type=F32)
                    c_mem[h] = decay * c_mem[h] + c_upd if carried else c_upd
                    if ci < nc - 1:
                        n_upd = jnp.dot(kw_t, ones_b, preferred_element_type=F32)
                        n_mat[h] = decay * n_mat[h] + n_upd if carried else n_upd
                    if want_state:
                        n_row = jnp.sum(kw, axis=0, keepdims=True)
                        n_mem[h] = decay * n_mem[h] + n_row if carried else n_row
                    m_prev[h] = m_new
        if want_state:
            for h in range(H_A):
                cn_ref[0, d, h] = c_mem[h]
                nn_ref[0, d, h] = n_mem[h]
                mn_ref[0, d, h] = m_prev[h]

    for h in range(H_A):
        lanes = slice(h * DH_A, (h + 1) * DH_A)
        hs = hacc_ref[:, lanes]
        y = hs * lax.rsqrt(jnp.mean(hs * hs, axis=1, keepdims=True) + EPS) * ng_ref[:, lanes]
        out_ref[:, lanes] = (y * _sigmoid(o_ref[:, lanes])).astype(BF16)


def mlstm_mixer(z, gates, gate_bias, norm_g, state, n_seq, n_tok, row_blk0, chunk, want_state):
    has_state = state is not None
    col = lambda base: pl.BlockSpec((n_tok, D_GRP), lambda b: (row_blk0 + b, base * LANES // D_GRP))
    in_specs = [col(ZA_Q), col(ZA_K), col(ZA_V), col(ZA_O),
                pl.BlockSpec((n_tok, LANES), lambda b: (row_blk0 + b, 0)),
                pl.BlockSpec((1, LANES), lambda b: (0, 0)),
                pl.BlockSpec((1, D_GRP), lambda b: (0, 0))]
    args = [z, z, z, z, gates, gate_bias, norm_g.reshape(1, D_GRP)]
    st_specs = [pl.BlockSpec((1, 2, H_A, DH_A, DH_A), lambda b: (b, 0, 0, 0, 0)),
                pl.BlockSpec((1, 2, H_A, 1, DH_A), lambda b: (b, 0, 0, 0, 0)),
                pl.BlockSpec((1, 2, H_A, 1, 1), lambda b: (b, 0, 0, 0, 0))]
    if has_state:
        c0, n0, m0 = state
        in_specs += st_specs
        args += [c0, n0.reshape(n_seq, 2, H_A, 1, DH_A), m0.reshape(n_seq, 2, H_A, 1, 1)]
    out_shape = [jax.ShapeDtypeStruct((n_seq * n_tok, D_GRP), BF16)]
    out_specs = [pl.BlockSpec((n_tok, D_GRP), lambda b: (b, 0))]
    if want_state:
        out_shape += [jax.ShapeDtypeStruct((n_seq, 2, H_A, DH_A, DH_A), F32),
                      jax.ShapeDtypeStruct((n_seq, 2, H_A, 1, DH_A), F32),
                      jax.ShapeDtypeStruct((n_seq, 2, H_A, 1, 1), F32)]
        out_specs += st_specs
    outs = pl.pallas_call(
        functools.partial(_mlstm_kernel, n_tok=n_tok, chunk=chunk, has_state=has_state, want_state=want_state),
        out_shape=out_shape,
        grid=(n_seq,),
        in_specs=in_specs,
        out_specs=out_specs,
        scratch_shapes=[pltpu.VMEM((n_tok, D_GRP), F32)],
        compiler_params=_cparams(("arbitrary",)),
        name="mlstm_mixer",
    )(*args)
    return outs


def _rope_tables(n_tok, dh):
    rows = n_tok // GRID_W
    t_row = jnp.repeat(jnp.arange(rows, dtype=F32), GRID_W)
    t_col = jnp.tile(jnp.arange(GRID_W, dtype=F32), rows)
    n_freq = dh // 4
    inv = ROPE_BASE ** (-jnp.arange(n_freq, dtype=F32) / n_freq)
    ang = jnp.concatenate([t_row[:, None] * inv, t_col[:, None] * inv], axis=-1)
    cos, sin = jnp.cos(ang), jnp.sin(ang)
    reps = LANES // dh
    cos_t = jnp.tile(jnp.concatenate([cos, cos], axis=-1), (1, reps))
    sin_t = jnp.tile(jnp.concatenate([-sin, sin], axis=-1), (1, reps))
    return cos_t, sin_t


def _rope(x, cos_t, sin_t, dh):
    half = dh // 2
    if dh == LANES:
        partner = pltpu.roll(x, half, 1)
    else:
        first = (_iota(x.shape, 1) & half) == 0
        partner = jnp.where(first, pltpu.roll(x, LANES - half, 1), pltpu.roll(x, half, 1))
    return x * cos_t + partner * sin_t


def _rms(x, g):
    return x * lax.rsqrt(jnp.mean(x * x, axis=1, keepdims=True) + EPS) * g


def _diff_kernel(*refs, n_tok, q_blk, lam_init, has_ctx):
    refs = list(refs)
    q_ref, k_ref, v_ref, qg_ref, kg_ref, lam_ref, og_ref = refs[:7]
    pos = 7
    if has_ctx:
        ck_ref, cv_ref, cos_ref, sin_ref = refs[pos:pos + 4]
        pos += 4
    out_ref = refs[pos]
    pos += 1
    if not has_ctx:
        kout_ref = refs[pos]

    grp0 = _iota((n_tok, LANES), 1) < DK_B

    def group_norm(x, g):
        x2 = x * x
        s0 = jnp.sum(jnp.where(grp0, x2, 0.0), axis=1, keepdims=True)
        s1 = jnp.sum(jnp.where(grp0, 0.0, x2), axis=1, keepdims=True)
        ms = jnp.where(grp0, s0, s1) * (1.0 / DK_B)
        return x * lax.rsqrt(ms + EPS) * g

    qn = group_norm(q_ref[...], qg_ref[...])
    kn = group_norm(k_ref[...], kg_ref[...])
    if has_ctx:
        cos_t, sin_t = cos_ref[...], sin_ref[...]
        qn = _rope(qn, cos_t, sin_t, DK_B)
        kn = _rope(kn, cos_t, sin_t, DK_B)
    else:
        kout_ref[...] = kn
    kb = kn.astype(BF16)
    vb = v_ref[...].astype(BF16)
    if has_ctx:
        ckb = ck_ref[0].astype(BF16)
        cvb = cv_ref[0].astype(BF16)

    lp = lam_ref[...]
    lam = (jnp.exp(jnp.sum(lp[0:1] * lp[1:2], axis=1, keepdims=True))
           - jnp.exp(jnp.sum(lp[2:3] * lp[3:4], axis=1, keepdims=True)) + lam_init)
    scale = DK_B ** -0.5
    grp0_q = _iota((q_blk, LANES), 1) < DK_B

    for qi in range(n_tok // q_blk):
        rows = slice(qi * q_blk, (qi + 1) * q_blk)
        q_rows = qn[rows]
        outs = []
        for comp in range(2):
            keep = grp0_q if comp == 0 else jnp.logical_not(grp0_q)
            qm = jnp.where(keep, q_rows, 0.0).astype(BF16)
            s = _dot_nt(qm, kb) * scale
            mx = jnp.max(s, axis=1, keepdims=True)
            if has_ctx:
                sc = _dot_nt(qm, ckb) * scale
                mx = jnp.maximum(mx, jnp.max(sc, axis=1, keepdims=True))
            p = jnp.exp(s - mx)
            den = jnp.sum(p, axis=1, keepdims=True)
            acc = _dot(p, vb)
            if has_ctx:
                pc = jnp.exp(sc - mx)
                den = den + jnp.sum(pc, axis=1, keepdims=True)
                acc = acc + _dot(pc, cvb)
            outs.append(acc / den)
        a = outs[0] - lam * outs[1]
        out_ref[rows, :] = (_rms(a, og_ref[...]) * (1.0 - lam_init)).astype(BF16)


def diff_mixer(z, qn_g, kn_g, lam, out_g, lam_init, ctx, n_seq, n_tok, row_blk0):
    has_ctx = ctx is not None
    col = lambda base: pl.BlockSpec((n_tok, LANES), lambda b, h: (row_blk0 + b, base + h))
    vec = pl.BlockSpec((1, LANES), lambda b, h: (0, 0))
    in_specs = [col(ZB_Q), col(ZB_K), col(ZB_V), vec, vec,
                pl.BlockSpec((4, LANES), lambda b, h: (0, 0)), vec]
    args = [z, z, z, jnp.tile(qn_g, 2).reshape(1, LANES), jnp.tile(kn_g, 2).reshape(1, LANES),
            jnp.pad(lam, ((0, 0), (0, LANES - DK_B))), out_g.reshape(1, LANES)]
    if has_ctx:
        ck, cv = ctx
        s_ctx = ck.shape[1]
        cos_t, sin_t = _rope_tables(n_tok, DK_B)
        tab = pl.BlockSpec((n_tok, LANES), lambda b, h: (0, 0))
        cspec = pl.BlockSpec((1, s_ctx, LANES), lambda b, h: (b, 0, h))
        in_specs += [cspec, cspec, tab, tab]
        args += [ck.reshape(n_seq, s_ctx, H_B * 2 * DK_B), cv.reshape(n_seq, s_ctx, H_B * DV_B), cos_t, sin_t]
    out_shape = [jax.ShapeDtypeStruct((n_seq * n_tok, D_GRP), BF16)]
    out_specs = [pl.BlockSpec((n_tok, LANES), lambda b, h: (b, h))]
    if not has_ctx:
        out_shape.append(jax.ShapeDtypeStruct((n_seq * n_tok, D_GRP), F32))
        out_specs.append(pl.BlockSpec((n_tok, LANES), lambda b, h: (b, h)))
    return pl.pallas_call(
        functools.partial(_diff_kernel, n_tok=n_tok, q_blk=min(n_tok, 256), lam_init=lam_init, has_ctx=has_ctx),
        out_shape=out_shape,
        grid=(n_seq, H_B),
        in_specs=in_specs,
        out_specs=out_specs,
        compiler_params=_cparams(("arbitrary", "arbitrary")),
        name="diff_mixer",
    )(*args)


def _swa_kernel(*refs, n_tok, q_blk, has_ctx):
    refs = list(refs)
    sink_ref, q_ref, k_ref, v_ref, qg_ref, kg_ref = refs[:6]
    pos = 6
    if has_ctx:
        ck_ref, cv_ref, cos_ref, sin_ref = refs[pos:pos + 4]
        pos += 4
    out_ref = refs[pos]
    pos += 1
    if not has_ctx:
        kout_ref = refs[pos]

    kv = pl.program_id(1)
    kn = _rms(k_ref[...], kg_ref[...])
    if has_ctx:
        cos_t, sin_t = cos_ref[...], sin_ref[...]
        kn = _rope(kn, cos_t, sin_t, DH_C)
        ckb = ck_ref[0].astype(BF16)
        cvb = cv_ref[0].astype(BF16)
    else:
        kout_ref[...] = kn
    kb = kn.astype(BF16)
    vb = v_ref[...].astype(BF16)
    scale = DH_C ** -0.5

    for g in range(G_C):
        sink = sink_ref[kv * G_C + g]
        qn = _rms(q_ref[:, g * DH_C:(g + 1) * DH_C], qg_ref[...])
        if has_ctx:
            qn = _rope(qn, cos_t, sin_t, DH_C)
        for qi in range(n_tok // q_blk):
            rows = slice(qi * q_blk, (qi + 1) * q_blk)
            qb = qn[rows].astype(BF16)
            s = _dot_nt(qb, kb) * scale
            if has_ctx:
                qpos = qi * q_blk + _iota((q_blk, n_tok), 0)
                kpos = _iota((q_blk, n_tok), 1)
                s = jnp.where(jnp.abs(kpos - qpos) <= WINDOW, s, -jnp.inf)
            mx = jnp.maximum(jnp.max(s, axis=1, keepdims=True), sink)
            if has_ctx:
                sc = _dot_nt(qb, ckb) * scale
                mx = jnp.maximum(mx, jnp.max(sc, axis=1, keepdims=True))
            p = jnp.exp(s - mx)
            den = jnp.sum(p, axis=1, keepdims=True) + jnp.exp(sink - mx)
            acc = _dot(p, vb)
            if has_ctx:
                pc = jnp.exp(sc - mx)
                den = den + jnp.sum(pc, axis=1, keepdims=True)
                acc = acc + _dot(pc, cvb)
            out_ref[rows, g * DH_C:(g + 1) * DH_C] = (acc / den).astype(BF16)


def swa_mixer(z, qn_g, kn_g, sink, ctx, n_seq, n_tok, row_blk0):
    has_ctx = ctx is not None
    vec = pl.BlockSpec((1, LANES), lambda b, kv: (0, 0))
    in_specs = [pl.BlockSpec(memory_space=pltpu.SMEM),
                pl.BlockSpec((n_tok, G_C * DH_C), lambda b, kv: (row_blk0 + b, ZC_Q // G_C + kv)),
                pl.BlockSpec((n_tok, LANES), lambda b, kv: (row_blk0 + b, ZC_K + kv)),
                pl.BlockSpec((n_tok, LANES), lambda b, kv: (row_blk0 + b, ZC_V + kv)),
                vec, vec]
    args = [sink.astype(F32), z, z, z, qn_g.reshape(1, LANES), kn_g.reshape(1, LANES)]
    if has_ctx:
        ck, cv = ctx
        s_ctx = ck.shape[1]
        cos_t, sin_t = _rope_tables(n_tok, DH_C)
        tab = pl.BlockSpec((n_tok, LANES), lambda b, kv: (0, 0))
        cspec = pl.BlockSpec((1, s_ctx, LANES), lambda b, kv: (b, 0, kv))
        in_specs += [cspec, cspec, tab, tab]
        args += [ck.reshape(n_seq, s_ctx, KV_C * DH_C), cv.reshape(n_seq, s_ctx, KV_C * DH_C), cos_t, sin_t]
    out_shape = [jax.ShapeDtypeStruct((n_seq * n_tok, D_GRP), BF16)]
    out_specs = [pl.BlockSpec((n_tok, G_C * DH_C), lambda b, kv: (b, kv))]
    if not has_ctx:
        out_shape.append(jax.ShapeDtypeStruct((n_seq * n_tok, KV_C * DH_C), F32))
        out_specs.append(pl.BlockSpec((n_tok, LANES), lambda b, kv: (b, kv)))
    return pl.pallas_call(
        functools.partial(_swa_kernel, n_tok=n_tok, q_blk=min(n_tok, 256), has_ctx=has_ctx),
        out_shape=out_shape,
        grid=(n_seq, KV_C),
        in_specs=in_specs,
        out_specs=out_specs,
        compiler_params=_cparams(("arbitrary", "arbitrary")),
        name="swa_mixer",
    )(*args)


def _dft_matrices(n_tok):
    k = jnp.arange(n_tok, dtype=jnp.int32)[:, None]
    s = jnp.arange(n_tok, dtype=jnp.int32)[None, :]
    ang = (math.pi / n_tok) * ((k * s) % (2 * n_tok)).astype(F32)
    cos, sin = jnp.cos(ang), jnp.sin(ang)
    alt_s = jnp.where(s % 2 == 0, 1.0, -1.0).astype(F32)
    fwd = jnp.concatenate([cos, jnp.where(k == 0, alt_s, -sin)], axis=0)
    inv_re = jnp.where(s == 0, 1.0, 2.0 * cos) / (2 * n_tok)
    alt_t = jnp.where(k % 2 == 0, 1.0, -1.0).astype(F32)
    inv_im = jnp.where(s == 0, alt_t, -2.0 * sin) / (2 * n_tok)
    inv = jnp.concatenate([inv_re, inv_im], axis=1)
    return fwd, inv


def _hyena_filter_kernel(feat_ref, w1_ref, b1_ref, fr_ref, w2_ref, b2_ref, w3_ref, b3_ref, win_ref,
                         fh_ref, fl_ref, g_ref, *, n_tok):
    fr = fr_ref[...]
    h = jnp.sin(fr * (_dot3(feat_ref[...], w1_ref[...]) + b1_ref[...]))
    h = jnp.sin(fr * (_dot3(h, w2_ref[...]) + b2_ref[...]))
    f = _dot3(h, w3_ref[...]) + b3_ref[...]
    win = win_ref[...]
    h_fwd = f[:, :D_GRP] * win
    h_bwd = jnp.where(_iota((n_tok, D_GRP), 0) == 0, 0.0, f[:, D_GRP:] * win)
    fh, fl = fh_ref[...], fl_ref[...]

    def spectrum(x):
        xh, xl = _split2(x)
        return (jnp.dot(fh, xh, preferred_element_type=F32) + jnp.dot(fl, xh, preferred_element_type=F32)
                + jnp.dot(fh, xl, preferred_element_type=F32))

    conj_sign = jnp.where(_iota((2 * n_tok, D_GRP), 0) <= n_tok, 1.0, -1.0)
    g_ref[0] = spectrum(h_fwd) + conj_sign * spectrum(h_bwd)


def hyena_filter_spectra(n_tok, w1, b1, freq, w2, b2, w3, b3, fwd_hi, fwd_lo):
    t = jnp.linspace(0.0, 1.0, n_tok, dtype=F32)[:, None]
    w = (2.0 * math.pi / n_tok) * jnp.arange(n_tok, dtype=F32)[:, None]
    bands = jnp.linspace(1e-4, HY_BANDS - 1, HY_BANDS, dtype=F32)[None, :]
    feats = jnp.concatenate([t, jnp.cos(bands * w), -jnp.sin(bands * w)], axis=-1)
    feats = jnp.pad(feats, ((0, 0), (0, LANES - HY_POS_DIM)))
    rates = jnp.abs(jnp.linspace(HY_FAST, HY_SLOW, D_GRP, dtype=F32))
    window = jnp.exp(-t * rates)
    padv = lambda a: jnp.pad(a, (0, LANES - HY_FF)).reshape(1, LANES)
    w1p = jnp.pad(w1, ((0, LANES - HY_POS_DIM), (0, LANES - HY_FF)))
    w2p = jnp.pad(w2, ((0, LANES - HY_FF), (0, LANES - HY_FF)))
    w3p = jnp.pad(w3, ((0, LANES - HY_FF), (0, 0)))
    full = lambda shape: pl.BlockSpec(shape, lambda o: (0,) * len(shape))
    return pl.pallas_call(
        functools.partial(_hyena_filter_kernel, n_tok=n_tok),
        out_shape=jax.ShapeDtypeStruct((HY_ORDER, 2 * n_tok, D_GRP), F32),
        grid=(HY_ORDER,),
        in_specs=[full((n_tok, LANES)), full((LANES, LANES)), full((1, LANES)), full((1, LANES)),
                  full((LANES, LANES)), full((1, LANES)),
                  pl.BlockSpec((LANES, 2 * D_GRP), lambda o: (0, o)),
                  pl.BlockSpec((1, 2 * D_GRP), lambda o: (0, o)),
                  full((n_tok, D_GRP)), full((2 * n_tok, n_tok)), full((2 * n_tok, n_tok))],
        out_specs=pl.BlockSpec((1, 2 * n_tok, D_GRP), lambda o: (o, 0, 0)),
        compiler_params=_cparams(("arbitrary",)),
        name="hyena_filter_spectra",
    )(feats, w1p, padv(b1), padv(freq), w2p, padv(b2), w3p, b3.reshape(1, -1), window, fwd_hi, fwd_lo)


def _hyena_conv_kernel(zy_ref, zx1_ref, zx2_ref, cwy_ref, cw1_ref, cw2_ref, cby_ref, cb1_ref, cb2_ref,
                       skip_ref, g_ref, fwd_ref, inv_ref, out_ref, *, n_tok):
    width = zy_ref.shape[1]
    row = _iota((n_tok, width), 0)
    first, last = row == 0, row == n_tok - 1

    def dwconv(x_ref, w_ref, b_ref):
        x = x_ref[...]
        w = w_ref[...]
        prev = jnp.where(first, 0.0, pltpu.roll(x, 1, 0))
        nxt = jnp.where(last, 0.0, pltpu.roll(x, n_tok - 1, 0))
        return prev * w[0:1] + x * w[1:2] + nxt * w[2:3] + b_ref[...]

    y = dwconv(zy_ref, cwy_ref, cby_ref)
    gates = (dwconv(zx1_ref, cw1_ref, cb1_ref), dwconv(zx2_ref, cw2_ref, cb2_ref))
    fwd = fwd_ref[...]
    for o in range(HY_ORDER):
        u = jnp.dot(fwd, y.astype(BF16), preferred_element_type=F32)
        a, b = u[:n_tok], u[n_tok:]
        c, d = g_ref[o, :n_tok, :], g_ref[o, n_tok:, :]
        bd = b * d
        y_re = a * c - jnp.where(first, 0.0, bd)
        y_im = jnp.where(first, bd, a * d + b * c)
        conv = (jnp.dot(inv_ref[:, :n_tok], y_re.astype(BF16), preferred_element_type=F32)
                + jnp.dot(inv_ref[:, n_tok:], y_im.astype(BF16), preferred_element_type=F32))
        y = gates[o] * (conv + skip_ref[o:o + 1, :] * y)
    out_ref[...] = y.astype(BF16)


def hyena_mixer(z, conv_w, conv_b, skip, spectra, fwd_hi, inv_hi, n_seq, n_tok, row_blk0):
    width = 256
    nw = D_GRP // width
    zcol = lambda base: pl.BlockSpec((n_tok, width), lambda b, j: (row_blk0 + b, base * LANES // width + j))
    wcol = lambda part: pl.BlockSpec((3, width), lambda b, j: (0, part * nw + j))
    bcol = lambda part: pl.BlockSpec((1, width), lambda b, j: (0, part * nw + j))
    cb = conv_b.reshape(1, 3 * D_GRP)
    in_specs = [zcol(ZD_Y), zcol(ZD_X1), zcol(ZD_X2), wcol(0), wcol(1), wcol(2), bcol(0), bcol(1), bcol(2),
                pl.BlockSpec((HY_ORDER, width), lambda b, j: (0, j)),
                pl.BlockSpec((HY_ORDER, 2 * n_tok, width), lambda b, j: (0, 0, j)),
                pl.BlockSpec((2 * n_tok, n_tok), lambda b, j: (0, 0)),
                pl.BlockSpec((n_tok, 2 * n_tok), lambda b, j: (0, 0))]
    args = [z, z, z, conv_w, conv_w, conv_w, cb, cb, cb, skip, spectra, fwd_hi, inv_hi]
    return pl.pallas_call(
        functools.partial(_hyena_conv_kernel, n_tok=n_tok),
        out_shape=jax.ShapeDtypeStruct((n_seq * n_tok, D_GRP), BF16),
        grid=(n_seq, nw),
        in_specs=in_specs,
        out_specs=pl.BlockSpec((n_tok, width), lambda b, j: (b, j)),
        compiler_params=_cparams(("arbitrary", "arbitrary")),
        name="hyena_mixer",
    )(*args)


def _outproj_kernel(*refs, n_ctx_blk):
    ctx_parts, dec_parts = refs[0:N_MIXERS], refs[N_MIXERS:2 * N_MIXERS]
    w_ref, x_ref, g1_ref, ng_ref, sc_ref, sh_ref, x1_ref, h2_ref = refs[2 * N_MIXERS:]
    is_ctx = pl.program_id(0) < n_ctx_blk
    acc = None
    for g in range(N_MIXERS):
        part = jnp.where(is_ctx, ctx_parts[g][...], dec_parts[g][...])
        term = jnp.dot(part, w_ref[g * D_GRP:(g + 1) * D_GRP, :], preferred_element_type=F32)
        acc = term if acc is None else acc + term
    x1 = x_ref[...] + g1_ref[0] * acc
    x1_ref[...] = x1
    h2_ref[...] = _ada_norm(x1, ng_ref[...], sc_ref[0], sh_ref[0]).astype(BF16)


def out_projection(ctx_parts, dec_parts, w_out_bf16, x, mod, layer, norm_g, n_ctx_tok, dec_seq):
    m = x.shape[0]
    tm = 512
    n_ctx_blk = n_ctx_tok // tm
    modspec = lambda part: pl.BlockSpec((1, 1, D_MODEL), _mod_row_index(layer, part, tm, n_ctx_tok, dec_seq))
    ctx_part = pl.BlockSpec((tm, D_GRP), lambda i: (jnp.minimum(i, n_ctx_blk - 1), 0))
    dec_part = pl.BlockSpec((tm, D_GRP), lambda i: (jnp.maximum(i - n_ctx_blk, 0), 0))
    row = pl.BlockSpec((tm, D_MODEL), lambda i: (i, 0))
    return pl.pallas_call(
        functools.partial(_outproj_kernel, n_ctx_blk=n_ctx_blk),
        out_shape=[jax.ShapeDtypeStruct((m, D_MODEL), F32), jax.ShapeDtypeStruct((m, D_MODEL), BF16)],
        grid=(m // tm,),
        in_specs=[ctx_part] * N_MIXERS + [dec_part] * N_MIXERS
        + [pl.BlockSpec((D_MODEL, D_MODEL), lambda i: (0, 0)), row,
           modspec(2), pl.BlockSpec((1, D_MODEL), lambda i: (0, 0)), modspec(4), modspec(3)],
        out_specs=[row, row],
        compiler_params=_cparams(("arbitrary",)),
        name="out_projection",
    )(*ctx_parts, *dec_parts, w_out_bf16, x, mod, norm_g.reshape(1, D_MODEL), mod, mod)


def _ffn_kernel(h_ref, wa_ref, wb_ref, cwa_ref, cwb_ref, cba_ref, cbb_ref, wd_ref, y_ref,
                ua0_ref, ub0_ref, ua1_ref, ub1_ref, *, tile_rows, n_ctx_tok, ctx_seq, dec_seq):
    i = pl.program_id(0)
    j = pl.program_id(1)
    tf = wa_ref.shape[2]
    body = slice(SUBLANES, SUBLANES + tile_rows)

    @pl.when((i == 0) & (j == 0))
    def _():
        for ref in (ua0_ref, ub0_ref, ua1_ref, ub1_ref):
            ref[...] = jnp.zeros_like(ref)

    half = tile_rows // 2

    def park(dst, w_ref):
        dst[body, :] = jnp.dot(h_ref[...], w_ref[0].astype(BF16), preferred_element_type=F32)

    def gated(ready_a, ready_b, r0):
        seq_mask = jnp.where(i * tile_rows < n_ctx_tok, ctx_seq - 1, dec_seq - 1)
        pos = (r0 + _iota((half, tf), 0)) & seq_mask
        first, last = pos == 0, pos == seq_mask
        lo = SUBLANES + r0

        def dwconv(u_ref, cw_ref, cb_ref):
            cw = cw_ref[0]
            prev = jnp.where(first, 0.0, u_ref[lo - 1:lo - 1 + half, :])
            nxt = jnp.where(last, 0.0, u_ref[lo + 1:lo + 1 + half, :])
            return prev * cw[0:1] + u_ref[lo:lo + half, :] * cw[1:2] + nxt * cw[2:3] + cb_ref[0]

        ua = dwconv(ready_a, cwa_ref, cba_ref)
        ub = dwconv(ready_b, cwb_ref, cbb_ref)
        return (ua * _sigmoid(ua) * ub).astype(BF16)

    def step(park_a, park_b, ready_a, ready_b, first_finish):
        act0 = gated(ready_a, ready_b, 0)
        park(park_a, wa_ref)
        act1 = gated(ready_a, ready_b, half)
        park(park_b, wb_ref)
        wd = wd_ref[0].astype(BF16)
        for r0, act in ((0, act0), (half, act1)):
            contrib = jnp.dot(act, wd, preferred_element_type=F32)
            if first_finish:
                y_ref[r0:r0 + half, :] = contrib
            else:
                y_ref[r0:r0 + half, :] += contrib

    @pl.when(j == 0)
    def _():
        park(ua0_ref, wa_ref)
        park(ub0_ref, wb_ref)

    @pl.when(j == 1)
    def _():
        step(ua1_ref, ub1_ref, ua0_ref, ub0_ref, True)

    @pl.when((j > 1) & (j % 2 == 0))
    def _():
        step(ua0_ref, ub0_ref, ua1_ref, ub1_ref, False)

    @pl.when((j > 1) & (j % 2 == 1))
    def _():
        step(ua1_ref, ub1_ref, ua0_ref, ub0_ref, False)


def conv_ffn(h2, layer, w_up, conv_w, conv_b, w_down, n_ctx_tok, ctx_seq, dec_seq):
    m = h2.shape[0]
    tm, tf = 1024, 256
    nf = D_FF // tf
    cb = conv_b.reshape(DEPTH, 1, 2 * D_FF)
    up = lambda j: jnp.minimum(j, nf - 1)
    dn = lambda j: jnp.maximum(j - 1, 0)
    park = pltpu.VMEM((tm + 2 * SUBLANES, tf), F32)
    return pl.pallas_call(
        functools.partial(_ffn_kernel, tile_rows=tm, n_ctx_tok=n_ctx_tok, ctx_seq=ctx_seq, dec_seq=dec_seq),
        out_shape=jax.ShapeDtypeStruct((m, D_MODEL), F32),
        grid=(m // tm, nf + 1),
        in_specs=[pl.BlockSpec((tm, D_MODEL), lambda i, j: (i, 0)),
                  pl.BlockSpec((1, D_MODEL, tf), lambda i, j: (layer, 0, up(j))),
                  pl.BlockSpec((1, D_MODEL, tf), lambda i, j: (layer, 0, nf + up(j))),
                  pl.BlockSpec((1, 3, tf), lambda i, j: (layer, 0, dn(j))),
                  pl.BlockSpec((1, 3, tf), lambda i, j: (layer, 0, nf + dn(j))),
                  pl.BlockSpec((1, 1, tf), lambda i, j: (layer, 0, dn(j))),
                  pl.BlockSpec((1, 1, tf), lambda i, j: (layer, 0, nf + dn(j))),
                  pl.BlockSpec((1, tf, D_MODEL), lambda i, j: (layer, dn(j), 0))],
        out_specs=pl.BlockSpec((tm, D_MODEL), lambda i, j: (i, 0)),
        scratch_shapes=[park, park, park, park],
        compiler_params=_cparams(("arbitrary", "arbitrary")),
        name="conv_ffn",
    )(h2, w_up, w_up, conv_w, conv_w, cb, cb, w_down)


def _residual_kernel(x_ref, y_ref, g_ref, o_ref):
    o_ref[...] = x_ref[...] + g_ref[0] * y_ref[...]


def final_residual(x1, y, mod, layer, row0, n_rows, n_ctx_tok, dec_seq):
    tm = 512
    blk0 = row0 // tm

    def mod_index(i):
        start = (blk0 + i) * tm
        row = jnp.where(start < n_ctx_tok, 0, 1 + (start - n_ctx_tok) // dec_seq)
        return (layer * 48 + row * 6 + 5, 0, 0)

    return pl.pallas_call(
        _residual_kernel,
        out_shape=jax.ShapeDtypeStruct((n_rows, D_MODEL), F32),
        grid=(n_rows // tm,),
        in_specs=[pl.BlockSpec((tm, D_MODEL), lambda i: (blk0 + i, 0)),
                  pl.BlockSpec((tm, D_MODEL), lambda i: (blk0 + i, 0)),
                  pl.BlockSpec((1, 1, D_MODEL), mod_index)],
        out_specs=pl.BlockSpec((tm, D_MODEL), lambda i: (i, 0)),
        compiler_params=_cparams(("arbitrary",)),
        name="final_residual",
    )(x1, y, mod)


def kernel(x_prompt, x_sample, cache_diff_k, cache_diff_v, cache_swa_k, cache_swa_v, state_mlstm_C, state_mlstm_n, state_mlstm_m, c, c_ctx, w_mod, b_mod, norm1_g, norm2_g, w_in, mlstm_ig_b, mlstm_fg_b, mlstm_norm_g, diff_qn_g, diff_kn_g, diff_lam, diff_out_g, swa_qn_g, swa_kn_g, swa_sink, hy_conv_w, hy_conv_b, hy_w1, hy_b1, hy_freq, hy_w2, hy_b2, hy_w3, hy_b3, hy_bias, w_out, ffn_w_up, ffn_conv_w, ffn_conv_b, ffn_w_down):
    n_ctx, ctx_seq = x_prompt.shape[:2]
    n_dec, dec_seq = x_sample.shape[:2]
    n_ctx_tok = n_ctx * ctx_seq
    n_dec_tok = n_dec * dec_seq
    dec_blk0 = n_ctx_tok // dec_seq

    x = (x_prompt.reshape(n_ctx_tok, D_MODEL), x_sample.reshape(n_dec_tok, D_MODEL))
    w_pack, wg_hi, wg_lo = pack_in_weights(w_in)
    cond8 = jnp.concatenate([c_ctx[None, :], c, jnp.zeros((8 - 1 - n_dec, D_MODEL), F32)], axis=0)
    mod = mod_vectors(cond8, w_mod, b_mod)

    dft = {}
    for t in (ctx_seq, dec_seq):
        fwd, inv = _dft_matrices(t)
        dft[t] = dict(fwd=fwd.astype(BF16), inv=inv.astype(BF16), split=_split2(fwd))

    delta = None
    caches = []
    for l in range(DEPTH):
        x, z, gates = in_projection(x, delta, mod, l, norm1_g[l], w_pack, wg_hi, wg_lo, n_ctx_tok, dec_seq)

        gate_bias = jnp.pad(jnp.concatenate([mlstm_ig_b[l].reshape(-1), mlstm_fg_b[l].reshape(-1)]),
                            (0, LANES - 4 * H_A)).reshape(1, LANES)
        lam_init = 0.8 - 0.6 * math.exp(-0.3 * l)

        a_ctx, c_new, n_new, m_new = mlstm_mixer(z, gates, gate_bias, mlstm_norm_g[l], None,
                                                 n_ctx, ctx_seq, 0, ctx_seq, True)
        b_ctx, diff_k = diff_mixer(z, diff_qn_g[l], diff_kn_g[l], diff_lam[l], diff_out_g[l], lam_init,
                                   None, n_ctx, ctx_seq, 0)
        c_ctx_out, swa_k = swa_mixer(z, swa_qn_g[l], swa_kn_g[l], swa_sink[l], None, n_ctx, ctx_seq, 0)
        spec_ctx = hyena_filter_spectra(ctx_seq, hy_w1[l], hy_b1[l], hy_freq[l], hy_w2[l], hy_b2[l],
                                        hy_w3[l], hy_b3[l], *dft[ctx_seq]["split"])
        d_ctx = hyena_mixer(z, hy_conv_w[l], hy_conv_b[l], hy_bias[l], spec_ctx, dft[ctx_seq]["fwd"],
                            dft[ctx_seq]["inv"], n_ctx, ctx_seq, 0)
        a_dec, = mlstm_mixer(z, gates, gate_bias, mlstm_norm_g[l],
                             (state_mlstm_C[:, l], state_mlstm_n[:, l], state_mlstm_m[:, l]),
                             n_dec, dec_seq, dec_blk0, 512, False)
        b_dec, = diff_mixer(z, diff_qn_g[l], diff_kn_g[l], diff_lam[l], diff_out_g[l], lam_init,
                            (cache_diff_k[:, l], cache_diff_v[:, l]), n_dec, dec_seq, dec_blk0)
        c_dec, = swa_mixer(z, swa_qn_g[l], swa_kn_g[l], swa_sink[l],
                           (cache_swa_k[:, l], cache_swa_v[:, l]), n_dec, dec_seq, dec_blk0)
        spec_dec = hyena_filter_spectra(dec_seq, hy_w1[l], hy_b1[l], hy_freq[l], hy_w2[l], hy_b2[l],
                                        hy_w3[l], hy_b3[l], *dft[dec_seq]["split"])
        d_dec = hyena_mixer(z, hy_conv_w[l], hy_conv_b[l], hy_bias[l], spec_dec, dft[dec_seq]["fwd"],
                            dft[dec_seq]["inv"], n_dec, dec_seq, dec_blk0)

        x, h2 = out_projection((a_ctx, b_ctx, c_ctx_out, d_ctx), (a_dec, b_dec, c_dec, d_dec),
                               w_out[l].astype(BF16), x, mod, l, norm2_g[l], n_ctx_tok, dec_seq)
        delta = conv_ffn(h2, l, ffn_w_up, ffn_conv_w, ffn_conv_b, ffn_w_down, n_ctx_tok, ctx_seq, dec_seq)

        zc = z[:n_ctx_tok]
        caches.append((
            diff_k.reshape(n_ctx, ctx_seq, H_B, 2, DK_B),
            zc[:, ZB_V * LANES:(ZB_V + 4) * LANES].reshape(n_ctx, ctx_seq, H_B, DV_B),
            swa_k.reshape(n_ctx, ctx_seq, KV_C, DH_C),
            zc[:, ZC_V * LANES:(ZC_V + 2) * LANES].reshape(n_ctx, ctx_seq, KV_C, DH_C),
            c_new,
            n_new.reshape(n_ctx, 2, H_A, DH_A),
            m_new.reshape(n_ctx, 2, H_A),
        ))

    y_prompt = final_residual(x, delta, mod, DEPTH - 1, 0, n_ctx_tok, n_ctx_tok, dec_seq)
    y_sample = final_residual(x, delta, mod, DEPTH - 1, n_ctx_tok, n_dec_tok, n_ctx_tok, dec_seq)
    outs = [jnp.stack([s[i] for s in caches], axis=1) for i in range(7)]
    return (y_prompt.reshape(n_ctx, ctx_seq, D_MODEL), y_sample.reshape(n_dec, dec_seq, D_MODEL), *outs)
```

```python
import functools
import math

import jax
import jax.numpy as jnp
from jax import lax
from jax.experimental import pallas as pl
from jax.experimental.pallas import tpu as pltpu

D_MODEL = 2048
DEPTH = 2
GRID_W = 64
D_GRP = 512
N_MIXERS = D_MODEL // D_GRP
H_A = 4
DH_A = 128
H_B = 4
DV_B = 128
DK_B = 64
H_C = 4
KV_C = 2
G_C = 2
DH_C = 128
WINDOW = 128
HY_ORDER = 2
HY_POS_DIM = 33
HY_BANDS = (HY_POS_DIM - 1) // 2
HY_FF = 64
HY_FAST = math.log(1e-2) / 0.3
HY_SLOW = math.log(1e-2) / 1.5
D_FF = 5632
ROPE_BASE = 10000.0
EPS = 1e-6
N_A = 4 * D_GRP + 4 * H_A

LANES = 128
SUBLANES = 8
V7X_VMEM_LIMIT_BYTES = 56 * 1024 * 1024
Z_COLS = 12 * D_GRP
INPROJ_TN = 3 * D_GRP
FFN_TF = 256
ZA_Q, ZA_K, ZA_V, ZA_O = 0, 4, 8, 12
ZB_Q, ZB_K, ZB_V = 16, 20, 24
ZC_Q, ZC_K, ZC_V = 28, 32, 34
ZD_Y, ZD_X1, ZD_X2 = 36, 40, 44

BF16 = jnp.bfloat16
F32 = jnp.float32


def _cparams(sem):
    return pltpu.CompilerParams(dimension_semantics=sem, vmem_limit_bytes=V7X_VMEM_LIMIT_BYTES)


def _dot(a, b):
    return jnp.dot(a.astype(BF16), b.astype(BF16), preferred_element_type=F32)


def _dot_nt(a, b):
    return lax.dot_general(a.astype(BF16), b.astype(BF16), (((1,), (1,)), ((), ())),
                           preferred_element_type=F32)


def _bf16_head(a):
    bits = lax.bitcast_convert_type(a, jnp.int32) & jnp.int32(-65536)
    return lax.bitcast_convert_type(bits, F32)


def _split2(a):
    head = _bf16_head(a)
    return head.astype(BF16), (a - head).astype(BF16)


def _split3(a):
    head = _bf16_head(a)
    rest = a - head
    mid = _bf16_head(rest)
    return head.astype(BF16), mid.astype(BF16), (rest - mid).astype(BF16)


def _dot3(a, b):
    ah, al = _split2(a)
    bh, bl = _split2(b)
    return (jnp.dot(ah, bh, preferred_element_type=F32) + jnp.dot(al, bh, preferred_element_type=F32)
            + jnp.dot(ah, bl, preferred_element_type=F32))


def _iota(shape, dim):
    return lax.broadcasted_iota(jnp.int32, shape, dim)


def _sigmoid(x):
    return 1.0 / (1.0 + jnp.exp(-x))


def _log_sigmoid(x):
    return jnp.minimum(x, 0.0) - jnp.log(1.0 + jnp.exp(-jnp.abs(x)))


def _mod_kernel(c_ref, w_ref, b_ref, o_ref):
    k = pl.program_id(1)
    c = c_ref[...]
    part = _dot(c * _sigmoid(c), w_ref[0])

    @pl.when(k == 0)
    def _():
        o_ref[0] = part + b_ref[0]

    @pl.when(k > 0)
    def _():
        o_ref[0] += part


def mod_vectors(cond8, w_mod, b_mod):
    tk = 256
    n = w_mod.shape[2]
    out = pl.pallas_call(
        _mod_kernel,
        out_shape=jax.ShapeDtypeStruct((DEPTH, 8, n), F32),
        grid=(DEPTH, D_MODEL // tk),
        in_specs=[pl.BlockSpec((8, tk), lambda l, k: (0, k)),
                  pl.BlockSpec((1, tk, n), lambda l, k: (l, k, 0)),
                  pl.BlockSpec((1, 1, n), lambda l, k: (l, 0, 0))],
        out_specs=pl.BlockSpec((1, 8, n), lambda l, k: (l, 0, 0)),
        compiler_params=_cparams(("arbitrary", "arbitrary")),
        name="mod_vectors",
    )(cond8, w_mod, b_mod.reshape(DEPTH, 1, n))
    return out.reshape(DEPTH * 8 * 6, 1, D_MODEL)


def _mod_row_index(layer, part, tile_rows, n_ctx_tok, dec_seq):
    def index(i, *_):
        start = i * tile_rows
        row = jnp.where(start < n_ctx_tok, 0, 1 + (start - n_ctx_tok) // dec_seq)
        return (layer * 48 + row * 6 + part, 0, 0)
    return index


def _ada_norm(x, ng, sc, sh):
    y = x * lax.rsqrt(jnp.mean(x * x, axis=-1, keepdims=True) + EPS)
    return y * ng * (1.0 + sc) + sh


def _pack_in_kernel(a_ref, b_ref, w_ref, gh_ref, gl_ref):
    c = pl.program_id(1)
    n_aligned = 4 * D_GRP // a_ref.shape[1]
    gap = 4 * H_A

    @pl.when(c < n_aligned)
    def _():
        w_ref[0, 0] = a_ref[0].T.astype(BF16)

    @pl.when(c >= n_aligned)
    def _():
        w_ref[0, 0] = jnp.concatenate([a_ref[0, gap:, :], b_ref[0]], axis=0).T.astype(BF16)

    @pl.when(c == n_aligned)
    def _():
        rows = a_ref[0, 0:LANES, :]
        hi, lo = _split2(jnp.where(_iota(rows.shape, 0) < gap, rows, 0.0).T)
        gh_ref[0] = hi
        gl_ref[0] = lo


def pack_in_weights(w_in):
    tc = 512
    gap = 4 * H_A
    n_blk = Z_COLS // tc
    per_tile = INPROJ_TN // tc
    w_t = jnp.swapaxes(w_in, 1, 2)
    gate = pl.BlockSpec((1, D_MODEL, LANES), lambda l, c: (l, 0, 0))
    return pl.pallas_call(
        _pack_in_kernel,
        out_shape=[jax.ShapeDtypeStruct((DEPTH, Z_COLS // INPROJ_TN, D_MODEL, INPROJ_TN), BF16),
                   jax.ShapeDtypeStruct((DEPTH, D_MODEL, LANES), BF16),
                   jax.ShapeDtypeStruct((DEPTH, D_MODEL, LANES), BF16)],
        grid=(DEPTH, n_blk),
        in_specs=[pl.BlockSpec((1, tc, D_MODEL), lambda l, c: (l, c, 0)),
                  pl.BlockSpec((1, gap, D_MODEL), lambda l, c: (l, (c + 1) * (tc // gap), 0))],
        out_specs=[pl.BlockSpec((1, 1, D_MODEL, tc), lambda l, c: (l, c // per_tile, 0, c % per_tile)),
                   gate, gate],
        compiler_params=_cparams(("arbitrary", "arbitrary")),
        name="pack_in_weights",
    )(w_t, w_t)


def _inproj_kernel(*refs, has_delta, n_ctx_blk):
    if has_delta:
        (x_ref, y_ref, g2_ref, ng_ref, sc_ref, sh_ref, w_ref, wgh_ref, wgl_ref,
         xn_ref, z_ref, gate_ref, hb_ref) = refs
    else:
        (xc_ref, xd_ref, ng_ref, sc_ref, sh_ref, w_ref, wgh_ref, wgl_ref,
         xn_ref, z_ref, gate_ref, hb_ref) = refs

    i = pl.program_id(0)

    @pl.when(pl.program_id(1) == 0)
    def _():
        if has_delta:
            x = x_ref[...] + g2_ref[0] * y_ref[...]
        else:
            x = jnp.where(i < n_ctx_blk, xc_ref[...], xd_ref[...])
        xn_ref[...] = x
        h = _ada_norm(x, ng_ref[...], sc_ref[0], sh_ref[0])
        hb_ref[...] = h.astype(BF16)
        hh, hl = _split2(h)
        wgh = wgh_ref[0]
        gate_ref[...] = (jnp.dot(hh, wgh, preferred_element_type=F32)
                         + jnp.dot(hl, wgh, preferred_element_type=F32)
                         + jnp.dot(hh, wgl_ref[0], preferred_element_type=F32))

    z_ref[...] = jnp.dot(hb_ref[...], w_ref[0, 0], preferred_element_type=F32)


def in_projection(x, delta, mod, layer, norm_g, w_pack, wg_hi, wg_lo, n_ctx_tok, dec_seq):
    tm, tn = 512, INPROJ_TN
    has_delta = delta is not None
    n_ctx_blk = n_ctx_tok // tm
    row = pl.BlockSpec((tm, D_MODEL), lambda i, j: (i, 0))
    modspec = lambda part: pl.BlockSpec((1, 1, D_MODEL), _mod_row_index(layer, part, tm, n_ctx_tok, dec_seq))
    vec = pl.BlockSpec((1, D_MODEL), lambda i, j: (0, 0))
    if has_delta:
        m = x.shape[0]
        g2spec = pl.BlockSpec((1, 1, D_MODEL), _mod_row_index(layer - 1, 5, tm, n_ctx_tok, dec_seq))
        in_specs = [row, row, g2spec]
        args = [x, delta, mod]
    else:
        x_ctx, x_dec = x
        m = x_ctx.shape[0] + x_dec.shape[0]
        in_specs = [pl.BlockSpec((tm, D_MODEL), lambda i, j: (jnp.minimum(i, n_ctx_blk - 1), 0)),
                    pl.BlockSpec((tm, D_MODEL), lambda i, j: (jnp.maximum(i - n_ctx_blk, 0), 0))]
        args = [x_ctx, x_dec]
    gate_w = pl.BlockSpec((1, D_MODEL, LANES), lambda i, j: (layer, 0, 0))
    in_specs += [vec, modspec(1), modspec(0),
                 pl.BlockSpec((1, 1, D_MODEL, tn), lambda i, j: (layer, j, 0, 0)), gate_w, gate_w]
    args += [norm_g.reshape(1, D_MODEL), mod, mod, w_pack, wg_hi, wg_lo]
    return pl.pallas_call(
        functools.partial(_inproj_kernel, has_delta=has_delta, n_ctx_blk=n_ctx_blk),
        out_shape=[jax.ShapeDtypeStruct((m, D_MODEL), F32), jax.ShapeDtypeStruct((m, Z_COLS), F32),
                   jax.ShapeDtypeStruct((m, LANES), F32)],
        grid=(m // tm, Z_COLS // tn),
        in_specs=in_specs,
        out_specs=[row, pl.BlockSpec((tm, tn), lambda i, j: (i, j)), pl.BlockSpec((tm, LANES), lambda i, j: (i, 0))],
        scratch_shapes=[pltpu.VMEM((tm, D_MODEL), BF16)],
        compiler_params=_cparams(("arbitrary", "arbitrary")),
        name="in_projection",
    )(*args)


def _mlstm_kernel(*refs, n_tok, chunk, has_state, want_state):
    refs = list(refs)
    q_ref, k_ref, v_ref, o_ref, gate_ref, gb_ref, ng_ref = refs[:7]
    pos = 7
    if has_state:
        c0_ref, n0_ref, m0_ref = refs[pos:pos + 3]
        pos += 3
    out_ref = refs[pos]
    pos += 1
    if want_state:
        cn_ref, nn_ref, mn_ref = refs[pos:pos + 3]
        pos += 3
    hacc_ref = refs[pos]

    L = chunk
    nc = n_tok // L
    scale = DH_A ** -0.5
    r_idx = _iota((L, L), 0)
    c_idx = _iota((L, L), 1)
    ones_b = jnp.ones((L, DH_A), BF16)

    g = gate_ref[...] + gb_ref[...]
    vals = jnp.where(_iota((n_tok, LANES), 1) < 2 * H_A, g, _log_sigmoid(g))

    for d in range(2):
        causal = (c_idx <= r_idx) if d == 0 else (c_idx >= r_idx)
        tri = jnp.where(causal, 1.0, 0.0).astype(BF16)
        if has_state:
            c_mem = [c0_ref[0, d, h] for h in range(H_A)]
            n_mem = [n0_ref[0, d, h] for h in range(H_A)]
            m_prev = [m0_ref[0, d, h] for h in range(H_A)]
            pad = jnp.zeros((SUBLANES - 1, DH_A), F32)
            n_mat = [_dot(jnp.concatenate([n, pad], axis=0).T, jnp.ones((SUBLANES, DH_A), BF16)) for n in n_mem]
        else:
            c_mem = [None] * H_A
            n_mem = [None] * H_A
            n_mat = [None] * H_A
            m_prev = [jnp.zeros((1, 1), F32)] * H_A
        order = range(nc) if d == 0 else range(nc - 1, -1, -1)
        for ci, c in enumerate(order):
            rows = slice(c * L, (c + 1) * L)
            v_c = vals[rows]
            p0, p1, p2 = _split3(v_c)
            cum = (jnp.dot(tri, p0, preferred_element_type=F32) + jnp.dot(tri, p1, preferred_element_type=F32)
                   + jnp.dot(tri, p2, preferred_element_type=F32))
            v_t = v_c.T
            cum_t = cum.T
            for h in range(H_A):
                col_i = d * H_A + h
                col_f = 2 * H_A + d * H_A + h
                lanes = slice(h * DH_A, (h + 1) * DH_A)
                c_col = cum[:, col_f:col_f + 1]
                ig_col = v_c[:, col_i:col_i + 1]
                c_row = cum_t[col_f:col_f + 1, :]
                ig_row = v_t[col_i:col_i + 1, :]

                dmat = jnp.where(causal, c_col - c_row + ig_row, -jnp.inf)
                inter = c_col + m_prev[h]
                m_col = jnp.maximum(inter, jnp.max(dmat, axis=1, keepdims=True))
                dexp = jnp.exp(dmat - m_col)
                qs = q_ref[rows, lanes] * scale
                qb = qs.astype(BF16)
                kf = k_ref[rows, lanes]
                vb = v_ref[rows, lanes].astype(BF16)
                pb = (_dot_nt(qb, kf) * dexp).astype(BF16)
                num = jnp.dot(pb, vb, preferred_element_type=F32)
                den = jnp.dot(pb, ones_b, preferred_element_type=F32)
                carried = c_mem[h] is not None
                if carried:
                    w_inter = jnp.broadcast_to(jnp.exp(inter - m_col), (L, DH_A))
                    num = num + w_inter * _dot(qb, c_mem[h])
                    den = den + w_inter * _dot(qb, n_mat[h])
                hh = num / jnp.maximum(jnp.abs(den), jnp.exp(-m_col))
                if d == 0:
                    hacc_ref[rows, lanes] = hh
                else:
                    hacc_ref[rows, lanes] += hh

                if want_state or ci < nc - 1:
                    total = c_col[L - 1:L, :] if d == 0 else c_col[0:1, :]
                    w_row = total - c_row + ig_row
                    m_new = jnp.maximum(total + m_prev[h], jnp.max(w_row, axis=1, keepdims=True))
                    decay = jnp.exp(total + m_prev[h] - m_new)
                    wk = jnp.exp(total - c_col + ig_col - m_new)
                    kw = kf * wk
                    kw_t = kw.T.astype(BF16)
                    c_upd = jnp.dot(kw_t, vb, preferred_element_type=F32)
                    c_mem[h] = decay * c_mem[h] + c_upd if carried else c_upd
                    if ci < nc - 1:
                        n_upd = jnp.dot(kw_t, ones_b, preferred_element_type=F32)
                        n_mat[h] = decay * n_mat[h] + n_upd if carried else n_upd
                    if want_state:
                        n_row = jnp.sum(kw, axis=0, keepdims=True)
                        n_mem[h] = decay * n_mem[h] + n_row if carried else n_row
                    m_prev[h] = m_new
        if want_state:
            for h in range(H_A):
                cn_ref[0, d, h] = c_mem[h]
                nn_ref[0, d, h] = n_mem[h]
                mn_ref[0, d, h] = m_prev[h]

    for h in range(H_A):
        lanes = slice(h * DH_A, (h + 1) * DH_A)
        hs = hacc_ref[:, lanes]
        y = hs * lax.rsqrt(jnp.mean(hs * hs, axis=1, keepdims=True) + EPS) * ng_ref[:, lanes]
        out_ref[:, lanes] = (y * _sigmoid(o_ref[:, lanes])).astype(BF16)


def mlstm_mixer(z, gates, gate_bias, norm_g, state, n_seq, n_tok, row_blk0, chunk, want_state):
    has_state = state is not None
    col = lambda base: pl.BlockSpec((n_tok, D_GRP), lambda b: (row_blk0 + b, base * LANES // D_GRP))
    in_specs = [col(ZA_Q), col(ZA_K), col(ZA_V), col(ZA_O),
                pl.BlockSpec((n_tok, LANES), lambda b: (row_blk0 + b, 0)),
                pl.BlockSpec((1, LANES), lambda b: (0, 0)),
                pl.BlockSpec((1, D_GRP), lambda b: (0, 0))]
    args = [z, z, z, z, gates, gate_bias, norm_g.reshape(1, D_GRP)]
    st_specs = [pl.BlockSpec((1, 2, H_A, DH_A, DH_A), lambda b: (b, 0, 0, 0, 0)),
                pl.BlockSpec((1, 2, H_A, 1, DH_A), lambda b: (b, 0, 0, 0, 0)),
                pl.BlockSpec((1, 2, H_A, 1, 1), lambda b: (b, 0, 0, 0, 0))]
    if has_state:
        c0, n0, m0 = state
        in_specs += st_specs
        args += [c0, n0.reshape(n_seq, 2, H_A, 1, DH_A), m0.reshape(n_seq, 2, H_A, 1, 1)]
    out_shape = [jax.ShapeDtypeStruct((n_seq * n_tok, D_GRP), BF16)]
    out_specs = [pl.BlockSpec((n_tok, D_GRP), lambda b: (b, 0))]
    if want_state:
        out_shape += [jax.ShapeDtypeStruct((n_seq, 2, H_A, DH_A, DH_A), F32),
                      jax.ShapeDtypeStruct((n_seq, 2, H_A, 1, DH_A), F32),
                      jax.ShapeDtypeStruct((n_seq, 2, H_A, 1, 1), F32)]
        out_specs += st_specs
    outs = pl.pallas_call(
        functools.partial(_mlstm_kernel, n_tok=n_tok, chunk=chunk, has_state=has_state, want_state=want_state),
        out_shape=out_shape,
        grid=(n_seq,),
        in_specs=in_specs,
        out_specs=out_specs,
        scratch_shapes=[pltpu.VMEM((n_tok, D_GRP), F32)],
        compiler_params=_cparams(("arbitrary",)),
        name="mlstm_mixer",
    )(*args)
    return outs


def _rope_tables(n_tok, dh):
    rows = n_tok // GRID_W
    t_row = jnp.repeat(jnp.arange(rows, dtype=F32), GRID_W)
    t_col = jnp.tile(jnp.arange(GRID_W, dtype=F32), rows)
    n_freq = dh // 4
    inv = ROPE_BASE ** (-jnp.arange(n_freq, dtype=F32) / n_freq)
    ang = jnp.concatenate([t_row[:, None] * inv, t_col[:, None] * inv], axis=-1)
    cos, sin = jnp.cos(ang), jnp.sin(ang)
    reps = LANES // dh
    cos_t = jnp.tile(jnp.concatenate([cos, cos], axis=-1), (1, reps))
    sin_t = jnp.tile(jnp.concatenate([-sin, sin], axis=-1), (1, reps))
    return cos_t, sin_t


def _rope(x, cos_t, sin_t, dh):
    half = dh // 2
    if dh == LANES:
        partner = pltpu.roll(x, half, 1)
    else:
        first = (_iota(x.shape, 1) & half) == 0
        partner = jnp.where(first, pltpu.roll(x, LANES - half, 1), pltpu.roll(x, half, 1))
    return x * cos_t + partner * sin_t


def _rms(x, g):
    return x * lax.rsqrt(jnp.mean(x * x, axis=1, keepdims=True) + EPS) * g


def _diff_kernel(*refs, n_tok, q_blk, lam_init, has_ctx):
    refs = list(refs)
    q_ref, k_ref, v_ref, qg_ref, kg_ref, lam_ref, og_ref = refs[:7]
    pos = 7
    if has_ctx:
        ck_ref, cv_ref, cos_ref, sin_ref = refs[pos:pos + 4]
        pos += 4
    out_ref = refs[pos]
    pos += 1
    if not has_ctx:
        kout_ref = refs[pos]

    grp0 = _iota((n_tok, LANES), 1) < DK_B

    def group_norm(x, g):
        x2 = x * x
        s0 = jnp.sum(jnp.where(grp0, x2, 0.0), axis=1, keepdims=True)
        s1 = jnp.sum(jnp.where(grp0, 0.0, x2), axis=1, keepdims=True)
        ms = jnp.where(grp0, s0, s1) * (1.0 / DK_B)
        return x * lax.rsqrt(ms + EPS) * g

    qn = group_norm(q_ref[...], qg_ref[...])
    kn = group_norm(k_ref[...], kg_ref[...])
    if has_ctx:
        cos_t, sin_t = cos_ref[...], sin_ref[...]
        qn = _rope(qn, cos_t, sin_t, DK_B)
        kn = _rope(kn, cos_t, sin_t, DK_B)
    else:
        kout_ref[...] = kn
    kb = kn.astype(BF16)
    vb = v_ref[...].astype(BF16)
    if has_ctx:
        ckb = ck_ref[0].astype(BF16)
        cvb = cv_ref[0].astype(BF16)

    lp = lam_ref[...]
    lam = (jnp.exp(jnp.sum(lp[0:1] * lp[1:2], axis=1, keepdims=True))
           - jnp.exp(jnp.sum(lp[2:3] * lp[3:4], axis=1, keepdims=True)) + lam_init)
    scale = DK_B ** -0.5
    grp0_q = _iota((q_blk, LANES), 1) < DK_B

    for qi in range(n_tok // q_blk):
        rows = slice(qi * q_blk, (qi + 1) * q_blk)
        q_rows = qn[rows]
        outs = []
        for comp in range(2):
            keep = grp0_q if comp == 0 else jnp.logical_not(grp0_q)
            qm = jnp.where(keep, q_rows, 0.0).astype(BF16)
            s = _dot_nt(qm, kb) * scale
            mx = jnp.max(s, axis=1, keepdims=True)
            if has_ctx:
                sc = _dot_nt(qm, ckb) * scale
                mx = jnp.maximum(mx, jnp.max(sc, axis=1, keepdims=True))
            p = jnp.exp(s - mx)
            den = jnp.sum(p, axis=1, keepdims=True)
            acc = _dot(p, vb)
            if has_ctx:
                pc = jnp.exp(sc - mx)
                den = den + jnp.sum(pc, axis=1, keepdims=True)
                acc = acc + _dot(pc, cvb)
            outs.append(acc / den)
        a = outs[0] - lam * outs[1]
        out_ref[rows, :] = (_rms(a, og_ref[...]) * (1.0 - lam_init)).astype(BF16)


def diff_mixer(z, qn_g, kn_g, lam, out_g, lam_init, ctx, n_seq, n_tok, row_blk0):
    has_ctx = ctx is not None
    col = lambda base: pl.BlockSpec((n_tok, LANES), lambda b, h: (row_blk0 + b, base + h))
    vec = pl.BlockSpec((1, LANES), lambda b, h: (0, 0))
    in_specs = [col(ZB_Q), col(ZB_K), col(ZB_V), vec, vec,
                pl.BlockSpec((4, LANES), lambda b, h: (0, 0)), vec]
    args = [z, z, z, jnp.tile(qn_g, 2).reshape(1, LANES), jnp.tile(kn_g, 2).reshape(1, LANES),
            jnp.pad(lam, ((0, 0), (0, LANES - DK_B))), out_g.reshape(1, LANES)]
    if has_ctx:
        ck, cv = ctx
        s_ctx = ck.shape[1]
        cos_t, sin_t = _rope_tables(n_tok, DK_B)
        tab = pl.BlockSpec((n_tok, LANES), lambda b, h: (0, 0))
        cspec = pl.BlockSpec((1, s_ctx, LANES), lambda b, h: (b, 0, h))
        in_specs += [cspec, cspec, tab, tab]
        args += [ck.reshape(n_seq, s_ctx, H_B * 2 * DK_B), cv.reshape(n_seq, s_ctx, H_B * DV_B), cos_t, sin_t]
    out_shape = [jax.ShapeDtypeStruct((n_seq * n_tok, D_GRP), BF16)]
    out_specs = [pl.BlockSpec((n_tok, LANES), lambda b, h: (b, h))]
    if not has_ctx:
        out_shape.append(jax.ShapeDtypeStruct((n_seq * n_tok, D_GRP), F32))
        out_specs.append(pl.BlockSpec((n_tok, LANES), lambda b, h: (b, h)))
    return pl.pallas_call(
        functools.partial(_diff_kernel, n_tok=n_tok, q_blk=min(n_tok, 256), lam_init=lam_init, has_ctx=has_ctx),
        out_shape=out_shape,
        grid=(n_seq, H_B),
        in_specs=in_specs,
        out_specs=out_specs,
        compiler_params=_cparams(("arbitrary", "arbitrary")),
        name="diff_mixer",
    )(*args)


def _swa_kernel(*refs, n_tok, q_blk, has_ctx):
    refs = list(refs)
    sink_ref, q_ref, k_ref, v_ref, qg_ref, kg_ref = refs[:6]
    pos = 6
    if has_ctx:
        ck_ref, cv_ref, cos_ref, sin_ref = refs[pos:pos + 4]
        pos += 4
    out_ref = refs[pos]
    pos += 1
    if not has_ctx:
        kout_ref = refs[pos]

    kv = pl.program_id(1)
    kn = _rms(k_ref[...], kg_ref[...])
    if has_ctx:
        cos_t, sin_t = cos_ref[...], sin_ref[...]
        kn = _rope(kn, cos_t, sin_t, DH_C)
        ckb = ck_ref[0].astype(BF16)
        cvb = cv_ref[0].astype(BF16)
    else:
        kout_ref[...] = kn
    kb = kn.astype(BF16)
    vb = v_ref[...].astype(BF16)
    scale = DH_C ** -0.5

    for g in range(G_C):
        sink = sink_ref[kv * G_C + g]
        qn = _rms(q_ref[:, g * DH_C:(g + 1) * DH_C], qg_ref[...])
        if has_ctx:
            qn = _rope(qn, cos_t, sin_t, DH_C)
        for qi in range(n_tok // q_blk):
            rows = slice(qi * q_blk, (qi + 1) * q_blk)
            qb = qn[rows].astype(BF16)
            s = _dot_nt(qb, kb) * scale
            if has_ctx:
                qpos = qi * q_blk + _iota((q_blk, n_tok), 0)
                kpos = _iota((q_blk, n_tok), 1)
                s = jnp.where(jnp.abs(kpos - qpos) <= WINDOW, s, -jnp.inf)
            mx = jnp.maximum(jnp.max(s, axis=1, keepdims=True), sink)
            if has_ctx:
                sc = _dot_nt(qb, ckb) * scale
                mx = jnp.maximum(mx, jnp.max(sc, axis=1, keepdims=True))
            p = jnp.exp(s - mx)
            den = jnp.sum(p, axis=1, keepdims=True) + jnp.exp(sink - mx)
            acc = _dot(p, vb)
            if has_ctx:
                pc = jnp.exp(sc - mx)
                den = den + jnp.sum(pc, axis=1, keepdims=True)
                acc = acc + _dot(pc, cvb)
            out_ref[rows, g * DH_C:(g + 1) * DH_C] = (acc / den).astype(BF16)


def swa_mixer(z, qn_g, kn_g, sink, ctx, n_seq, n_tok, row_blk0):
    has_ctx = ctx is not None
    vec = pl.BlockSpec((1, LANES), lambda b, kv: (0, 0))
    in_specs = [pl.BlockSpec(memory_space=pltpu.SMEM),
                pl.BlockSpec((n_tok, G_C * DH_C), lambda b, kv: (row_blk0 + b, ZC_Q // G_C + kv)),
                pl.BlockSpec((n_tok, LANES), lambda b, kv: (row_blk0 + b, ZC_K + kv)),
                pl.BlockSpec((n_tok, LANES), lambda b, kv: (row_blk0 + b, ZC_V + kv)),
                vec, vec]
    args = [sink.astype(F32), z, z, z, qn_g.reshape(1, LANES), kn_g.reshape(1, LANES)]
    if has_ctx:
        ck, cv = ctx
        s_ctx = ck.shape[1]
        cos_t, sin_t = _rope_tables(n_tok, DH_C)
        tab = pl.BlockSpec((n_tok, LANES), lambda b, kv: (0, 0))
        cspec = pl.BlockSpec((1, s_ctx, LANES), lambda b, kv: (b, 0, kv))
        in_specs += [cspec, cspec, tab, tab]
        args += [ck.reshape(n_seq, s_ctx, KV_C * DH_C), cv.reshape(n_seq, s_ctx, KV_C * DH_C), cos_t, sin_t]
    out_shape = [jax.ShapeDtypeStruct((n_seq * n_tok, D_GRP), BF16)]
    out_specs = [pl.BlockSpec((n_tok, G_C * DH_C), lambda b, kv: (b, kv))]
    if not has_ctx:
        out_shape.append(jax.ShapeDtypeStruct((n_seq * n_tok, KV_C * DH_C), F32))
        out_specs.append(pl.BlockSpec((n_tok, LANES), lambda b, kv: (b, kv)))
    return pl.pallas_call(
        functools.partial(_swa_kernel, n_tok=n_tok, q_blk=min(n_tok, 256), has_ctx=has_ctx),
        out_shape=out_shape,
        grid=(n_seq, KV_C),
        in_specs=in_specs,
        out_specs=out_specs,
        compiler_params=_cparams(("arbitrary", "arbitrary")),
        name="swa_mixer",
    )(*args)


def _dft_matrices(n_tok):
    k = jnp.arange(n_tok, dtype=jnp.int32)[:, None]
    s = jnp.arange(n_tok, dtype=jnp.int32)[None, :]
    ang = (math.pi / n_tok) * ((k * s) % (2 * n_tok)).astype(F32)
    cos, sin = jnp.cos(ang), jnp.sin(ang)
    alt_s = jnp.where(s % 2 == 0, 1.0, -1.0).astype(F32)
    fwd = jnp.concatenate([cos, jnp.where(k == 0, alt_s, -sin)], axis=0)
    inv_re = jnp.where(s == 0, 1.0, 2.0 * cos) / (2 * n_tok)
    alt_t = jnp.where(k % 2 == 0, 1.0, -1.0).astype(F32)
    inv_im = jnp.where(s == 0, alt_t, -2.0 * sin) / (2 * n_tok)
    inv = jnp.concatenate([inv_re, inv_im], axis=1)
    return fwd, inv


def _hyena_filter_kernel(feat_ref, w1_ref, b1_ref, fr_ref, w2_ref, b2_ref, w3_ref, b3_ref, win_ref,
                         fh_ref, fl_ref, g_ref, *, n_tok):
    fr = fr_ref[...]
    h = jnp.sin(fr * (_dot3(feat_ref[...], w1_ref[...]) + b1_ref[...]))
    h = jnp.sin(fr * (_dot3(h, w2_ref[...]) + b2_ref[...]))
    f = _dot3(h, w3_ref[...]) + b3_ref[...]
    win = win_ref[...]
    h_fwd = f[:, :D_GRP] * win
    h_bwd = jnp.where(_iota((n_tok, D_GRP), 0) == 0, 0.0, f[:, D_GRP:] * win)
    fh, fl = fh_ref[...], fl_ref[...]

    def spectrum(x):
        xh, xl = _split2(x)
        return (jnp.dot(fh, xh, preferred_element_type=F32) + jnp.dot(fl, xh, preferred_element_type=F32)
                + jnp.dot(fh, xl, preferred_element_type=F32))

    conj_sign = jnp.where(_iota((2 * n_tok, D_GRP), 0) <= n_tok, 1.0, -1.0)
    g_ref[0] = spectrum(h_fwd) + conj_sign * spectrum(h_bwd)


def hyena_filter_spectra(n_tok, w1, b1, freq, w2, b2, w3, b3, fwd_hi, fwd_lo):
    t = jnp.linspace(0.0, 1.0, n_tok, dtype=F32)[:, None]
    w = (2.0 * math.pi / n_tok) * jnp.arange(n_tok, dtype=F32)[:, None]
    bands = jnp.linspace(1e-4, HY_BANDS - 1, HY_BANDS, dtype=F32)[None, :]
    feats = jnp.concatenate([t, jnp.cos(bands * w), -jnp.sin(bands * w)], axis=-1)
    feats = jnp.pad(feats, ((0, 0), (0, LANES - HY_POS_DIM)))
    rates = jnp.abs(jnp.linspace(HY_FAST, HY_SLOW, D_GRP, dtype=F32))
    window = jnp.exp(-t * rates)
    padv = lambda a: jnp.pad(a, (0, LANES - HY_FF)).reshape(1, LANES)
    w1p = jnp.pad(w1, ((0, LANES - HY_POS_DIM), (0, LANES - HY_FF)))
    w2p = jnp.pad(w2, ((0, LANES - HY_FF), (0, LANES - HY_FF)))
    w3p = jnp.pad(w3, ((0, LANES - HY_FF), (0, 0)))
    full = lambda shape: pl.BlockSpec(shape, lambda o: (0,) * len(shape))
    return pl.pallas_call(
        functools.partial(_hyena_filter_kernel, n_tok=n_tok),
        out_shape=jax.ShapeDtypeStruct((HY_ORDER, 2 * n_tok, D_GRP), F32),
        grid=(HY_ORDER,),
        in_specs=[full((n_tok, LANES)), full((LANES, LANES)), full((1, LANES)), full((1, LANES)),
                  full((LANES, LANES)), full((1, LANES)),
                  pl.BlockSpec((LANES, 2 * D_GRP), lambda o: (0, o)),
                  pl.BlockSpec((1, 2 * D_GRP), lambda o: (0, o)),
                  full((n_tok, D_GRP)), full((2 * n_tok, n_tok)), full((2 * n_tok, n_tok))],
        out_specs=pl.BlockSpec((1, 2 * n_tok, D_GRP), lambda o: (o, 0, 0)),
        compiler_params=_cparams(("arbitrary",)),
        name="hyena_filter_spectra",
    )(feats, w1p, padv(b1), padv(freq), w2p, padv(b2), w3p, b3.reshape(1, -1), window, fwd_hi, fwd_lo)


def _hyena_conv_kernel(zy_ref, zx1_ref, zx2_ref, cwy_ref, cw1_ref, cw2_ref, cby_ref, cb1_ref, cb2_ref,
                       skip_ref, g_ref, fwd_ref, inv_ref, out_ref, *, n_tok):
    width = zy_ref.shape[1]
    row = _iota((n_tok, width), 0)
    first, last = row == 0, row == n_tok - 1

    def dwconv(x_ref, w_ref, b_ref):
        x = x_ref[...]
        w = w_ref[...]
        prev = jnp.where(first, 0.0, pltpu.roll(x, 1, 0))
        nxt = jnp.where(last, 0.0, pltpu.roll(x, n_tok - 1, 0))
        return prev * w[0:1] + x * w[1:2] + nxt * w[2:3] + b_ref[...]

    y = dwconv(zy_ref, cwy_ref, cby_ref)
    gates = (dwconv(zx1_ref, cw1_ref, cb1_ref), dwconv(zx2_ref, cw2_ref, cb2_ref))
    fwd = fwd_ref[...]
    for o in range(HY_ORDER):
        u = jnp.dot(fwd, y.astype(BF16), preferred_element_type=F32)
        a, b = u[:n_tok], u[n_tok:]
        c, d = g_ref[o, :n_tok, :], g_ref[o, n_tok:, :]
        bd = b * d
        y_re = a * c - jnp.where(first, 0.0, bd)
        y_im = jnp.where(first, bd, a * d + b * c)
        conv = (jnp.dot(inv_ref[:, :n_tok], y_re.astype(BF16), preferred_element_type=F32)
                + jnp.dot(inv_ref[:, n_tok:], y_im.astype(BF16), preferred_element_type=F32))
        y = gates[o] * (conv + skip_ref[o:o + 1, :] * y)
    out_ref[...] = y.astype(BF16)


def hyena_mixer(z, conv_w, conv_b, skip, spectra, fwd_hi, inv_hi, n_seq, n_tok, row_blk0):
    width = 256
    nw = D_GRP // width
    zcol = lambda base: pl.BlockSpec((n_tok, width), lambda b, j: (row_blk0 + b, base * LANES // width + j))
    wcol = lambda part: pl.BlockSpec((3, width), lambda b, j: (0, part * nw + j))
    bcol = lambda part: pl.BlockSpec((1, width), lambda b, j: (0, part * nw + j))
    cb = conv_b.reshape(1, 3 * D_GRP)
    in_specs = [zcol(ZD_Y), zcol(ZD_X1), zcol(ZD_X2), wcol(0), wcol(1), wcol(2), bcol(0), bcol(1), bcol(2),
                pl.BlockSpec((HY_ORDER, width), lambda b, j: (0, j)),
                pl.BlockSpec((HY_ORDER, 2 * n_tok, width), lambda b, j: (0, 0, j)),
                pl.BlockSpec((2 * n_tok, n_tok), lambda b, j: (0, 0)),
                pl.BlockSpec((n_tok, 2 * n_tok), lambda b, j: (0, 0))]
    args = [z, z, z, conv_w, conv_w, conv_w, cb, cb, cb, skip, spectra, fwd_hi, inv_hi]
    return pl.pallas_call(
        functools.partial(_hyena_conv_kernel, n_tok=n_tok),
        out_shape=jax.ShapeDtypeStruct((n_seq * n_tok, D_GRP), BF16),
        grid=(n_seq, nw),
        in_specs=in_specs,
        out_specs=pl.BlockSpec((n_tok, width), lambda b, j: (b, j)),
        compiler_params=_cparams(("arbitrary", "arbitrary")),
        name="hyena_mixer",
    )(*args)


def _outproj_kernel(*refs, n_ctx_blk):
    ctx_parts, dec_parts = refs[0:N_MIXERS], refs[N_MIXERS:2 * N_MIXERS]
    w_ref, x_ref, g1_ref, ng_ref, sc_ref, sh_ref, x1_ref, h2_ref = refs[2 * N_MIXERS:]
    is_ctx = pl.program_id(0) < n_ctx_blk
    acc = None
    for g in range(N_MIXERS):
        part = jnp.where(is_ctx, ctx_parts[g][...], dec_parts[g][...])
        term = jnp.dot(part, w_ref[g * D_GRP:(g + 1) * D_GRP, :], preferred_element_type=F32)
        acc = term if acc is None else acc + term
    x1 = x_ref[...] + g1_ref[0] * acc
    x1_ref[...] = x1
    h2_ref[...] = _ada_norm(x1, ng_ref[...], sc_ref[0], sh_ref[0]).astype(BF16)


def out_projection(ctx_parts, dec_parts, w_out_bf16, x, mod, layer, norm_g, n_ctx_tok, dec_seq):
    m = x.shape[0]
    tm = 512
    n_ctx_blk = n_ctx_tok // tm
    modspec = lambda part: pl.BlockSpec((1, 1, D_MODEL), _mod_row_index(layer, part, tm, n_ctx_tok, dec_seq))
    ctx_part = pl.BlockSpec((tm, D_GRP), lambda i: (jnp.minimum(i, n_ctx_blk - 1), 0))
    dec_part = pl.BlockSpec((tm, D_GRP), lambda i: (jnp.maximum(i - n_ctx_blk, 0), 0))
    row = pl.BlockSpec((tm, D_MODEL), lambda i: (i, 0))
    return pl.pallas_call(
        functools.partial(_outproj_kernel, n_ctx_blk=n_ctx_blk),
        out_shape=[jax.ShapeDtypeStruct((m, D_MODEL), F32), jax.ShapeDtypeStruct((m, D_MODEL), BF16)],
        grid=(m // tm,),
        in_specs=[ctx_part] * N_MIXERS + [dec_part] * N_MIXERS
        + [pl.BlockSpec((D_MODEL, D_MODEL), lambda i: (0, 0)), row,
           modspec(2), pl.BlockSpec((1, D_MODEL), lambda i: (0, 0)), modspec(4), modspec(3)],
        out_specs=[row, row],
        compiler_params=_cparams(("arbitrary",)),
        name="out_projection",
    )(*ctx_parts, *dec_parts, w_out_bf16, x, mod, norm_g.reshape(1, D_MODEL), mod, mod)


def _pack_ffn_up_kernel(w_ref, o_ref, *, tf):
    nf = o_ref.shape[1]
    for j in range(nf):
        o_ref[0, j, :, 0:tf] = w_ref[0, :, j * tf:(j + 1) * tf].astype(BF16)
        o_ref[0, j, :, tf:2 * tf] = w_ref[0, :, D_FF + j * tf:D_FF + (j + 1) * tf].astype(BF16)


def pack_ffn_up_weights(w_up, tf):
    tk = 128
    nf = D_FF // tf
    return pl.pallas_call(
        functools.partial(_pack_ffn_up_kernel, tf=tf),
        out_shape=jax.ShapeDtypeStruct((DEPTH, nf, D_MODEL, 2 * tf), BF16),
        grid=(DEPTH, D_MODEL // tk),
        in_specs=[pl.BlockSpec((1, tk, 2 * D_FF), lambda l, k: (l, k, 0))],
        out_specs=pl.BlockSpec((1, nf, tk, 2 * tf), lambda l, k: (l, 0, k, 0)),
        compiler_params=_cparams(("arbitrary", "arbitrary")),
        name="pack_ffn_up_weights",
    )(w_up)


def _ffn_kernel(h_ref, wu_ref, cwa_ref, cwb_ref, cba_ref, cbb_ref, wd_ref, y_ref,
                ua0_ref, ub0_ref, ua1_ref, ub1_ref, *, tile_rows, n_ctx_tok, ctx_seq, dec_seq):
    i = pl.program_id(0)
    j = pl.program_id(1)
    tf = wd_ref.shape[1]
    body = slice(SUBLANES, SUBLANES + tile_rows)

    @pl.when((i == 0) & (j == 0))
    def _():
        for ref in (ua0_ref, ub0_ref, ua1_ref, ub1_ref):
            ref[...] = jnp.zeros_like(ref)

    half = tile_rows // 2

    def park(dst_a, dst_b):
        u = jnp.dot(h_ref[...], wu_ref[0, 0], preferred_element_type=F32)
        dst_a[body, :] = u[:, 0:tf]
        dst_b[body, :] = u[:, tf:2 * tf]

    def gated(ready_a, ready_b, r0):
        seq_mask = jnp.where(i * tile_rows < n_ctx_tok, ctx_seq - 1, dec_seq - 1)
        pos = (r0 + _iota((half, tf), 0)) & seq_mask
        first, last = pos == 0, pos == seq_mask
        lo = SUBLANES + r0

        def dwconv(u_ref, cw_ref, cb_ref):
            cw = cw_ref[0]
            prev = jnp.where(first, 0.0, u_ref[lo - 1:lo - 1 + half, :])
            nxt = jnp.where(last, 0.0, u_ref[lo + 1:lo + 1 + half, :])
            return prev * cw[0:1] + u_ref[lo:lo + half, :] * cw[1:2] + nxt * cw[2:3] + cb_ref[0]

        ua = dwconv(ready_a, cwa_ref, cba_ref)
        ub = dwconv(ready_b, cwb_ref, cbb_ref)
        return (ua * _sigmoid(ua) * ub).astype(BF16)

    def step(park_a, park_b, ready_a, ready_b, first_finish):
        park(park_a, park_b)
        act0 = gated(ready_a, ready_b, 0)
        act1 = gated(ready_a, ready_b, half)
        wd = wd_ref[0].astype(BF16)
        for r0, act in ((0, act0), (half, act1)):
            contrib = jnp.dot(act, wd, preferred_element_type=F32)
            if first_finish:
                y_ref[r0:r0 + half, :] = contrib
            else:
                y_ref[r0:r0 + half, :] += contrib

    @pl.when(j == 0)
    def _():
        park(ua0_ref, ub0_ref)

    @pl.when(j == 1)
    def _():
        step(ua1_ref, ub1_ref, ua0_ref, ub0_ref, True)

    @pl.when((j > 1) & (j % 2 == 0))
    def _():
        step(ua0_ref, ub0_ref, ua1_ref, ub1_ref, False)

    @pl.when((j > 1) & (j % 2 == 1))
    def _():
        step(ua1_ref, ub1_ref, ua0_ref, ub0_ref, False)


def conv_ffn(h2, layer, w_up_packed, conv_w, conv_b, w_down, n_ctx_tok, ctx_seq, dec_seq):
    m = h2.shape[0]
    tm, tf = 1024, FFN_TF
    nf = D_FF // tf
    cb = conv_b.reshape(DEPTH, 1, 2 * D_FF)
    up = lambda j: jnp.minimum(j, nf - 1)
    dn = lambda j: jnp.maximum(j - 1, 0)
    park = pltpu.VMEM((tm + 2 * SUBLANES, tf), F32)
    return pl.pallas_call(
        functools.partial(_ffn_kernel, tile_rows=tm, n_ctx_tok=n_ctx_tok, ctx_seq=ctx_seq, dec_seq=dec_seq),
        out_shape=jax.ShapeDtypeStruct((m, D_MODEL), F32),
        grid=(m // tm, nf + 1),
        in_specs=[pl.BlockSpec((tm, D_MODEL), lambda i, j: (i, 0)),
                  pl.BlockSpec((1, 1, D_MODEL, 2 * tf), lambda i, j: (layer, up(j), 0, 0)),
                  pl.BlockSpec((1, 3, tf), lambda i, j: (layer, 0, dn(j))),
                  pl.BlockSpec((1, 3, tf), lambda i, j: (layer, 0, nf + dn(j))),
                  pl.BlockSpec((1, 1, tf), lambda i, j: (layer, 0, dn(j))),
                  pl.BlockSpec((1, 1, tf), lambda i, j: (layer, 0, nf + dn(j))),
                  pl.BlockSpec((1, tf, D_MODEL), lambda i, j: (layer, dn(j), 0))],
        out_specs=pl.BlockSpec((tm, D_MODEL), lambda i, j: (i, 0)),
        scratch_shapes=[park, park, park, park],
        compiler_params=_cparams(("arbitrary", "arbitrary")),
        name="conv_ffn",
    )(h2, w_up_packed, conv_w, conv_w, cb, cb, w_down)


def _residual_kernel(x_ref, y_ref, g_ref, o_ref):
    o_ref[...] = x_ref[...] + g_ref[0] * y_ref[...]


def final_residual(x1, y, mod, layer, row0, n_rows, n_ctx_tok, dec_seq):
    tm = 512
    blk0 = row0 // tm

    def mod_index(i):
        start = (blk0 + i) * tm
        row = jnp.where(start < n_ctx_tok, 0, 1 + (start - n_ctx_tok) // dec_seq)
        return (layer * 48 + row * 6 + 5, 0, 0)

    return pl.pallas_call(
        _residual_kernel,
        out_shape=jax.ShapeDtypeStruct((n_rows, D_MODEL), F32),
        grid=(n_rows // tm,),
        in_specs=[pl.BlockSpec((tm, D_MODEL), lambda i: (blk0 + i, 0)),
                  pl.BlockSpec((tm, D_MODEL), lambda i: (blk0 + i, 0)),
                  pl.BlockSpec((1, 1, D_MODEL), mod_index)],
        out_specs=pl.BlockSpec((tm, D_MODEL), lambda i: (i, 0)),
        compiler_params=_cparams(("arbitrary",)),
        name="final_residual",
    )(x1, y, mod)


def kernel(x_prompt, x_sample, cache_diff_k, cache_diff_v, cache_swa_k, cache_swa_v, state_mlstm_C, state_mlstm_n, state_mlstm_m, c, c_ctx, w_mod, b_mod, norm1_g, norm2_g, w_in, mlstm_ig_b, mlstm_fg_b, mlstm_norm_g, diff_qn_g, diff_kn_g, diff_lam, diff_out_g, swa_qn_g, swa_kn_g, swa_sink, hy_conv_w, hy_conv_b, hy_w1, hy_b1, hy_freq, hy_w2, hy_b2, hy_w3, hy_b3, hy_bias, w_out, ffn_w_up, ffn_conv_w, ffn_conv_b, ffn_w_down):
    n_ctx, ctx_seq = x_prompt.shape[:2]
    n_dec, dec_seq = x_sample.shape[:2]
    n_ctx_tok = n_ctx * ctx_seq
    n_dec_tok = n_dec * dec_seq
    dec_blk0 = n_ctx_tok // dec_seq

    x = (x_prompt.reshape(n_ctx_tok, D_MODEL), x_sample.reshape(n_dec_tok, D_MODEL))
    w_pack, wg_hi, wg_lo = pack_in_weights(w_in)
    w_up_packed = pack_ffn_up_weights(ffn_w_up, FFN_TF)
    cond8 = jnp.concatenate([c_ctx[None, :], c, jnp.zeros((8 - 1 - n_dec, D_MODEL), F32)], axis=0)
    mod = mod_vectors(cond8, w_mod, b_mod)

    dft = {}
    for t in (ctx_seq, dec_seq):
        fwd, inv = _dft_matrices(t)
        dft[t] = dict(fwd=fwd.astype(BF16), inv=inv.astype(BF16), split=_split2(fwd))

    delta = None
    caches = []
    for l in range(DEPTH):
        x, z, gates = in_projection(x, delta, mod, l, norm1_g[l], w_pack, wg_hi, wg_lo, n_ctx_tok, dec_seq)

        gate_bias = jnp.pad(jnp.concatenate([mlstm_ig_b[l].reshape(-1), mlstm_fg_b[l].reshape(-1)]),
                            (0, LANES - 4 * H_A)).reshape(1, LANES)
        lam_init = 0.8 - 0.6 * math.exp(-0.3 * l)

        a_ctx, c_new, n_new, m_new = mlstm_mixer(z, gates, gate_bias, mlstm_norm_g[l], None,
                                                 n_ctx, ctx_seq, 0, ctx_seq, True)
        b_ctx, diff_k = diff_mixer(z, diff_qn_g[l], diff_kn_g[l], diff_lam[l], diff_out_g[l], lam_init,
                                   None, n_ctx, ctx_seq, 0)
        c_ctx_out, swa_k = swa_mixer(z, swa_qn_g[l], swa_kn_g[l], swa_sink[l], None, n_ctx, ctx_seq, 0)
        spec_ctx = hyena_filter_spectra(ctx_seq, hy_w1[l], hy_b1[l], hy_freq[l], hy_w2[l], hy_b2[l],
                                        hy_w3[l], hy_b3[l], *dft[ctx_seq]["split"])
        d_ctx = hyena_mixer(z, hy_conv_w[l], hy_conv_b[l], hy_bias[l], spec_ctx, dft[ctx_seq]["fwd"],
                            dft[ctx_seq]["inv"], n_ctx, ctx_seq, 0)
        a_dec, = mlstm_mixer(z, gates, gate_bias, mlstm_norm_g[l],
                             (state_mlstm_C[:, l], state_mlstm_n[:, l], state_mlstm_m[:, l]),
                             n_dec, dec_seq, dec_blk0, 512, False)
        b_dec, = diff_mixer(z, diff_qn_g[l], diff_kn_g[l], diff_lam[l], diff_out_g[l], lam_init,
                            (cache_diff_k[:, l], cache_diff_v[:, l]), n_dec, dec_seq, dec_blk0)
        c_dec, = swa_mixer(z, swa_qn_g[l], swa_kn_g[l], swa_sink[l],
                           (cache_swa_k[:, l], cache_swa_v[:, l]), n_dec, dec_seq, dec_blk0)
        spec_dec = hyena_filter_spectra(dec_seq, hy_w1[l], hy_b1[l], hy_freq[l], hy_w2[l], hy_b2[l],
                                        hy_w3[l], hy_b3[l], *dft[dec_seq]["split"])
        d_dec = hyena_mixer(z, hy_conv_w[l], hy_conv_b[l], hy_bias[l], spec_dec, dft[dec_seq]["fwd"],
                            dft[dec_seq]["inv"], n_dec, dec_seq, dec_blk0)

        x, h2 = out_projection((a_ctx, b_ctx, c_ctx_out, d_ctx), (a_dec, b_dec, c_dec, d_dec),
                               w_out[l].astype(BF16), x, mod, l, norm2_g[l], n_ctx_tok, dec_seq)
        delta = conv_ffn(h2, l, w_up_packed, ffn_conv_w, ffn_conv_b, ffn_w_down, n_ctx_tok, ctx_seq, dec_seq)

        zc = z[:n_ctx_tok]
        caches.append((
            diff_k.reshape(n_ctx, ctx_seq, H_B, 2, DK_B),
            zc[:, ZB_V * LANES:(ZB_V + 4) * LANES].reshape(n_ctx, ctx_seq, H_B, DV_B),
            swa_k.reshape(n_ctx, ctx_seq, KV_C, DH_C),
            zc[:, ZC_V * LANES:(ZC_V + 2) * LANES].reshape(n_ctx, ctx_seq, KV_C, DH_C),
            c_new,
            n_new.reshape(n_ctx, 2, H_A, DH_A),
            m_new.reshape(n_ctx, 2, H_A),
        ))

    y_prompt = final_residual(x, delta, mod, DEPTH - 1, 0, n_ctx_tok, n_ctx_tok, dec_seq)
    y_sample = final_residual(x, delta, mod, DEPTH - 1, n_ctx_tok, n_dec_tok, n_ctx_tok, dec_seq)
    outs = [jnp.stack([s[i] for s in caches], axis=1) for i in range(7)]
    return (y_prompt.reshape(n_ctx, ctx_seq, D_MODEL), y_sample.reshape(n_dec, dec_seq, D_MODEL), *outs)
```

```python
import functools
import math

import jax
import jax.numpy as jnp
from jax import lax
from jax.experimental import pallas as pl
from jax.experimental.pallas import tpu as pltpu

D_MODEL = 2048
DEPTH = 2
GRID_W = 64
D_GRP = 512
N_MIXERS = D_MODEL // D_GRP
H_A = 4
DH_A = 128
H_B = 4
DV_B = 128
DK_B = 64
H_C = 4
KV_C = 2
G_C = 2
DH_C = 128
WINDOW = 128
HY_ORDER = 2
HY_POS_DIM = 33
HY_BANDS = (HY_POS_DIM - 1) // 2
HY_FF = 64
HY_FAST = math.log(1e-2) / 0.3
HY_SLOW = math.log(1e-2) / 1.5
D_FF = 5632
ROPE_BASE = 10000.0
EPS = 1e-6
N_A = 4 * D_GRP + 4 * H_A

LANES = 128
SUBLANES = 8
V7X_VMEM_LIMIT_BYTES = 56 * 1024 * 1024
Z_COLS = 12 * D_GRP
INPROJ_TN = 3 * D_GRP
FFN_TF = 256
ZA_Q, ZA_K, ZA_V, ZA_O = 0, 4, 8, 12
ZB_Q, ZB_K, ZB_V = 16, 20, 24
ZC_Q, ZC_K, ZC_V = 28, 32, 34
ZD_Y, ZD_X1, ZD_X2 = 36, 40, 44

BF16 = jnp.bfloat16
F32 = jnp.float32


def _cparams(sem):
    return pltpu.CompilerParams(dimension_semantics=sem, vmem_limit_bytes=V7X_VMEM_LIMIT_BYTES)


def _dot(a, b):
    return jnp.dot(a.astype(BF16), b.astype(BF16), preferred_element_type=F32)


def _dot_nt(a, b):
    return lax.dot_general(a.astype(BF16), b.astype(BF16), (((1,), (1,)), ((), ())),
                           preferred_element_type=F32)


def _bf16_head(a):
    bits = lax.bitcast_convert_type(a, jnp.int32) & jnp.int32(-65536)
    return lax.bitcast_convert_type(bits, F32)


def _split2(a):
    head = _bf16_head(a)
    return head.astype(BF16), (a - head).astype(BF16)


def _split3(a):
    head = _bf16_head(a)
    rest = a - head
    mid = _bf16_head(rest)
    return head.astype(BF16), mid.astype(BF16), (rest - mid).astype(BF16)


def _dot3(a, b):
    ah, al = _split2(a)
    bh, bl = _split2(b)
    return (jnp.dot(ah, bh, preferred_element_type=F32) + jnp.dot(al, bh, preferred_element_type=F32)
            + jnp.dot(ah, bl, preferred_element_type=F32))


def _iota(shape, dim):
    return lax.broadcasted_iota(jnp.int32, shape, dim)


def _sigmoid(x):
    return 1.0 / (1.0 + jnp.exp(-x))


def _log_sigmoid(x):
    return jnp.minimum(x, 0.0) - jnp.log(1.0 + jnp.exp(-jnp.abs(x)))


def _mod_kernel(c_ref, w_ref, b_ref, o_ref):
    k = pl.program_id(1)
    c = c_ref[...]
    part = _dot(c * _sigmoid(c), w_ref[0])

    @pl.when(k == 0)
    def _():
        o_ref[0] = part + b_ref[0]

    @pl.when(k > 0)
    def _():
        o_ref[0] += part


def mod_vectors(cond8, w_mod, b_mod):
    tk = 256
    n = w_mod.shape[2]
    out = pl.pallas_call(
        _mod_kernel,
        out_shape=jax.ShapeDtypeStruct((DEPTH, 8, n), F32),
        grid=(DEPTH, D_MODEL // tk),
        in_specs=[pl.BlockSpec((8, tk), lambda l, k: (0, k)),
                  pl.BlockSpec((1, tk, n), lambda l, k: (l, k, 0)),
                  pl.BlockSpec((1, 1, n), lambda l, k: (l, 0, 0))],
        out_specs=pl.BlockSpec((1, 8, n), lambda l, k: (l, 0, 0)),
        compiler_params=_cparams(("arbitrary", "arbitrary")),
        name="mod_vectors",
    )(cond8, w_mod, b_mod.reshape(DEPTH, 1, n))
    return out.reshape(DEPTH * 8 * 6, 1, D_MODEL)


def _mod_row_index(layer, part, tile_rows, n_ctx_tok, dec_seq):
    def index(i, *_):
        start = i * tile_rows
        row = jnp.where(start < n_ctx_tok, 0, 1 + (start - n_ctx_tok) // dec_seq)
        return (layer * 48 + row * 6 + part, 0, 0)
    return index


def _ada_norm(x, ng, sc, sh):
    y = x * lax.rsqrt(jnp.mean(x * x, axis=-1, keepdims=True) + EPS)
    return y * ng * (1.0 + sc) + sh


def _pack_in_kernel(a_ref, b_ref, w_ref, gh_ref, gl_ref):
    c = pl.program_id(1)
    n_aligned = 4 * D_GRP // a_ref.shape[1]
    gap = 4 * H_A

    @pl.when(c < n_aligned)
    def _():
        w_ref[0, 0] = a_ref[0].T.astype(BF16)

    @pl.when(c >= n_aligned)
    def _():
        w_ref[0, 0] = jnp.concatenate([a_ref[0, gap:, :], b_ref[0]], axis=0).T.astype(BF16)

    @pl.when(c == n_aligned)
    def _():
        rows = a_ref[0, 0:LANES, :]
        hi, lo = _split2(jnp.where(_iota(rows.shape, 0) < gap, rows, 0.0).T)
        gh_ref[0] = hi
        gl_ref[0] = lo


def pack_in_weights(w_in):
    tc = 512
    gap = 4 * H_A
    n_blk = Z_COLS // tc
    per_tile = INPROJ_TN // tc
    w_t = jnp.swapaxes(w_in, 1, 2)
    gate = pl.BlockSpec((1, D_MODEL, LANES), lambda l, c: (l, 0, 0))
    return pl.pallas_call(
        _pack_in_kernel,
        out_shape=[jax.ShapeDtypeStruct((DEPTH, Z_COLS // INPROJ_TN, D_MODEL, INPROJ_TN), BF16),
                   jax.ShapeDtypeStruct((DEPTH, D_MODEL, LANES), BF16),
                   jax.ShapeDtypeStruct((DEPTH, D_MODEL, LANES), BF16)],
        grid=(DEPTH, n_blk),
        in_specs=[pl.BlockSpec((1, tc, D_MODEL), lambda l, c: (l, c, 0)),
                  pl.BlockSpec((1, gap, D_MODEL), lambda l, c: (l, (c + 1) * (tc // gap), 0))],
        out_specs=[pl.BlockSpec((1, 1, D_MODEL, tc), lambda l, c: (l, c // per_tile, 0, c % per_tile)),
                   gate, gate],
        compiler_params=_cparams(("arbitrary", "arbitrary")),
        name="pack_in_weights",
    )(w_t, w_t)


def _inproj_kernel(*refs, has_delta, n_ctx_blk):
    if has_delta:
        (x_ref, y_ref, g2_ref, ng_ref, sc_ref, sh_ref, w_ref, wgh_ref, wgl_ref,
         xn_ref, z_ref, gate_ref, hb_ref) = refs
    else:
        (xc_ref, xd_ref, ng_ref, sc_ref, sh_ref, w_ref, wgh_ref, wgl_ref,
         xn_ref, z_ref, gate_ref, hb_ref) = refs

    i = pl.program_id(0)

    @pl.when(pl.program_id(1) == 0)
    def _():
        if has_delta:
            x = x_ref[...] + g2_ref[0] * y_ref[...]
        else:
            x = jnp.where(i < n_ctx_blk, xc_ref[...], xd_ref[...])
        xn_ref[...] = x
        h = _ada_norm(x, ng_ref[...], sc_ref[0], sh_ref[0])
        hb_ref[...] = h.astype(BF16)
        hh, hl = _split2(h)
        wgh = wgh_ref[0]
        gate_ref[...] = (jnp.dot(hh, wgh, preferred_element_type=F32)
                         + jnp.dot(hl, wgh, preferred_element_type=F32)
                         + jnp.dot(hh, wgl_ref[0], preferred_element_type=F32))

    z_ref[...] = jnp.dot(hb_ref[...], w_ref[0, 0], preferred_element_type=F32)


def in_projection(x, delta, mod, layer, norm_g, w_pack, wg_hi, wg_lo, n_ctx_tok, dec_seq):
    tm, tn = 512, INPROJ_TN
    has_delta = delta is not None
    n_ctx_blk = n_ctx_tok // tm
    row = pl.BlockSpec((tm, D_MODEL), lambda i, j: (i, 0))
    modspec = lambda part: pl.BlockSpec((1, 1, D_MODEL), _mod_row_index(layer, part, tm, n_ctx_tok, dec_seq))
    vec = pl.BlockSpec((1, D_MODEL), lambda i, j: (0, 0))
    if has_delta:
        m = x.shape[0]
        g2spec = pl.BlockSpec((1, 1, D_MODEL), _mod_row_index(layer - 1, 5, tm, n_ctx_tok, dec_seq))
        in_specs = [row, row, g2spec]
        args = [x, delta, mod]
    else:
        x_ctx, x_dec = x
        m = x_ctx.shape[0] + x_dec.shape[0]
        in_specs = [pl.BlockSpec((tm, D_MODEL), lambda i, j: (jnp.minimum(i, n_ctx_blk - 1), 0)),
                    pl.BlockSpec((tm, D_MODEL), lambda i, j: (jnp.maximum(i - n_ctx_blk, 0), 0))]
        args = [x_ctx, x_dec]
    gate_w = pl.BlockSpec((1, D_MODEL, LANES), lambda i, j: (layer, 0, 0))
    in_specs += [vec, modspec(1), modspec(0),
                 pl.BlockSpec((1, 1, D_MODEL, tn), lambda i, j: (layer, j, 0, 0)), gate_w, gate_w]
    args += [norm_g.reshape(1, D_MODEL), mod, mod, w_pack, wg_hi, wg_lo]
    return pl.pallas_call(
        functools.partial(_inproj_kernel, has_delta=has_delta, n_ctx_blk=n_ctx_blk),
        out_shape=[jax.ShapeDtypeStruct((m, D_MODEL), F32), jax.ShapeDtypeStruct((m, Z_COLS), F32),
                   jax.ShapeDtypeStruct((m, LANES), F32)],
        grid=(m // tm, Z_COLS // tn),
        in_specs=in_specs,
        out_specs=[row, pl.BlockSpec((tm, tn), lambda i, j: (i, j)), pl.BlockSpec((tm, LANES), lambda i, j: (i, 0))],
        scratch_shapes=[pltpu.VMEM((tm, D_MODEL), BF16)],
        compiler_params=_cparams(("arbitrary", "arbitrary")),
        name="in_projection",
    )(*args)


def _mlstm_kernel(*refs, n_tok, chunk, has_state, want_state):
    refs = list(refs)
    q_ref, k_ref, v_ref, o_ref, gate_ref, gb_ref, ng_ref = refs[:7]
    pos = 7
    if has_state:
        c0_ref, n0_ref, m0_ref = refs[pos:pos + 3]
        pos += 3
    out_ref = refs[pos]
    pos += 1
    if want_state:
        cn_ref, nn_ref, mn_ref = refs[pos:pos + 3]
        pos += 3
    hacc_ref = refs[pos]

    L = chunk
    nc = n_tok // L
    scale = DH_A ** -0.5
    r_idx = _iota((L, L), 0)
    c_idx = _iota((L, L), 1)
    ones_b = jnp.ones((L, DH_A), BF16)

    g = gate_ref[...] + gb_ref[...]
    vals = jnp.where(_iota((n_tok, LANES), 1) < 2 * H_A, g, _log_sigmoid(g))

    for d in range(2):
        causal = (c_idx <= r_idx) if d == 0 else (c_idx >= r_idx)
        tri = jnp.where(causal, 1.0, 0.0).astype(BF16)
        if has_state:
            c_mem = [c0_ref[0, d, h] for h in range(H_A)]
            n_mem = [n0_ref[0, d, h] for h in range(H_A)]
            m_prev = [m0_ref[0, d, h] for h in range(H_A)]
            pad = jnp.zeros((SUBLANES - 1, DH_A), F32)
            n_mat = [_dot(jnp.concatenate([n, pad], axis=0).T, jnp.ones((SUBLANES, DH_A), BF16)) for n in n_mem]
        else:
            c_mem = [None] * H_A
            n_mem = [None] * H_A
            n_mat = [None] * H_A
            m_prev = [jnp.zeros((1, 1), F32)] * H_A
        order = range(nc) if d == 0 else range(nc - 1, -1, -1)
        for ci, c in enumerate(order):
            rows = slice(c * L, (c + 1) * L)
            v_c = vals[rows]
            p0, p1, p2 = _split3(v_c)
            cum = (jnp.dot(tri, p0, preferred_element_type=F32) + jnp.dot(tri, p1, preferred_element_type=F32)
                   + jnp.dot(tri, p2, preferred_element_type=F32))
            v_t = v_c.T
            cum_t = cum.T
            for h in range(H_A):
                col_i = d * H_A + h
                col_f = 2 * H_A + d * H_A + h
                lanes = slice(h * DH_A, (h + 1) * DH_A)
                c_col = cum[:, col_f:col_f + 1]
                ig_col = v_c[:, col_i:col_i + 1]
                c_row = cum_t[col_f:col_f + 1, :]
                ig_row = v_t[col_i:col_i + 1, :]

                dmat = jnp.where(causal, c_col - c_row + ig_row, -jnp.inf)
                inter = c_col + m_prev[h]
                m_col = jnp.maximum(inter, jnp.max(dmat, axis=1, keepdims=True))
                dexp = jnp.exp(dmat - m_col)
                qs = q_ref[rows, lanes] * scale
                qb = qs.astype(BF16)
                kf = k_ref[rows, lanes]
                vb = v_ref[rows, lanes].astype(BF16)
                pb = (_dot_nt(qb, kf) * dexp).astype(BF16)
                num = jnp.dot(pb, vb, preferred_element_type=F32)
                den = jnp.dot(pb, ones_b, preferred_element_type=F32)
                carried = c_mem[h] is not None
                if carried:
                    w_inter = jnp.broadcast_to(jnp.exp(inter - m_col), (L, DH_A))
                    num = num + w_inter * _dot(qb, c_mem[h])
                    den = den + w_inter * _dot(qb, n_mat[h])
                hh = num / jnp.maximum(jnp.abs(den), jnp.exp(-m_col))
                if d == 0:
                    hacc_ref[rows, lanes] = hh
                else:
                    hacc_ref[rows, lanes] += hh

                if want_state or ci < nc - 1:
                    total = c_col[L - 1:L, :] if d == 0 else c_col[0:1, :]
                    w_row = total - c_row + ig_row
                    m_new = jnp.maximum(total + m_prev[h], jnp.max(w_row, axis=1, keepdims=True))
                    decay = jnp.exp(total + m_prev[h] - m_new)
                    wk = jnp.exp(total - c_col + ig_col - m_new)
                    kw = kf * wk
                    kw_t = kw.T.astype(BF16)
                    c_upd = jnp.dot(kw_t, vb, preferred_element_type=F32)
                    c_mem[h] = decay * c_mem[h] + c_upd if carried else c_upd
                    if ci < nc - 1:
                        n_upd = jnp.dot(kw_t, ones_b, preferred_element_type=F32)
                        n_mat[h] = decay * n_mat[h] + n_upd if carried else n_upd
                    if want_state:
                        n_row = jnp.sum(kw, axis=0, keepdims=True)
                        n_mem[h] = decay * n_mem[h] + n_row if carried else n_row
                    m_prev[h] = m_new
        if want_state:
            for h in range(H_A):
                cn_ref[0, d, h] = c_mem[h]
                nn_ref[0, d, h] = n_mem[h]
                mn_ref[0, d, h] = m_prev[h]

    for h in range(H_A):
        lanes = slice(h * DH_A, (h + 1) * DH_A)
        hs = hacc_ref[:, lanes]
        y = hs * lax.rsqrt(jnp.mean(hs * hs, axis=1, keepdims=True) + EPS) * ng_ref[:, lanes]
        out_ref[:, lanes] = (y * _sigmoid(o_ref[:, lanes])).astype(BF16)


def mlstm_mixer(z, gates, gate_bias, norm_g, state, n_seq, n_tok, row_blk0, chunk, want_state):
    has_state = state is not None
    col = lambda base: pl.BlockSpec((n_tok, D_GRP), lambda b: (row_blk0 + b, base * LANES // D_GRP))
    in_specs = [col(ZA_Q), col(ZA_K), col(ZA_V), col(ZA_O),
                pl.BlockSpec((n_tok, LANES), lambda b: (row_blk0 + b, 0)),
                pl.BlockSpec((1, LANES), lambda b: (0, 0)),
                pl.BlockSpec((1, D_GRP), lambda b: (0, 0))]
    args = [z, z, z, z, gates, gate_bias, norm_g.reshape(1, D_GRP)]
    st_specs = [pl.BlockSpec((1, 2, H_A, DH_A, DH_A), lambda b: (b, 0, 0, 0, 0)),
                pl.BlockSpec((1, 2, H_A, 1, DH_A), lambda b: (b, 0, 0, 0, 0)),
                pl.BlockSpec((1, 2, H_A, 1, 1), lambda b: (b, 0, 0, 0, 0))]
    if has_state:
        c0, n0, m0 = state
        in_specs += st_specs
        args += [c0, n0.reshape(n_seq, 2, H_A, 1, DH_A), m0.reshape(n_seq, 2, H_A, 1, 1)]
    out_shape = [jax.ShapeDtypeStruct((n_seq * n_tok, D_GRP), BF16)]
    out_specs = [pl.BlockSpec((n_tok, D_GRP), lambda b: (b, 0))]
    if want_state:
        out_shape += [jax.ShapeDtypeStruct((n_seq, 2, H_A, DH_A, DH_A), F32),
                      jax.ShapeDtypeStruct((n_seq, 2, H_A, 1, DH_A), F32),
                      jax.ShapeDtypeStruct((n_seq, 2, H_A, 1, 1), F32)]
        out_specs += st_specs
    outs = pl.pallas_call(
        functools.partial(_mlstm_kernel, n_tok=n_tok, chunk=chunk, has_state=has_state, want_state=want_state),
        out_shape=out_shape,
        grid=(n_seq,),
        in_specs=in_specs,
        out_specs=out_specs,
        scratch_shapes=[pltpu.VMEM((n_tok, D_GRP), F32)],
        compiler_params=_cparams(("arbitrary",)),
        name="mlstm_mixer",
    )(*args)
    return outs


def _rope_tables(n_tok, dh):
    rows = n_tok // GRID_W
    t_row = jnp.repeat(jnp.arange(rows, dtype=F32), GRID_W)
    t_col = jnp.tile(jnp.arange(GRID_W, dtype=F32), rows)
    n_freq = dh // 4
    inv = ROPE_BASE ** (-jnp.arange(n_freq, dtype=F32) / n_freq)
    ang = jnp.concatenate([t_row[:, None] * inv, t_col[:, None] * inv], axis=-1)
    cos, sin = jnp.cos(ang), jnp.sin(ang)
    reps = LANES // dh
    cos_t = jnp.tile(jnp.concatenate([cos, cos], axis=-1), (1, reps))
    sin_t = jnp.tile(jnp.concatenate([-sin, sin], axis=-1), (1, reps))
    return cos_t, sin_t


def _rope(x, cos_t, sin_t, dh):
    half = dh // 2
    if dh == LANES:
        partner = pltpu.roll(x, half, 1)
    else:
        first = (_iota(x.shape, 1) & half) == 0
        partner = jnp.where(first, pltpu.roll(x, LANES - half, 1), pltpu.roll(x, half, 1))
    return x * cos_t + partner * sin_t


def _rms(x, g):
    return x * lax.rsqrt(jnp.mean(x * x, axis=1, keepdims=True) + EPS) * g


def _diff_kernel(*refs, n_tok, q_blk, lam_init, has_ctx):
    refs = list(refs)
    q_ref, k_ref, v_ref, qg_ref, kg_ref, lam_ref, og_ref = refs[:7]
    pos = 7
    if has_ctx:
        ck_ref, cv_ref, cos_ref, sin_ref = refs[pos:pos + 4]
        pos += 4
    out_ref = refs[pos]
    pos += 1
    if not has_ctx:
        kout_ref = refs[pos]

    grp0 = _iota((n_tok, LANES), 1) < DK_B

    def group_norm(x, g):
        x2 = x * x
        s0 = jnp.sum(jnp.where(grp0, x2, 0.0), axis=1, keepdims=True)
        s1 = jnp.sum(jnp.where(grp0, 0.0, x2), axis=1, keepdims=True)
        ms = jnp.where(grp0, s0, s1) * (1.0 / DK_B)
        return x * lax.rsqrt(ms + EPS) * g

    qn = group_norm(q_ref[...], qg_ref[...])
    kn = group_norm(k_ref[...], kg_ref[...])
    if has_ctx:
        cos_t, sin_t = cos_ref[...], sin_ref[...]
        qn = _rope(qn, cos_t, sin_t, DK_B)
        kn = _rope(kn, cos_t, sin_t, DK_B)
    else:
        kout_ref[...] = kn
    kb = kn.astype(BF16)
    vb = v_ref[...].astype(BF16)
    if has_ctx:
        ckb = ck_ref[0].astype(BF16)
        cvb = cv_ref[0].astype(BF16)

    lp = lam_ref[...]
    lam = (jnp.exp(jnp.sum(lp[0:1] * lp[1:2], axis=1, keepdims=True))
           - jnp.exp(jnp.sum(lp[2:3] * lp[3:4], axis=1, keepdims=True)) + lam_init)
    scale = DK_B ** -0.5
    assert math.frexp(scale)[0] == 0.5
    grp0_q = _iota((q_blk, LANES), 1) < DK_B

    for qi in range(n_tok // q_blk):
        rows = slice(qi * q_blk, (qi + 1) * q_blk)
        q_rows = qn[rows] * scale
        outs = []
        for comp in range(2):
            keep = grp0_q if comp == 0 else jnp.logical_not(grp0_q)
            qm = jnp.where(keep, q_rows, 0.0).astype(BF16)
            s = _dot_nt(qm, kb)
            mx = jnp.max(s, axis=1, keepdims=True)
            if has_ctx:
                sc = _dot_nt(qm, ckb)
                mx = jnp.maximum(mx, jnp.max(sc, axis=1, keepdims=True))
            p = jnp.exp(s - mx)
            den = jnp.sum(p, axis=1, keepdims=True)
            acc = _dot(p, vb)
            if has_ctx:
                pc = jnp.exp(sc - mx)
                den = den + jnp.sum(pc, axis=1, keepdims=True)
                acc = acc + _dot(pc, cvb)
            outs.append(acc / den)
        a = outs[0] - lam * outs[1]
        out_ref[rows, :] = (_rms(a, og_ref[...]) * (1.0 - lam_init)).astype(BF16)


def diff_mixer(z, qn_g, kn_g, lam, out_g, lam_init, ctx, n_seq, n_tok, row_blk0):
    has_ctx = ctx is not None
    col = lambda base: pl.BlockSpec((n_tok, LANES), lambda b, h: (row_blk0 + b, base + h))
    vec = pl.BlockSpec((1, LANES), lambda b, h: (0, 0))
    in_specs = [col(ZB_Q), col(ZB_K), col(ZB_V), vec, vec,
                pl.BlockSpec((4, LANES), lambda b, h: (0, 0)), vec]
    args = [z, z, z, jnp.tile(qn_g, 2).reshape(1, LANES), jnp.tile(kn_g, 2).reshape(1, LANES),
            jnp.pad(lam, ((0, 0), (0, LANES - DK_B))), out_g.reshape(1, LANES)]
    if has_ctx:
        ck, cv = ctx
        s_ctx = ck.shape[1]
        cos_t, sin_t = _rope_tables(n_tok, DK_B)
        tab = pl.BlockSpec((n_tok, LANES), lambda b, h: (0, 0))
        cspec = pl.BlockSpec((1, s_ctx, LANES), lambda b, h: (b, 0, h))
        in_specs += [cspec, cspec, tab, tab]
        args += [ck.reshape(n_seq, s_ctx, H_B * 2 * DK_B), cv.reshape(n_seq, s_ctx, H_B * DV_B), cos_t, sin_t]
    out_shape = [jax.ShapeDtypeStruct((n_seq * n_tok, D_GRP), BF16)]
    out_specs = [pl.BlockSpec((n_tok, LANES), lambda b, h: (b, h))]
    if not has_ctx:
        out_shape.append(jax.ShapeDtypeStruct((n_seq * n_tok, D_GRP), F32))
        out_specs.append(pl.BlockSpec((n_tok, LANES), lambda b, h: (b, h)))
    return pl.pallas_call(
        functools.partial(_diff_kernel, n_tok=n_tok, q_blk=min(n_tok, 256), lam_init=lam_init, has_ctx=has_ctx),
        out_shape=out_shape,
        grid=(n_seq, H_B),
        in_specs=in_specs,
        out_specs=out_specs,
        compiler_params=_cparams(("arbitrary", "arbitrary")),
        name="diff_mixer",
    )(*args)


def _swa_kernel(*refs, n_tok, q_blk, has_ctx):
    refs = list(refs)
    sink_ref, q_ref, k_ref, v_ref, qg_ref, kg_ref = refs[:6]
    pos = 6
    if has_ctx:
        ck_ref, cv_ref, cos_ref, sin_ref = refs[pos:pos + 4]
        pos += 4
    out_ref = refs[pos]
    pos += 1
    if not has_ctx:
        kout_ref = refs[pos]

    kv = pl.program_id(1)
    kn = _rms(k_ref[...], kg_ref[...])
    if has_ctx:
        cos_t, sin_t = cos_ref[...], sin_ref[...]
        kn = _rope(kn, cos_t, sin_t, DH_C)
        ckb = ck_ref[0].astype(BF16)
        cvb = cv_ref[0].astype(BF16)
    else:
        kout_ref[...] = kn
    kb = kn.astype(BF16)
    vb = v_ref[...].astype(BF16)
    scale = DH_C ** -0.5

    for g in range(G_C):
        sink = sink_ref[kv * G_C + g]
        qn = _rms(q_ref[:, g * DH_C:(g + 1) * DH_C], qg_ref[...])
        if has_ctx:
            qn = _rope(qn, cos_t, sin_t, DH_C)
        for qi in range(n_tok // q_blk):
            rows = slice(qi * q_blk, (qi + 1) * q_blk)
            qb = qn[rows].astype(BF16)
            s = _dot_nt(qb, kb) * scale
            if has_ctx:
                qpos = qi * q_blk + _iota((q_blk, n_tok), 0)
                kpos = _iota((q_blk, n_tok), 1)
                s = jnp.where(jnp.abs(kpos - qpos) <= WINDOW, s, -jnp.inf)
            mx = jnp.maximum(jnp.max(s, axis=1, keepdims=True), sink)
            if has_ctx:
                sc = _dot_nt(qb, ckb) * scale
                mx = jnp.maximum(mx, jnp.max(sc, axis=1, keepdims=True))
            p = jnp.exp(s - mx)
            den = jnp.sum(p, axis=1, keepdims=True) + jnp.exp(sink - mx)
            acc = _dot(p, vb)
            if has_ctx:
                pc = jnp.exp(sc - mx)
                den = den + jnp.sum(pc, axis=1, keepdims=True)
                acc = acc + _dot(pc, cvb)
            out_ref[rows, g * DH_C:(g + 1) * DH_C] = (acc / den).astype(BF16)


def swa_mixer(z, qn_g, kn_g, sink, ctx, n_seq, n_tok, row_blk0):
    has_ctx = ctx is not None
    vec = pl.BlockSpec((1, LANES), lambda b, kv: (0, 0))
    in_specs = [pl.BlockSpec(memory_space=pltpu.SMEM),
                pl.BlockSpec((n_tok, G_C * DH_C), lambda b, kv: (row_blk0 + b, ZC_Q // G_C + kv)),
                pl.BlockSpec((n_tok, LANES), lambda b, kv: (row_blk0 + b, ZC_K + kv)),
                pl.BlockSpec((n_tok, LANES), lambda b, kv: (row_blk0 + b, ZC_V + kv)),
                vec, vec]
    args = [sink.astype(F32), z, z, z, qn_g.reshape(1, LANES), kn_g.reshape(1, LANES)]
    if has_ctx:
        ck, cv = ctx
        s_ctx = ck.shape[1]
        cos_t, sin_t = _rope_tables(n_tok, DH_C)
        tab = pl.BlockSpec((n_tok, LANES), lambda b, kv: (0, 0))
        cspec = pl.BlockSpec((1, s_ctx, LANES), lambda b, kv: (b, 0, kv))
        in_specs += [cspec, cspec, tab, tab]
        args += [ck.reshape(n_seq, s_ctx, KV_C * DH_C), cv.reshape(n_seq, s_ctx, KV_C * DH_C), cos_t, sin_t]
    out_shape = [jax.ShapeDtypeStruct((n_seq * n_tok, D_GRP), BF16)]
    out_specs = [pl.BlockSpec((n_tok, G_C * DH_C), lambda b, kv: (b, kv))]
    if not has_ctx:
        out_shape.append(jax.ShapeDtypeStruct((n_seq * n_tok, KV_C * DH_C), F32))
        out_specs.append(pl.BlockSpec((n_tok, LANES), lambda b, kv: (b, kv)))
    return pl.pallas_call(
        functools.partial(_swa_kernel, n_tok=n_tok, q_blk=min(n_tok, 256), has_ctx=has_ctx),
        out_shape=out_shape,
        grid=(n_seq, KV_C),
        in_specs=in_specs,
        out_specs=out_specs,
        compiler_params=_cparams(("arbitrary", "arbitrary")),
        name="swa_mixer",
    )(*args)


def _dft_matrices(n_tok):
    k = jnp.arange(n_tok, dtype=jnp.int32)[:, None]
    s = jnp.arange(n_tok, dtype=jnp.int32)[None, :]
    ang = (math.pi / n_tok) * ((k * s) % (2 * n_tok)).astype(F32)
    cos, sin = jnp.cos(ang), jnp.sin(ang)
    alt_s = jnp.where(s % 2 == 0, 1.0, -1.0).astype(F32)
    fwd = jnp.concatenate([cos, jnp.where(k == 0, alt_s, -sin)], axis=0)
    inv_re = jnp.where(s == 0, 1.0, 2.0 * cos) / (2 * n_tok)
    alt_t = jnp.where(k % 2 == 0, 1.0, -1.0).astype(F32)
    inv_im = jnp.where(s == 0, alt_t, -2.0 * sin) / (2 * n_tok)
    inv = jnp.concatenate([inv_re, inv_im], axis=1)
    return fwd, inv


def _hyena_filter_kernel(feat_ref, w1_ref, b1_ref, fr_ref, w2_ref, b2_ref, w3_ref, b3_ref, win_ref,
                         fh_ref, fl_ref, g_ref, *, n_tok):
    fr = fr_ref[...]
    h = jnp.sin(fr * (_dot3(feat_ref[...], w1_ref[...]) + b1_ref[...]))
    h = jnp.sin(fr * (_dot3(h, w2_ref[...]) + b2_ref[...]))
    f = _dot3(h, w3_ref[...]) + b3_ref[...]
    win = win_ref[...]
    h_fwd = f[:, :D_GRP] * win
    h_bwd = jnp.where(_iota((n_tok, D_GRP), 0) == 0, 0.0, f[:, D_GRP:] * win)
    fh, fl = fh_ref[...], fl_ref[...]

    def spectrum(x):
        xh, xl = _split2(x)
        return (jnp.dot(fh, xh, preferred_element_type=F32) + jnp.dot(fl, xh, preferred_element_type=F32)
                + jnp.dot(fh, xl, preferred_element_type=F32))

    conj_sign = jnp.where(_iota((2 * n_tok, D_GRP), 0) <= n_tok, 1.0, -1.0)
    g_ref[0] = spectrum(h_fwd) + conj_sign * spectrum(h_bwd)


def hyena_filter_spectra(n_tok, w1, b1, freq, w2, b2, w3, b3, fwd_hi, fwd_lo):
    t = jnp.linspace(0.0, 1.0, n_tok, dtype=F32)[:, None]
    w = (2.0 * math.pi / n_tok) * jnp.arange(n_tok, dtype=F32)[:, None]
    bands = jnp.linspace(1e-4, HY_BANDS - 1, HY_BANDS, dtype=F32)[None, :]
    feats = jnp.concatenate([t, jnp.cos(bands * w), -jnp.sin(bands * w)], axis=-1)
    feats = jnp.pad(feats, ((0, 0), (0, LANES - HY_POS_DIM)))
    rates = jnp.abs(jnp.linspace(HY_FAST, HY_SLOW, D_GRP, dtype=F32))
    window = jnp.exp(-t * rates)
    padv = lambda a: jnp.pad(a, (0, LANES - HY_FF)).reshape(1, LANES)
    w1p = jnp.pad(w1, ((0, LANES - HY_POS_DIM), (0, LANES - HY_FF)))
    w2p = jnp.pad(w2, ((0, LANES - HY_FF), (0, LANES - HY_FF)))
    w3p = jnp.pad(w3, ((0, LANES - HY_FF), (0, 0)))
    full = lambda shape: pl.BlockSpec(shape, lambda o: (0,) * len(shape))
    return pl.pallas_call(
        functools.partial(_hyena_filter_kernel, n_tok=n_tok),
        out_shape=jax.ShapeDtypeStruct((HY_ORDER, 2 * n_tok, D_GRP), F32),
        grid=(HY_ORDER,),
        in_specs=[full((n_tok, LANES)), full((LANES, LANES)), full((1, LANES)), full((1, LANES)),
                  full((LANES, LANES)), full((1, LANES)),
                  pl.BlockSpec((LANES, 2 * D_GRP), lambda o: (0, o)),
                  pl.BlockSpec((1, 2 * D_GRP), lambda o: (0, o)),
                  full((n_tok, D_GRP)), full((2 * n_tok, n_tok)), full((2 * n_tok, n_tok))],
        out_specs=pl.BlockSpec((1, 2 * n_tok, D_GRP), lambda o: (o, 0, 0)),
        compiler_params=_cparams(("arbitrary",)),
        name="hyena_filter_spectra",
    )(feats, w1p, padv(b1), padv(freq), w2p, padv(b2), w3p, b3.reshape(1, -1), window, fwd_hi, fwd_lo)


def _hyena_conv_kernel(zy_ref, zx1_ref, zx2_ref, cwy_ref, cw1_ref, cw2_ref, cby_ref, cb1_ref, cb2_ref,
                       skip_ref, g_ref, fwd_ref, inv_ref, out_ref, *, n_tok):
    width = zy_ref.shape[1]
    row = _iota((n_tok, width), 0)
    first, last = row == 0, row == n_tok - 1

    def dwconv(x_ref, w_ref, b_ref):
        x = x_ref[...]
        w = w_ref[...]
        prev = jnp.where(first, 0.0, pltpu.roll(x, 1, 0))
        nxt = jnp.where(last, 0.0, pltpu.roll(x, n_tok - 1, 0))
        return prev * w[0:1] + x * w[1:2] + nxt * w[2:3] + b_ref[...]

    y = dwconv(zy_ref, cwy_ref, cby_ref)
    gates = (dwconv(zx1_ref, cw1_ref, cb1_ref), dwconv(zx2_ref, cw2_ref, cb2_ref))
    fwd = fwd_ref[...]
    for o in range(HY_ORDER):
        u = jnp.dot(fwd, y.astype(BF16), preferred_element_type=F32)
        a, b = u[:n_tok], u[n_tok:]
        c, d = g_ref[o, :n_tok, :], g_ref[o, n_tok:, :]
        bd = b * d
        y_re = a * c - jnp.where(first, 0.0, bd)
        y_im = jnp.where(first, bd, a * d + b * c)
        conv = (jnp.dot(inv_ref[:, :n_tok], y_re.astype(BF16), preferred_element_type=F32)
                + jnp.dot(inv_ref[:, n_tok:], y_im.astype(BF16), preferred_element_type=F32))
        y = gates[o] * (conv + skip_ref[o:o + 1, :] * y)
    out_ref[...] = y.astype(BF16)


def hyena_mixer(z, conv_w, conv_b, skip, spectra, fwd_hi, inv_hi, n_seq, n_tok, row_blk0):
    width = 256
    nw = D_GRP // width
    zcol = lambda base: pl.BlockSpec((n_tok, width), lambda b, j: (row_blk0 + b, base * LANES // width + j))
    wcol = lambda part: pl.BlockSpec((3, width), lambda b, j: (0, part * nw + j))
    bcol = lambda part: pl.BlockSpec((1, width), lambda b, j: (0, part * nw + j))
    cb = conv_b.reshape(1, 3 * D_GRP)
    in_specs = [zcol(ZD_Y), zcol(ZD_X1), zcol(ZD_X2), wcol(0), wcol(1), wcol(2), bcol(0), bcol(1), bcol(2),
                pl.BlockSpec((HY_ORDER, width), lambda b, j: (0, j)),
                pl.BlockSpec((HY_ORDER, 2 * n_tok, width), lambda b, j: (0, 0, j)),
                pl.BlockSpec((2 * n_tok, n_tok), lambda b, j: (0, 0)),
                pl.BlockSpec((n_tok, 2 * n_tok), lambda b, j: (0, 0))]
    args = [z, z, z, conv_w, conv_w, conv_w, cb, cb, cb, skip, spectra, fwd_hi, inv_hi]
    return pl.pallas_call(
        functools.partial(_hyena_conv_kernel, n_tok=n_tok),
        out_shape=jax.ShapeDtypeStruct((n_seq * n_tok, D_GRP), BF16),
        grid=(n_seq, nw),
        in_specs=in_specs,
        out_specs=pl.BlockSpec((n_tok, width), lambda b, j: (b, j)),
        compiler_params=_cparams(("arbitrary", "arbitrary")),
        name="hyena_mixer",
    )(*args)


def _outproj_kernel(*refs, n_ctx_blk):
    ctx_parts, dec_parts = refs[0:N_MIXERS], refs[N_MIXERS:2 * N_MIXERS]
    w_ref, x_ref, g1_ref, ng_ref, sc_ref, sh_ref, x1_ref, h2_ref = refs[2 * N_MIXERS:]
    is_ctx = pl.program_id(0) < n_ctx_blk
    acc = None
    for g in range(N_MIXERS):
        part = jnp.where(is_ctx, ctx_parts[g][...], dec_parts[g][...])
        term = jnp.dot(part, w_ref[g * D_GRP:(g + 1) * D_GRP, :], preferred_element_type=F32)
        acc = term if acc is None else acc + term
    x1 = x_ref[...] + g1_ref[0] * acc
    x1_ref[...] = x1
    h2_ref[...] = _ada_norm(x1, ng_ref[...], sc_ref[0], sh_ref[0]).astype(BF16)


def out_projection(ctx_parts, dec_parts, w_out_bf16, x, mod, layer, norm_g, n_ctx_tok, dec_seq):
    m = x.shape[0]
    tm = 512
    n_ctx_blk = n_ctx_tok // tm
    modspec = lambda part: pl.BlockSpec((1, 1, D_MODEL), _mod_row_index(layer, part, tm, n_ctx_tok, dec_seq))
    ctx_part = pl.BlockSpec((tm, D_GRP), lambda i: (jnp.minimum(i, n_ctx_blk - 1), 0))
    dec_part = pl.BlockSpec((tm, D_GRP), lambda i: (jnp.maximum(i - n_ctx_blk, 0), 0))
    row = pl.BlockSpec((tm, D_MODEL), lambda i: (i, 0))
    return pl.pallas_call(
        functools.partial(_outproj_kernel, n_ctx_blk=n_ctx_blk),
        out_shape=[jax.ShapeDtypeStruct((m, D_MODEL), F32), jax.ShapeDtypeStruct((m, D_MODEL), BF16)],
        grid=(m // tm,),
        in_specs=[ctx_part] * N_MIXERS + [dec_part] * N_MIXERS
        + [pl.BlockSpec((D_MODEL, D_MODEL), lambda i: (0, 0)), row,
           modspec(2), pl.BlockSpec((1, D_MODEL), lambda i: (0, 0)), modspec(4), modspec(3)],
        out_specs=[row, row],
        compiler_params=_cparams(("arbitrary",)),
        name="out_projection",
    )(*ctx_parts, *dec_parts, w_out_bf16, x, mod, norm_g.reshape(1, D_MODEL), mod, mod)


def _ffn_kernel(h_ref, wa_ref, wb_ref, cwa_ref, cwb_ref, cba_ref, cbb_ref, wd_ref, y_ref,
                ua0_ref, ub0_ref, ua1_ref, ub1_ref, *, tile_rows, n_ctx_tok, ctx_seq, dec_seq):
    i = pl.program_id(0)
    j = pl.program_id(1)
    tf = wd_ref.shape[1]
    body = slice(SUBLANES, SUBLANES + tile_rows)

    @pl.when((i == 0) & (j == 0))
    def _():
        for ref in (ua0_ref, ub0_ref, ua1_ref, ub1_ref):
            ref[...] = jnp.zeros_like(ref)

    half = tile_rows // 2

    def park(dst, w_ref):
        dst[body, :] = jnp.dot(h_ref[...], w_ref[0].astype(BF16), preferred_element_type=F32)

    def gated(ready_a, ready_b, r0, seq):
        lo = SUBLANES + r0
        edges = seq != tile_rows
        if edges:
            pos = (r0 + _iota((half, tf), 0)) & (seq - 1)
            first, last = pos == 0, pos == seq - 1

        def dwconv(u_ref, cw_ref, cb_ref):
            cw = cw_ref[0]
            prev = u_ref[lo - 1:lo - 1 + half, :]
            nxt = u_ref[lo + 1:lo + 1 + half, :]
            if edges:
                prev = jnp.where(first, 0.0, prev)
                nxt = jnp.where(last, 0.0, nxt)
            return prev * cw[0:1] + u_ref[lo:lo + half, :] * cw[1:2] + nxt * cw[2:3] + cb_ref[0]

        ua = dwconv(ready_a, cwa_ref, cba_ref)
        ub = dwconv(ready_b, cwb_ref, cbb_ref)
        return (ua * _sigmoid(ua) * ub).astype(BF16)

    def step(park_a, park_b, ready_a, ready_b, first_finish, seq):
        wd = wd_ref[0].astype(BF16)
        for r0, dst, w_ref in ((0, park_a, wa_ref), (half, park_b, wb_ref)):
            park(dst, w_ref)
            contrib = jnp.dot(gated(ready_a, ready_b, r0, seq), wd, preferred_element_type=F32)
            if first_finish:
                y_ref[r0:r0 + half, :] = contrib
            else:
                y_ref[r0:r0 + half, :] += contrib

    @pl.when(j == 0)
    def _():
        park(ua0_ref, wa_ref)
        park(ub0_ref, wb_ref)

    is_ctx = i * tile_rows < n_ctx_tok
    for in_group, seq in ((is_ctx, ctx_seq), (jnp.logical_not(is_ctx), dec_seq)):
        @pl.when(in_group & (j == 1))
        def _():
            step(ua1_ref, ub1_ref, ua0_ref, ub0_ref, True, seq)

        @pl.when(in_group & (j > 1) & (j % 2 == 0))
        def _():
            step(ua0_ref, ub0_ref, ua1_ref, ub1_ref, False, seq)

        @pl.when(in_group & (j > 1) & (j % 2 == 1))
        def _():
            step(ua1_ref, ub1_ref, ua0_ref, ub0_ref, False, seq)


def conv_ffn(h2, layer, w_up, conv_w, conv_b, w_down, n_ctx_tok, ctx_seq, dec_seq):
    m = h2.shape[0]
    tm, tf = 1024, FFN_TF
    nf = D_FF // tf
    for seq in (ctx_seq, dec_seq):
        assert tm % seq == 0 and seq & (seq - 1) == 0, seq
    assert n_ctx_tok % tm == 0 and m % tm == 0
    cb = conv_b.reshape(DEPTH, 1, 2 * D_FF)
    up = lambda j: jnp.minimum(j, nf - 1)
    dn = lambda j: jnp.maximum(j - 1, 0)
    park = pltpu.VMEM((tm + 2 * SUBLANES, tf), F32)
    return pl.pallas_call(
        functools.partial(_ffn_kernel, tile_rows=tm, n_ctx_tok=n_ctx_tok, ctx_seq=ctx_seq, dec_seq=dec_seq),
        out_shape=jax.ShapeDtypeStruct((m, D_MODEL), F32),
        grid=(m // tm, nf + 1),
        in_specs=[pl.BlockSpec((tm, D_MODEL), lambda i, j: (i, 0)),
                  pl.BlockSpec((1, D_MODEL, tf), lambda i, j: (layer, 0, up(j))),
                  pl.BlockSpec((1, D_MODEL, tf), lambda i, j: (layer, 0, nf + up(j))),
                  pl.BlockSpec((1, 3, tf), lambda i, j: (layer, 0, dn(j))),
                  pl.BlockSpec((1, 3, tf), lambda i, j: (layer, 0, nf + dn(j))),
                  pl.BlockSpec((1, 1, tf), lambda i, j: (layer, 0, dn(j))),
                  pl.BlockSpec((1, 1, tf), lambda i, j: (layer, 0, nf + dn(j))),
                  pl.BlockSpec((1, tf, D_MODEL), lambda i, j: (layer, dn(j), 0))],
        out_specs=pl.BlockSpec((tm, D_MODEL), lambda i, j: (i, 0)),
        scratch_shapes=[park, park, park, park],
        compiler_params=_cparams(("arbitrary", "arbitrary")),
        name="conv_ffn",
    )(h2, w_up, w_up, conv_w, conv_w, cb, cb, w_down)


def _residual_kernel(x_ref, y_ref, g_ref, o_ref):
    o_ref[...] = x_ref[...] + g_ref[0] * y_ref[...]


def final_residual(x1, y, mod, layer, row0, n_rows, n_ctx_tok, dec_seq):
    tm = 512
    blk0 = row0 // tm

    def mod_index(i):
        start = (blk0 + i) * tm
        row = jnp.where(start < n_ctx_tok, 0, 1 + (start - n_ctx_tok) // dec_seq)
        return (layer * 48 + row * 6 + 5, 0, 0)

    return pl.pallas_call(
        _residual_kernel,
        out_shape=jax.ShapeDtypeStruct((n_rows, D_MODEL), F32),
        grid=(n_rows // tm,),
        in_specs=[pl.BlockSpec((tm, D_MODEL), lambda i: (blk0 + i, 0)),
                  pl.BlockSpec((tm, D_MODEL), lambda i: (blk0 + i, 0)),
                  pl.BlockSpec((1, 1, D_MODEL), mod_index)],
        out_specs=pl.BlockSpec((tm, D_MODEL), lambda i: (i, 0)),
        compiler_params=_cparams(("arbitrary",)),
        name="final_residual",
    )(x1, y, mod)


def kernel(x_prompt, x_sample, cache_diff_k, cache_diff_v, cache_swa_k, cache_swa_v, state_mlstm_C, state_mlstm_n, state_mlstm_m, c, c_ctx, w_mod, b_mod, norm1_g, norm2_g, w_in, mlstm_ig_b, mlstm_fg_b, mlstm_norm_g, diff_qn_g, diff_kn_g, diff_lam, diff_out_g, swa_qn_g, swa_kn_g, swa_sink, hy_conv_w, hy_conv_b, hy_w1, hy_b1, hy_freq, hy_w2, hy_b2, hy_w3, hy_b3, hy_bias, w_out, ffn_w_up, ffn_conv_w, ffn_conv_b, ffn_w_down):
    n_ctx, ctx_seq = x_prompt.shape[:2]
    n_dec, dec_seq = x_sample.shape[:2]
    n_ctx_tok = n_ctx * ctx_seq
    n_dec_tok = n_dec * dec_seq
    dec_blk0 = n_ctx_tok // dec_seq

    x = (x_prompt.reshape(n_ctx_tok, D_MODEL), x_sample.reshape(n_dec_tok, D_MODEL))
    w_pack, wg_hi, wg_lo = pack_in_weights(w_in)
    cond8 = jnp.concatenate([c_ctx[None, :], c, jnp.zeros((8 - 1 - n_dec, D_MODEL), F32)], axis=0)
    mod = mod_vectors(cond8, w_mod, b_mod)

    dft = {}
    for t in (ctx_seq, dec_seq):
        fwd, inv = _dft_matrices(t)
        dft[t] = dict(fwd=fwd.astype(BF16), inv=inv.astype(BF16), split=_split2(fwd))

    delta = None
    caches = []
    for l in range(DEPTH):
        x, z, gates = in_projection(x, delta, mod, l, norm1_g[l], w_pack, wg_hi, wg_lo, n_ctx_tok, dec_seq)

        gate_bias = jnp.pad(jnp.concatenate([mlstm_ig_b[l].reshape(-1), mlstm_fg_b[l].reshape(-1)]),
                            (0, LANES - 4 * H_A)).reshape(1, LANES)
        lam_init = 0.8 - 0.6 * math.exp(-0.3 * l)

        a_ctx, c_new, n_new, m_new = mlstm_mixer(z, gates, gate_bias, mlstm_norm_g[l], None,
                                                 n_ctx, ctx_seq, 0, ctx_seq, True)
        b_ctx, diff_k = diff_mixer(z, diff_qn_g[l], diff_kn_g[l], diff_lam[l], diff_out_g[l], lam_init,
                                   None, n_ctx, ctx_seq, 0)
        c_ctx_out, swa_k = swa_mixer(z, swa_qn_g[l], swa_kn_g[l], swa_sink[l], None, n_ctx, ctx_seq, 0)
        spec_ctx = hyena_filter_spectra(ctx_seq, hy_w1[l], hy_b1[l], hy_freq[l], hy_w2[l], hy_b2[l],
                                        hy_w3[l], hy_b3[l], *dft[ctx_seq]["split"])
        d_ctx = hyena_mixer(z, hy_conv_w[l], hy_conv_b[l], hy_bias[l], spec_ctx, dft[ctx_seq]["fwd"],
                            dft[ctx_seq]["inv"], n_ctx, ctx_seq, 0)
        a_dec, = mlstm_mixer(z, gates, gate_bias, mlstm_norm_g[l],
                             (state_mlstm_C[:, l], state_mlstm_n[:, l], state_mlstm_m[:, l]),
                             n_dec, dec_seq, dec_blk0, 512, False)
        b_dec, = diff_mixer(z, diff_qn_g[l], diff_kn_g[l], diff_lam[l], diff_out_g[l], lam_init,
                            (cache_diff_k[:, l], cache_diff_v[:, l]), n_dec, dec_seq, dec_blk0)
        c_dec, = swa_mixer(z, swa_qn_g[l], swa_kn_g[l], swa_sink[l],
                           (cache_swa_k[:, l], cache_swa_v[:, l]), n_dec, dec_seq, dec_blk0)
        spec_dec = hyena_filter_spectra(dec_seq, hy_w1[l], hy_b1[l], hy_freq[l], hy_w2[l], hy_b2[l],
                                        hy_w3[l], hy_b3[l], *dft[dec_seq]["split"])
        d_dec = hyena_mixer(z, hy_conv_w[l], hy_conv_b[l], hy_bias[l], spec_dec, dft[dec_seq]["fwd"],
                            dft[dec_seq]["inv"], n_dec, dec_seq, dec_blk0)

        x, h2 = out_projection((a_ctx, b_ctx, c_ctx_out, d_ctx), (a_dec, b_dec, c_dec, d_dec),
                               w_out[l].astype(BF16), x, mod, l, norm2_g[l], n_ctx_tok, dec_seq)
        delta = conv_ffn(h2, l, ffn_w_up, ffn_conv_w, ffn_conv_b, ffn_w_down, n_ctx_tok, ctx_seq, dec_seq)

        zc = z[:n_ctx_tok]
        caches.append((
            diff_k.reshape(n_ctx, ctx_seq, H_B, 2, DK_B),
            zc[:, ZB_V * LANES:(ZB_V + 4) * LANES].reshape(n_ctx, ctx_seq, H_B, DV_B),
            swa_k.reshape(n_ctx, ctx_seq, KV_C, DH_C),
            zc[:, ZC_V * LANES:(ZC_V + 2) * LANES].reshape(n_ctx, ctx_seq, KV_C, DH_C),
            c_new,
            n_new.reshape(n_ctx, 2, H_A, DH_A),
            m_new.reshape(n_ctx, 2, H_A),
        ))

    y_prompt = final_residual(x, delta, mod, DEPTH - 1, 0, n_ctx_tok, n_ctx_tok, dec_seq)
    y_sample = final_residual(x, delta, mod, DEPTH - 1, n_ctx_tok, n_dec_tok, n_ctx_tok, dec_seq)
    outs = [jnp.stack([s[i] for s in caches], axis=1) for i in range(7)]
    return (y_prompt.reshape(n_ctx, ctx_seq, D_MODEL), y_sample.reshape(n_dec, dec_seq, D_MODEL), *outs)
```

```python
import functools
import math

import jax
import jax.numpy as jnp
from jax import lax
from jax.experimental import pallas as pl
from jax.experimental.pallas import tpu as pltpu

D_MODEL = 2048
DEPTH = 2
GRID_W = 64
D_GRP = 512
N_MIXERS = D_MODEL // D_GRP
H_A = 4
DH_A = 128
H_B = 4
DV_B = 128
DK_B = 64
H_C = 4
KV_C = 2
G_C = 2
DH_C = 128
WINDOW = 128
HY_ORDER = 2
HY_POS_DIM = 33
HY_BANDS = (HY_POS_DIM - 1) // 2
HY_FF = 64
HY_FAST = math.log(1e-2) / 0.3
HY_SLOW = math.log(1e-2) / 1.5
D_FF = 5632
ROPE_BASE = 10000.0
EPS = 1e-6
N_A = 4 * D_GRP + 4 * H_A

LANES = 128
SUBLANES = 8
V7X_VMEM_LIMIT_BYTES = 56 * 1024 * 1024
Z_COLS = 12 * D_GRP
INPROJ_TN = 3 * D_GRP
FFN_TF = 256
ZA_Q, ZA_K, ZA_V, ZA_O = 0, 4, 8, 12
ZB_Q, ZB_K, ZB_V = 16, 20, 24
ZC_Q, ZC_K, ZC_V = 28, 32, 34
ZD_Y, ZD_X1, ZD_X2 = 36, 40, 44

BF16 = jnp.bfloat16
F32 = jnp.float32


def _cparams(sem):
    return pltpu.CompilerParams(dimension_semantics=sem, vmem_limit_bytes=V7X_VMEM_LIMIT_BYTES)


def _dot(a, b):
    return jnp.dot(a.astype(BF16), b.astype(BF16), preferred_element_type=F32)


def _dot_nt(a, b):
    return lax.dot_general(a.astype(BF16), b.astype(BF16), (((1,), (1,)), ((), ())),
                           preferred_element_type=F32)


def _bf16_head(a):
    bits = lax.bitcast_convert_type(a, jnp.int32) & jnp.int32(-65536)
    return lax.bitcast_convert_type(bits, F32)


def _split2(a):
    head = _bf16_head(a)
    return head.astype(BF16), (a - head).astype(BF16)


def _split3(a):
    head = _bf16_head(a)
    rest = a - head
    mid = _bf16_head(rest)
    return head.astype(BF16), mid.astype(BF16), (rest - mid).astype(BF16)


def _dot3(a, b):
    ah, al = _split2(a)
    bh, bl = _split2(b)
    return (jnp.dot(ah, bh, preferred_element_type=F32) + jnp.dot(al, bh, preferred_element_type=F32)
            + jnp.dot(ah, bl, preferred_element_type=F32))


def _iota(shape, dim):
    return lax.broadcasted_iota(jnp.int32, shape, dim)


def _sigmoid(x):
    return 1.0 / (1.0 + jnp.exp(-x))


def _log_sigmoid(x):
    return jnp.minimum(x, 0.0) - jnp.log(1.0 + jnp.exp(-jnp.abs(x)))


def _mod_kernel(c_ref, w_ref, b_ref, o_ref):
    k = pl.program_id(1)
    c = c_ref[...]
    part = _dot(c * _sigmoid(c), w_ref[0])

    @pl.when(k == 0)
    def _():
        o_ref[0] = part + b_ref[0]

    @pl.when(k > 0)
    def _():
        o_ref[0] += part


def mod_vectors(cond8, w_mod, b_mod):
    tk = 256
    n = w_mod.shape[2]
    out = pl.pallas_call(
        _mod_kernel,
        out_shape=jax.ShapeDtypeStruct((DEPTH, 8, n), F32),
        grid=(DEPTH, D_MODEL // tk),
        in_specs=[pl.BlockSpec((8, tk), lambda l, k: (0, k)),
                  pl.BlockSpec((1, tk, n), lambda l, k: (l, k, 0)),
                  pl.BlockSpec((1, 1, n), lambda l, k: (l, 0, 0))],
        out_specs=pl.BlockSpec((1, 8, n), lambda l, k: (l, 0, 0)),
        compiler_params=_cparams(("arbitrary", "arbitrary")),
        name="mod_vectors",
    )(cond8, w_mod, b_mod.reshape(DEPTH, 1, n))
    return out.reshape(DEPTH * 8 * 6, 1, D_MODEL)


def _mod_row_index(layer, part, tile_rows, n_ctx_tok, dec_seq):
    def index(i, *_):
        start = i * tile_rows
        row = jnp.where(start < n_ctx_tok, 0, 1 + (start - n_ctx_tok) // dec_seq)
        return (layer * 48 + row * 6 + part, 0, 0)
    return index


def _ada_norm(x, ng, sc, sh):
    y = x * lax.rsqrt(jnp.mean(x * x, axis=-1, keepdims=True) + EPS)
    return y * ng * (1.0 + sc) + sh


def _pack_in_kernel(a_ref, b_ref, w_ref, gh_ref, gl_ref):
    c = pl.program_id(1)
    n_aligned = 4 * D_GRP // a_ref.shape[1]
    gap = 4 * H_A

    @pl.when(c < n_aligned)
    def _():
        w_ref[0, 0] = a_ref[0].T.astype(BF16)

    @pl.when(c >= n_aligned)
    def _():
        w_ref[0, 0] = jnp.concatenate([a_ref[0, gap:, :], b_ref[0]], axis=0).T.astype(BF16)

    @pl.when(c == n_aligned)
    def _():
        rows = a_ref[0, 0:LANES, :]
        hi, lo = _split2(jnp.where(_iota(rows.shape, 0) < gap, rows, 0.0).T)
        gh_ref[0] = hi
        gl_ref[0] = lo


def pack_in_weights(w_in):
    tc = 512
    gap = 4 * H_A
    n_blk = Z_COLS // tc
    per_tile = INPROJ_TN // tc
    w_t = jnp.swapaxes(w_in, 1, 2)
    gate = pl.BlockSpec((1, D_MODEL, LANES), lambda l, c: (l, 0, 0))
    return pl.pallas_call(
        _pack_in_kernel,
        out_shape=[jax.ShapeDtypeStruct((DEPTH, Z_COLS // INPROJ_TN, D_MODEL, INPROJ_TN), BF16),
                   jax.ShapeDtypeStruct((DEPTH, D_MODEL, LANES), BF16),
                   jax.ShapeDtypeStruct((DEPTH, D_MODEL, LANES), BF16)],
        grid=(DEPTH, n_blk),
        in_specs=[pl.BlockSpec((1, tc, D_MODEL), lambda l, c: (l, c, 0)),
                  pl.BlockSpec((1, gap, D_MODEL), lambda l, c: (l, (c + 1) * (tc // gap), 0))],
        out_specs=[pl.BlockSpec((1, 1, D_MODEL, tc), lambda l, c: (l, c // per_tile, 0, c % per_tile)),
                   gate, gate],
        compiler_params=_cparams(("arbitrary", "arbitrary")),
        name="pack_in_weights",
    )(w_t, w_t)


def _inproj_kernel(*refs, has_delta, n_ctx_blk):
    if has_delta:
        (x_ref, y_ref, g2_ref, ng_ref, sc_ref, sh_ref, w_ref, wgh_ref, wgl_ref,
         xn_ref, z_ref, gate_ref, hb_ref) = refs
    else:
        (xc_ref, xd_ref, ng_ref, sc_ref, sh_ref, w_ref, wgh_ref, wgl_ref,
         xn_ref, z_ref, gate_ref, hb_ref) = refs

    i = pl.program_id(0)

    @pl.when(pl.program_id(1) == 0)
    def _():
        if has_delta:
            x = x_ref[...] + g2_ref[0] * y_ref[...]
        else:
            x = jnp.where(i < n_ctx_blk, xc_ref[...], xd_ref[...])
        xn_ref[...] = x
        h = _ada_norm(x, ng_ref[...], sc_ref[0], sh_ref[0])
        hb_ref[...] = h.astype(BF16)
        hh, hl = _split2(h)
        wgh = wgh_ref[0]
        gate_ref[...] = (jnp.dot(hh, wgh, preferred_element_type=F32)
                         + jnp.dot(hl, wgh, preferred_element_type=F32)
                         + jnp.dot(hh, wgl_ref[0], preferred_element_type=F32))

    z_ref[...] = jnp.dot(hb_ref[...], w_ref[0, 0], preferred_element_type=F32)


def in_projection(x, delta, mod, layer, norm_g, w_pack, wg_hi, wg_lo, n_ctx_tok, dec_seq):
    tm, tn = 512, INPROJ_TN
    has_delta = delta is not None
    n_ctx_blk = n_ctx_tok // tm
    row = pl.BlockSpec((tm, D_MODEL), lambda i, j: (i, 0))
    modspec = lambda part: pl.BlockSpec((1, 1, D_MODEL), _mod_row_index(layer, part, tm, n_ctx_tok, dec_seq))
    vec = pl.BlockSpec((1, D_MODEL), lambda i, j: (0, 0))
    if has_delta:
        m = x.shape[0]
        g2spec = pl.BlockSpec((1, 1, D_MODEL), _mod_row_index(layer - 1, 5, tm, n_ctx_tok, dec_seq))
        in_specs = [row, row, g2spec]
        args = [x, delta, mod]
    else:
        x_ctx, x_dec = x
        m = x_ctx.shape[0] + x_dec.shape[0]
        in_specs = [pl.BlockSpec((tm, D_MODEL), lambda i, j: (jnp.minimum(i, n_ctx_blk - 1), 0)),
                    pl.BlockSpec((tm, D_MODEL), lambda i, j: (jnp.maximum(i - n_ctx_blk, 0), 0))]
        args = [x_ctx, x_dec]
    gate_w = pl.BlockSpec((1, D_MODEL, LANES), lambda i, j: (layer, 0, 0))
    in_specs += [vec, modspec(1), modspec(0),
                 pl.BlockSpec((1, 1, D_MODEL, tn), lambda i, j: (layer, j, 0, 0)), gate_w, gate_w]
    args += [norm_g.reshape(1, D_MODEL), mod, mod, w_pack, wg_hi, wg_lo]
    return pl.pallas_call(
        functools.partial(_inproj_kernel, has_delta=has_delta, n_ctx_blk=n_ctx_blk),
        out_shape=[jax.ShapeDtypeStruct((m, D_MODEL), F32), jax.ShapeDtypeStruct((m, Z_COLS), F32),
                   jax.ShapeDtypeStruct((m, LANES), F32)],
        grid=(m // tm, Z_COLS // tn),
        in_specs=in_specs,
        out_specs=[row, pl.BlockSpec((tm, tn), lambda i, j: (i, j)), pl.BlockSpec((tm, LANES), lambda i, j: (i, 0))],
        scratch_shapes=[pltpu.VMEM((tm, D_MODEL), BF16)],
        compiler_params=_cparams(("arbitrary", "arbitrary")),
        name="in_projection",
    )(*args)


def _mlstm_kernel(*refs, n_tok, chunk, has_state, want_state):
    refs = list(refs)
    q_ref, k_ref, v_ref, o_ref, gate_ref, gb_ref, ng_ref = refs[:7]
    pos = 7
    if has_state:
        c0_ref, n0_ref, m0_ref = refs[pos:pos + 3]
        pos += 3
    out_ref = refs[pos]
    pos += 1
    if want_state:
        cn_ref, nn_ref, mn_ref = refs[pos:pos + 3]
        pos += 3
    hacc_ref = refs[pos]

    L = chunk
    nc = n_tok // L
    scale = DH_A ** -0.5
    r_idx = _iota((L, L), 0)
    c_idx = _iota((L, L), 1)
    ones_b = jnp.ones((L, DH_A), BF16)

    g = gate_ref[...] + gb_ref[...]
    vals = jnp.where(_iota((n_tok, LANES), 1) < 2 * H_A, g, _log_sigmoid(g))

    for d in range(2):
        causal = (c_idx <= r_idx) if d == 0 else (c_idx >= r_idx)
        tri = jnp.where(causal, 1.0, 0.0).astype(BF16)
        if has_state:
            c_mem = [c0_ref[0, d, h] for h in range(H_A)]
            n_mem = [n0_ref[0, d, h] for h in range(H_A)]
            m_prev = [m0_ref[0, d, h] for h in range(H_A)]
            pad = jnp.zeros((SUBLANES - 1, DH_A), F32)
            n_mat = [_dot(jnp.concatenate([n, pad], axis=0).T, jnp.ones((SUBLANES, DH_A), BF16)) for n in n_mem]
        else:
            c_mem = [None] * H_A
            n_mem = [None] * H_A
            n_mat = [None] * H_A
            m_prev = [jnp.zeros((1, 1), F32)] * H_A
        order = range(nc) if d == 0 else range(nc - 1, -1, -1)
        for ci, c in enumerate(order):
            rows = slice(c * L, (c + 1) * L)
            v_c = vals[rows]
            p0, p1, p2 = _split3(v_c)
            cum = (jnp.dot(tri, p0, preferred_element_type=F32) + jnp.dot(tri, p1, preferred_element_type=F32)
                   + jnp.dot(tri, p2, preferred_element_type=F32))
            v_t = v_c.T
            cum_t = cum.T
            for h in range(H_A):
                col_i = d * H_A + h
                col_f = 2 * H_A + d * H_A + h
                lanes = slice(h * DH_A, (h + 1) * DH_A)
                c_col = cum[:, col_f:col_f + 1]
                ig_col = v_c[:, col_i:col_i + 1]
                c_row = cum_t[col_f:col_f + 1, :]
                ig_row = v_t[col_i:col_i + 1, :]

                dmat = jnp.where(causal, c_col - c_row + ig_row, -jnp.inf)
                inter = c_col + m_prev[h]
                m_col = jnp.maximum(inter, jnp.max(dmat, axis=1, keepdims=True))
                dexp = jnp.exp(dmat - m_col)
                qs = q_ref[rows, lanes] * scale
                qb = qs.astype(BF16)
                kf = k_ref[rows, lanes]
                vb = v_ref[rows, lanes].astype(BF16)
                pb = (_dot_nt(qb, kf) * dexp).astype(BF16)
                num = jnp.dot(pb, vb, preferred_element_type=F32)
                den = jnp.dot(pb, ones_b, preferred_element_type=F32)
                carried = c_mem[h] is not None
                if carried:
                    w_inter = jnp.broadcast_to(jnp.exp(inter - m_col), (L, DH_A))
                    num = num + w_inter * _dot(qb, c_mem[h])
                    den = den + w_inter * _dot(qb, n_mat[h])
                hh = num / jnp.maximum(jnp.abs(den), jnp.exp(-m_col))
                if d == 0:
                    hacc_ref[rows, lanes] = hh
                else:
                    hacc_ref[rows, lanes] += hh

                if want_state or ci < nc - 1:
                    total = c_col[L - 1:L, :] if d == 0 else c_col[0:1, :]
                    w_row = total - c_row + ig_row
                    m_new = jnp.maximum(total + m_prev[h], jnp.max(w_row, axis=1, keepdims=True))
                    decay = jnp.exp(total + m_prev[h] - m_new)
                    wk = jnp.exp(total - c_col + ig_col - m_new)
                    kw = kf * wk
                    kw_t = kw.T.astype(BF16)
                    c_upd = jnp.dot(kw_t, vb, preferred_element_type=F32)
                    c_mem[h] = decay * c_mem[h] + c_upd if carried else c_upd
                    if ci < nc - 1:
                        n_upd = jnp.dot(kw_t, ones_b, preferred_element_type=F32)
                        n_mat[h] = decay * n_mat[h] + n_upd if carried else n_upd
                    if want_state:
                        n_row = jnp.sum(kw, axis=0, keepdims=True)
                        n_mem[h] = decay * n_mem[h] + n_row if carried else n_row
                    m_prev[h] = m_new
        if want_state:
            for h in range(H_A):
                cn_ref[0, d, h] = c_mem[h]
                nn_ref[0, d, h] = n_mem[h]
                mn_ref[0, d, h] = m_prev[h]

    for h in range(H_A):
        lanes = slice(h * DH_A, (h + 1) * DH_A)
        hs = hacc_ref[:, lanes]
        y = hs * lax.rsqrt(jnp.mean(hs * hs, axis=1, keepdims=True) + EPS) * ng_ref[:, lanes]
        out_ref[:, lanes] = (y * _sigmoid(o_ref[:, lanes])).astype(BF16)


def mlstm_mixer(z, gates, gate_bias, norm_g, state, n_seq, n_tok, row_blk0, chunk, want_state):
    has_state = state is not None
    col = lambda base: pl.BlockSpec((n_tok, D_GRP), lambda b: (row_blk0 + b, base * LANES // D_GRP))
    in_specs = [col(ZA_Q), col(ZA_K), col(ZA_V), col(ZA_O),
                pl.BlockSpec((n_tok, LANES), lambda b: (row_blk0 + b, 0)),
                pl.BlockSpec((1, LANES), lambda b: (0, 0)),
                pl.BlockSpec((1, D_GRP), lambda b: (0, 0))]
    args = [z, z, z, z, gates, gate_bias, norm_g.reshape(1, D_GRP)]
    st_specs = [pl.BlockSpec((1, 2, H_A, DH_A, DH_A), lambda b: (b, 0, 0, 0, 0)),
                pl.BlockSpec((1, 2, H_A, 1, DH_A), lambda b: (b, 0, 0, 0, 0)),
                pl.BlockSpec((1, 2, H_A, 1, 1), lambda b: (b, 0, 0, 0, 0))]
    if has_state:
        c0, n0, m0 = state
        in_specs += st_specs
        args += [c0, n0.reshape(n_seq, 2, H_A, 1, DH_A), m0.reshape(n_seq, 2, H_A, 1, 1)]
    out_shape = [jax.ShapeDtypeStruct((n_seq * n_tok, D_GRP), BF16)]
    out_specs = [pl.BlockSpec((n_tok, D_GRP), lambda b: (b, 0))]
    if want_state:
        out_shape += [jax.ShapeDtypeStruct((n_seq, 2, H_A, DH_A, DH_A), F32),
                      jax.ShapeDtypeStruct((n_seq, 2, H_A, 1, DH_A), F32),
                      jax.ShapeDtypeStruct((n_seq, 2, H_A, 1, 1), F32)]
        out_specs += st_specs
    outs = pl.pallas_call(
        functools.partial(_mlstm_kernel, n_tok=n_tok, chunk=chunk, has_state=has_state, want_state=want_state),
        out_shape=out_shape,
        grid=(n_seq,),
        in_specs=in_specs,
        out_specs=out_specs,
        scratch_shapes=[pltpu.VMEM((n_tok, D_GRP), F32)],
        compiler_params=_cparams(("arbitrary",)),
        name="mlstm_mixer",
    )(*args)
    return outs


def _rope_tables(n_tok, dh):
    rows = n_tok // GRID_W
    t_row = jnp.repeat(jnp.arange(rows, dtype=F32), GRID_W)
    t_col = jnp.tile(jnp.arange(GRID_W, dtype=F32), rows)
    n_freq = dh // 4
    inv = ROPE_BASE ** (-jnp.arange(n_freq, dtype=F32) / n_freq)
    ang = jnp.concatenate([t_row[:, None] * inv, t_col[:, None] * inv], axis=-1)
    cos, sin = jnp.cos(ang), jnp.sin(ang)
    reps = LANES // dh
    cos_t = jnp.tile(jnp.concatenate([cos, cos], axis=-1), (1, reps))
    sin_t = jnp.tile(jnp.concatenate([-sin, sin], axis=-1), (1, reps))
    return cos_t, sin_t


def _rope(x, cos_t, sin_t, dh):
    half = dh // 2
    if dh == LANES:
        partner = pltpu.roll(x, half, 1)
    else:
        first = (_iota(x.shape, 1) & half) == 0
        partner = jnp.where(first, pltpu.roll(x, LANES - half, 1), pltpu.roll(x, half, 1))
    return x * cos_t + partner * sin_t


def _rms(x, g):
    return x * lax.rsqrt(jnp.mean(x * x, axis=1, keepdims=True) + EPS) * g


def _diff_kernel(*refs, n_tok, q_blk, lam_init, has_ctx):
    refs = list(refs)
    q_ref, k_ref, v_ref, qg_ref, kg_ref, lam_ref, og_ref = refs[:7]
    pos = 7
    if has_ctx:
        ck_ref, cv_ref, cos_ref, sin_ref = refs[pos:pos + 4]
        pos += 4
    out_ref = refs[pos]
    pos += 1
    if not has_ctx:
        kout_ref = refs[pos]

    grp0 = _iota((n_tok, LANES), 1) < DK_B

    def group_norm(x, g):
        x2 = x * x
        s0 = jnp.sum(jnp.where(grp0, x2, 0.0), axis=1, keepdims=True)
        s1 = jnp.sum(jnp.where(grp0, 0.0, x2), axis=1, keepdims=True)
        ms = jnp.where(grp0, s0, s1) * (1.0 / DK_B)
        return x * lax.rsqrt(ms + EPS) * g

    qn = group_norm(q_ref[...], qg_ref[...])
    kn = group_norm(k_ref[...], kg_ref[...])
    if has_ctx:
        cos_t, sin_t = cos_ref[...], sin_ref[...]
        qn = _rope(qn, cos_t, sin_t, DK_B)
        kn = _rope(kn, cos_t, sin_t, DK_B)
    else:
        kout_ref[...] = kn
    kb = kn.astype(BF16)
    vb = v_ref[...].astype(BF16)
    if has_ctx:
        ckb = ck_ref[0].astype(BF16)
        cvb = cv_ref[0].astype(BF16)

    lp = lam_ref[...]
    lam = (jnp.exp(jnp.sum(lp[0:1] * lp[1:2], axis=1, keepdims=True))
           - jnp.exp(jnp.sum(lp[2:3] * lp[3:4], axis=1, keepdims=True)) + lam_init)
    scale = DK_B ** -0.5
    assert math.frexp(scale)[0] == 0.5
    grp0_q = _iota((q_blk, LANES), 1) < DK_B

    for qi in range(n_tok // q_blk):
        rows = slice(qi * q_blk, (qi + 1) * q_blk)
        q_rows = qn[rows] * scale
        outs = []
        for comp in range(2):
            keep = grp0_q if comp == 0 else jnp.logical_not(grp0_q)
            qm = jnp.where(keep, q_rows, 0.0).astype(BF16)
            s = _dot_nt(qm, kb)
            mx = jnp.max(s, axis=1, keepdims=True)
            if has_ctx:
                sc = _dot_nt(qm, ckb)
                mx = jnp.maximum(mx, jnp.max(sc, axis=1, keepdims=True))
            p = jnp.exp(s - mx)
            den = jnp.sum(p, axis=1, keepdims=True)
            acc = _dot(p, vb)
            if has_ctx:
                pc = jnp.exp(sc - mx)
                den = den + jnp.sum(pc, axis=1, keepdims=True)
                acc = acc + _dot(pc, cvb)
            outs.append(acc / den)
        a = outs[0] - lam * outs[1]
        out_ref[rows, :] = (_rms(a, og_ref[...]) * (1.0 - lam_init)).astype(BF16)


def diff_mixer(z, qn_g, kn_g, lam, out_g, lam_init, ctx, n_seq, n_tok, row_blk0):
    has_ctx = ctx is not None
    col = lambda base: pl.BlockSpec((n_tok, LANES), lambda b, h: (row_blk0 + b, base + h))
    vec = pl.BlockSpec((1, LANES), lambda b, h: (0, 0))
    in_specs = [col(ZB_Q), col(ZB_K), col(ZB_V), vec, vec,
                pl.BlockSpec((4, LANES), lambda b, h: (0, 0)), vec]
    args = [z, z, z, jnp.tile(qn_g, 2).reshape(1, LANES), jnp.tile(kn_g, 2).reshape(1, LANES),
            jnp.pad(lam, ((0, 0), (0, LANES - DK_B))), out_g.reshape(1, LANES)]
    if has_ctx:
        ck, cv = ctx
        s_ctx = ck.shape[1]
        cos_t, sin_t = _rope_tables(n_tok, DK_B)
        tab = pl.BlockSpec((n_tok, LANES), lambda b, h: (0, 0))
        cspec = pl.BlockSpec((1, s_ctx, LANES), lambda b, h: (b, 0, h))
        in_specs += [cspec, cspec, tab, tab]
        args += [ck.reshape(n_seq, s_ctx, H_B * 2 * DK_B), cv.reshape(n_seq, s_ctx, H_B * DV_B), cos_t, sin_t]
    out_shape = [jax.ShapeDtypeStruct((n_seq * n_tok, D_GRP), BF16)]
    out_specs = [pl.BlockSpec((n_tok, LANES), lambda b, h: (b, h))]
    if not has_ctx:
        out_shape.append(jax.ShapeDtypeStruct((n_seq * n_tok, D_GRP), F32))
        out_specs.append(pl.BlockSpec((n_tok, LANES), lambda b, h: (b, h)))
    return pl.pallas_call(
        functools.partial(_diff_kernel, n_tok=n_tok, q_blk=min(n_tok, 256), lam_init=lam_init, has_ctx=has_ctx),
        out_shape=out_shape,
        grid=(n_seq, H_B),
        in_specs=in_specs,
        out_specs=out_specs,
        compiler_params=_cparams(("arbitrary", "arbitrary")),
        name="diff_mixer",
    )(*args)


def _swa_kernel(*refs, n_tok, q_blk, has_ctx):
    refs = list(refs)
    sink_ref, q_ref, k_ref, v_ref, qg_ref, kg_ref = refs[:6]
    pos = 6
    if has_ctx:
        ck_ref, cv_ref, cos_ref, sin_ref = refs[pos:pos + 4]
        pos += 4
    out_ref = refs[pos]
    pos += 1
    if not has_ctx:
        kout_ref = refs[pos]

    kv = pl.program_id(1)
    kn = _rms(k_ref[...], kg_ref[...])
    if has_ctx:
        cos_t, sin_t = cos_ref[...], sin_ref[...]
        kn = _rope(kn, cos_t, sin_t, DH_C)
        ckb = ck_ref[0].astype(BF16)
        cvb = cv_ref[0].astype(BF16)
    else:
        kout_ref[...] = kn
    kb = kn.astype(BF16)
    vb = v_ref[...].astype(BF16)
    scale = DH_C ** -0.5

    for g in range(G_C):
        sink = sink_ref[kv * G_C + g]
        qn = _rms(q_ref[:, g * DH_C:(g + 1) * DH_C], qg_ref[...])
        if has_ctx:
            qn = _rope(qn, cos_t, sin_t, DH_C)
        for qi in range(n_tok // q_blk):
            rows = slice(qi * q_blk, (qi + 1) * q_blk)
            qb = qn[rows].astype(BF16)
            s = _dot_nt(qb, kb) * scale
            if has_ctx:
                qpos = qi * q_blk + _iota((q_blk, n_tok), 0)
                kpos = _iota((q_blk, n_tok), 1)
                s = jnp.where(jnp.abs(kpos - qpos) <= WINDOW, s, -jnp.inf)
            mx = jnp.maximum(jnp.max(s, axis=1, keepdims=True), sink)
            if has_ctx:
                sc = _dot_nt(qb, ckb) * scale
                mx = jnp.maximum(mx, jnp.max(sc, axis=1, keepdims=True))
            p = jnp.exp(s - mx)
            den = jnp.sum(p, axis=1, keepdims=True) + jnp.exp(sink - mx)
            acc = _dot(p, vb)
            if has_ctx:
                pc = jnp.exp(sc - mx)
                den = den + jnp.sum(pc, axis=1, keepdims=True)
                acc = acc + _dot(pc, cvb)
            out_ref[rows, g * DH_C:(g + 1) * DH_C] = (acc / den).astype(BF16)


def swa_mixer(z, qn_g, kn_g, sink, ctx, n_seq, n_tok, row_blk0):
    has_ctx = ctx is not None
    vec = pl.BlockSpec((1, LANES), lambda b, kv: (0, 0))
    in_specs = [pl.BlockSpec(memory_space=pltpu.SMEM),
                pl.BlockSpec((n_tok, G_C * DH_C), lambda b, kv: (row_blk0 + b, ZC_Q // G_C + kv)),
                pl.BlockSpec((n_tok, LANES), lambda b, kv: (row_blk0 + b, ZC_K + kv)),
                pl.BlockSpec((n_tok, LANES), lambda b, kv: (row_blk0 + b, ZC_V + kv)),
                vec, vec]
    args = [sink.astype(F32), z, z, z, qn_g.reshape(1, LANES), kn_g.reshape(1, LANES)]
    if has_ctx:
        ck, cv = ctx
        s_ctx = ck.shape[1]
        cos_t, sin_t = _rope_tables(n_tok, DH_C)
        tab = pl.BlockSpec((n_tok, LANES), lambda b, kv: (0, 0))
        cspec = pl.BlockSpec((1, s_ctx, LANES), lambda b, kv: (b, 0, kv))
        in_specs += [cspec, cspec, tab, tab]
        args += [ck.reshape(n_seq, s_ctx, KV_C * DH_C), cv.reshape(n_seq, s_ctx, KV_C * DH_C), cos_t, sin_t]
    out_shape = [jax.ShapeDtypeStruct((n_seq * n_tok, D_GRP), BF16)]
    out_specs = [pl.BlockSpec((n_tok, G_C * DH_C), lambda b, kv: (b, kv))]
    if not has_ctx:
        out_shape.append(jax.ShapeDtypeStruct((n_seq * n_tok, KV_C * DH_C), F32))
        out_specs.append(pl.BlockSpec((n_tok, LANES), lambda b, kv: (b, kv)))
    return pl.pallas_call(
        functools.partial(_swa_kernel, n_tok=n_tok, q_blk=min(n_tok, 256), has_ctx=has_ctx),
        out_shape=out_shape,
        grid=(n_seq, KV_C),
        in_specs=in_specs,
        out_specs=out_specs,
        compiler_params=_cparams(("arbitrary", "arbitrary")),
        name="swa_mixer",
    )(*args)


def _dft_matrices(n_tok):
    k = jnp.arange(n_tok, dtype=jnp.int32)[:, None]
    s = jnp.arange(n_tok, dtype=jnp.int32)[None, :]
    ang = (math.pi / n_tok) * ((k * s) % (2 * n_tok)).astype(F32)
    cos, sin = jnp.cos(ang), jnp.sin(ang)
    alt_s = jnp.where(s % 2 == 0, 1.0, -1.0).astype(F32)
    fwd = jnp.concatenate([cos, jnp.where(k == 0, alt_s, -sin)], axis=0)
    inv_re = jnp.where(s == 0, 1.0, 2.0 * cos) / (2 * n_tok)
    alt_t = jnp.where(k % 2 == 0, 1.0, -1.0).astype(F32)
    inv_im = jnp.where(s == 0, alt_t, -2.0 * sin) / (2 * n_tok)
    inv = jnp.concatenate([inv_re, inv_im], axis=1)
    return fwd, inv


def _hyena_filter_kernel(feat_ref, w1_ref, b1_ref, fr_ref, w2_ref, b2_ref, w3_ref, b3_ref, win_ref,
                         fh_ref, fl_ref, g_ref, *, n_tok):
    fr = fr_ref[...]
    h = jnp.sin(fr * (_dot3(feat_ref[...], w1_ref[...]) + b1_ref[...]))
    h = jnp.sin(fr * (_dot3(h, w2_ref[...]) + b2_ref[...]))
    f = _dot3(h, w3_ref[...]) + b3_ref[...]
    win = win_ref[...]
    h_fwd = f[:, :D_GRP] * win
    h_bwd = jnp.where(_iota((n_tok, D_GRP), 0) == 0, 0.0, f[:, D_GRP:] * win)
    fh, fl = fh_ref[...], fl_ref[...]

    def spectrum(x):
        xh, xl = _split2(x)
        return (jnp.dot(fh, xh, preferred_element_type=F32) + jnp.dot(fl, xh, preferred_element_type=F32)
                + jnp.dot(fh, xl, preferred_element_type=F32))

    conj_sign = jnp.where(_iota((2 * n_tok, D_GRP), 0) <= n_tok, 1.0, -1.0)
    g_ref[0] = spectrum(h_fwd) + conj_sign * spectrum(h_bwd)


def hyena_filter_spectra(n_tok, w1, b1, freq, w2, b2, w3, b3, fwd_hi, fwd_lo):
    t = jnp.linspace(0.0, 1.0, n_tok, dtype=F32)[:, None]
    w = (2.0 * math.pi / n_tok) * jnp.arange(n_tok, dtype=F32)[:, None]
    bands = jnp.linspace(1e-4, HY_BANDS - 1, HY_BANDS, dtype=F32)[None, :]
    feats = jnp.concatenate([t, jnp.cos(bands * w), -jnp.sin(bands * w)], axis=-1)
    feats = jnp.pad(feats, ((0, 0), (0, LANES - HY_POS_DIM)))
    rates = jnp.abs(jnp.linspace(HY_FAST, HY_SLOW, D_GRP, dtype=F32))
    window = jnp.exp(-t * rates)
    padv = lambda a: jnp.pad(a, (0, LANES - HY_FF)).reshape(1, LANES)
    w1p = jnp.pad(w1, ((0, LANES - HY_POS_DIM), (0, LANES - HY_FF)))
    w2p = jnp.pad(w2, ((0, LANES - HY_FF), (0, LANES - HY_FF)))
    w3p = jnp.pad(w3, ((0, LANES - HY_FF), (0, 0)))
    full = lambda shape: pl.BlockSpec(shape, lambda o: (0,) * len(shape))
    return pl.pallas_call(
        functools.partial(_hyena_filter_kernel, n_tok=n_tok),
        out_shape=jax.ShapeDtypeStruct((HY_ORDER, 2 * n_tok, D_GRP), F32),
        grid=(HY_ORDER,),
        in_specs=[full((n_tok, LANES)), full((LANES, LANES)), full((1, LANES)), full((1, LANES)),
                  full((LANES, LANES)), full((1, LANES)),
                  pl.BlockSpec((LANES, 2 * D_GRP), lambda o: (0, o)),
                  pl.BlockSpec((1, 2 * D_GRP), lambda o: (0, o)),
                  full((n_tok, D_GRP)), full((2 * n_tok, n_tok)), full((2 * n_tok, n_tok))],
        out_specs=pl.BlockSpec((1, 2 * n_tok, D_GRP), lambda o: (o, 0, 0)),
        compiler_params=_cparams(("arbitrary",)),
        name="hyena_filter_spectra",
    )(feats, w1p, padv(b1), padv(freq), w2p, padv(b2), w3p, b3.reshape(1, -1), window, fwd_hi, fwd_lo)


def _hyena_conv_kernel(zy_ref, zx1_ref, zx2_ref, cwy_ref, cw1_ref, cw2_ref, cby_ref, cb1_ref, cb2_ref,
                       skip_ref, g_ref, fwd_ref, inv_ref, out_ref, *, n_tok):
    width = zy_ref.shape[1]
    row = _iota((n_tok, width), 0)
    first, last = row == 0, row == n_tok - 1

    def dwconv(x_ref, w_ref, b_ref):
        x = x_ref[...]
        w = w_ref[...]
        prev = jnp.where(first, 0.0, pltpu.roll(x, 1, 0))
        nxt = jnp.where(last, 0.0, pltpu.roll(x, n_tok - 1, 0))
        return prev * w[0:1] + x * w[1:2] + nxt * w[2:3] + b_ref[...]

    y = dwconv(zy_ref, cwy_ref, cby_ref)
    gates = (dwconv(zx1_ref, cw1_ref, cb1_ref), dwconv(zx2_ref, cw2_ref, cb2_ref))
    fwd = fwd_ref[...]
    for o in range(HY_ORDER):
        u = jnp.dot(fwd, y.astype(BF16), preferred_element_type=F32)
        a, b = u[:n_tok], u[n_tok:]
        c, d = g_ref[o, :n_tok, :], g_ref[o, n_tok:, :]
        bd = b * d
        y_re = a * c - jnp.where(first, 0.0, bd)
        y_im = jnp.where(first, bd, a * d + b * c)
        conv = (jnp.dot(inv_ref[:, :n_tok], y_re.astype(BF16), preferred_element_type=F32)
                + jnp.dot(inv_ref[:, n_tok:], y_im.astype(BF16), preferred_element_type=F32))
        y = gates[o] * (conv + skip_ref[o:o + 1, :] * y)
    out_ref[...] = y.astype(BF16)


def hyena_mixer(z, conv_w, conv_b, skip, spectra, fwd_hi, inv_hi, n_seq, n_tok, row_blk0):
    width = 256
    nw = D_GRP // width
    zcol = lambda base: pl.BlockSpec((n_tok, width), lambda b, j: (row_blk0 + b, base * LANES // width + j))
    wcol = lambda part: pl.BlockSpec((3, width), lambda b, j: (0, part * nw + j))
    bcol = lambda part: pl.BlockSpec((1, width), lambda b, j: (0, part * nw + j))
    cb = conv_b.reshape(1, 3 * D_GRP)
    in_specs = [zcol(ZD_Y), zcol(ZD_X1), zcol(ZD_X2), wcol(0), wcol(1), wcol(2), bcol(0), bcol(1), bcol(2),
                pl.BlockSpec((HY_ORDER, width), lambda b, j: (0, j)),
                pl.BlockSpec((HY_ORDER, 2 * n_tok, width), lambda b, j: (0, 0, j)),
                pl.BlockSpec((2 * n_tok, n_tok), lambda b, j: (0, 0)),
                pl.BlockSpec((n_tok, 2 * n_tok), lambda b, j: (0, 0))]
    args = [z, z, z, conv_w, conv_w, conv_w, cb, cb, cb, skip, spectra, fwd_hi, inv_hi]
    return pl.pallas_call(
        functools.partial(_hyena_conv_kernel, n_tok=n_tok),
        out_shape=jax.ShapeDtypeStruct((n_seq * n_tok, D_GRP), BF16),
        grid=(n_seq, nw),
        in_specs=in_specs,
        out_specs=pl.BlockSpec((n_tok, width), lambda b, j: (b, j)),
        compiler_params=_cparams(("arbitrary", "arbitrary")),
        name="hyena_mixer",
    )(*args)


def _outproj_kernel(*refs, n_ctx_blk):
    ctx_parts, dec_parts = refs[0:N_MIXERS], refs[N_MIXERS:2 * N_MIXERS]
    w_ref, x_ref, g1_ref, ng_ref, sc_ref, sh_ref, x1_ref, h2_ref = refs[2 * N_MIXERS:]
    is_ctx = pl.program_id(0) < n_ctx_blk
    acc = None
    for g in range(N_MIXERS):
        part = jnp.where(is_ctx, ctx_parts[g][...], dec_parts[g][...])
        term = jnp.dot(part, w_ref[g * D_GRP:(g + 1) * D_GRP, :], preferred_element_type=F32)
        acc = term if acc is None else acc + term
    x1 = x_ref[...] + g1_ref[0] * acc
    x1_ref[...] = x1
    h2_ref[...] = _ada_norm(x1, ng_ref[...], sc_ref[0], sh_ref[0]).astype(BF16)


def out_projection(ctx_parts, dec_parts, w_out_bf16, x, mod, layer, norm_g, n_ctx_tok, dec_seq):
    m = x.shape[0]
    tm = 512
    n_ctx_blk = n_ctx_tok // tm
    modspec = lambda part: pl.BlockSpec((1, 1, D_MODEL), _mod_row_index(layer, part, tm, n_ctx_tok, dec_seq))
    ctx_part = pl.BlockSpec((tm, D_GRP), lambda i: (jnp.minimum(i, n_ctx_blk - 1), 0))
    dec_part = pl.BlockSpec((tm, D_GRP), lambda i: (jnp.maximum(i - n_ctx_blk, 0), 0))
    row = pl.BlockSpec((tm, D_MODEL), lambda i: (i, 0))
    return pl.pallas_call(
        functools.partial(_outproj_kernel, n_ctx_blk=n_ctx_blk),
        out_shape=[jax.ShapeDtypeStruct((m, D_MODEL), F32), jax.ShapeDtypeStruct((m, D_MODEL), BF16)],
        grid=(m // tm,),
        in_specs=[ctx_part] * N_MIXERS + [dec_part] * N_MIXERS
        + [pl.BlockSpec((D_MODEL, D_MODEL), lambda i: (0, 0)), row,
           modspec(2), pl.BlockSpec((1, D_MODEL), lambda i: (0, 0)), modspec(4), modspec(3)],
        out_specs=[row, row],
        compiler_params=_cparams(("arbitrary",)),
        name="out_projection",
    )(*ctx_parts, *dec_parts, w_out_bf16, x, mod, norm_g.reshape(1, D_MODEL), mod, mod)


def _ffn_kernel(h_ref, wa_ref, wb_ref, cwa_ref, cwb_ref, cba_ref, cbb_ref, wd_ref, y_ref,
                ua0_ref, ub0_ref, ua1_ref, ub1_ref, *, tile_rows, n_ctx_tok, ctx_seq, dec_seq):
    i = pl.program_id(0)
    j = pl.program_id(1)
    tf = wd_ref.shape[1]
    body = slice(SUBLANES, SUBLANES + tile_rows)

    @pl.when((i == 0) & (j == 0))
    def _():
        for ref in (ua0_ref, ub0_ref, ua1_ref, ub1_ref):
            ref[...] = jnp.zeros_like(ref)

    half = tile_rows // 2

    def park(dst, w_ref):
        dst[body, :] = jnp.dot(h_ref[...], w_ref[0].astype(BF16), preferred_element_type=F32)

    def gated(ready_a, ready_b, r0, seq):
        lo = SUBLANES + r0
        edges = seq != tile_rows
        if edges:
            pos = (r0 + _iota((half, tf), 0)) & (seq - 1)
            first, last = pos == 0, pos == seq - 1

        def dwconv(u_ref, cw_ref, cb_ref):
            cw = cw_ref[0].astype(BF16)
            prev = u_ref[lo - 1:lo - 1 + half, :]
            nxt = u_ref[lo + 1:lo + 1 + half, :]
            if edges:
                prev = jnp.where(first, 0.0, prev)
                nxt = jnp.where(last, 0.0, nxt)
            prev, cur, nxt = prev.astype(BF16), u_ref[lo:lo + half, :].astype(BF16), nxt.astype(BF16)
            return prev * cw[0:1] + cur * cw[1:2] + nxt * cw[2:3] + cb_ref[0].astype(BF16)

        ua = dwconv(ready_a, cwa_ref, cba_ref)
        ub = dwconv(ready_b, cwb_ref, cbb_ref)
        return ua * _sigmoid(ua) * ub

    def step(park_a, park_b, ready_a, ready_b, first_finish, seq):
        wd = wd_ref[0].astype(BF16)
        for r0, dst, w_ref in ((0, park_a, wa_ref), (half, park_b, wb_ref)):
            park(dst, w_ref)
            contrib = jnp.dot(gated(ready_a, ready_b, r0, seq), wd, preferred_element_type=F32)
            if first_finish:
                y_ref[r0:r0 + half, :] = contrib
            else:
                y_ref[r0:r0 + half, :] += contrib

    @pl.when(j == 0)
    def _():
        park(ua0_ref, wa_ref)
        park(ub0_ref, wb_ref)

    is_ctx = i * tile_rows < n_ctx_tok
    for in_group, seq in ((is_ctx, ctx_seq), (jnp.logical_not(is_ctx), dec_seq)):
        @pl.when(in_group & (j == 1))
        def _():
            step(ua1_ref, ub1_ref, ua0_ref, ub0_ref, True, seq)

        @pl.when(in_group & (j > 1) & (j % 2 == 0))
        def _():
            step(ua0_ref, ub0_ref, ua1_ref, ub1_ref, False, seq)

        @pl.when(in_group & (j > 1) & (j % 2 == 1))
        def _():
            step(ua1_ref, ub1_ref, ua0_ref, ub0_ref, False, seq)


def conv_ffn(h2, layer, w_up, conv_w, conv_b, w_down, n_ctx_tok, ctx_seq, dec_seq):
    m = h2.shape[0]
    tm, tf = 1024, FFN_TF
    nf = D_FF // tf
    for seq in (ctx_seq, dec_seq):
        assert tm % seq == 0 and seq & (seq - 1) == 0, seq
    assert n_ctx_tok % tm == 0 and m % tm == 0
    cb = conv_b.reshape(DEPTH, 1, 2 * D_FF)
    up = lambda j: jnp.minimum(j, nf - 1)
    dn = lambda j: jnp.maximum(j - 1, 0)
    park = pltpu.VMEM((tm + 2 * SUBLANES, tf), F32)
    return pl.pallas_call(
        functools.partial(_ffn_kernel, tile_rows=tm, n_ctx_tok=n_ctx_tok, ctx_seq=ctx_seq, dec_seq=dec_seq),
        out_shape=jax.ShapeDtypeStruct((m, D_MODEL), F32),
        grid=(m // tm, nf + 1),
        in_specs=[pl.BlockSpec((tm, D_MODEL), lambda i, j: (i, 0)),
                  pl.BlockSpec((1, D_MODEL, tf), lambda i, j: (layer, 0, up(j))),
                  pl.BlockSpec((1, D_MODEL, tf), lambda i, j: (layer, 0, nf + up(j))),
                  pl.BlockSpec((1, 3, tf), lambda i, j: (layer, 0, dn(j))),
                  pl.BlockSpec((1, 3, tf), lambda i, j: (layer, 0, nf + dn(j))),
                  pl.BlockSpec((1, 1, tf), lambda i, j: (layer, 0, dn(j))),
                  pl.BlockSpec((1, 1, tf), lambda i, j: (layer, 0, nf + dn(j))),
                  pl.BlockSpec((1, tf, D_MODEL), lambda i, j: (layer, dn(j), 0))],
        out_specs=pl.BlockSpec((tm, D_MODEL), lambda i, j: (i, 0)),
        scratch_shapes=[park, park, park, park],
        compiler_params=_cparams(("arbitrary", "arbitrary")),
        name="conv_ffn",
    )(h2, w_up, w_up, conv_w, conv_w, cb, cb, w_down)


def _residual_kernel(x_ref, y_ref, g_ref, o_ref):
    o_ref[...] = x_ref[...] + g_ref[0] * y_ref[...]


def final_residual(x1, y, mod, layer, row0, n_rows, n_ctx_tok, dec_seq):
    tm = 512
    blk0 = row0 // tm

    def mod_index(i):
        start = (blk0 + i) * tm
        row = jnp.where(start < n_ctx_tok, 0, 1 + (start - n_ctx_tok) // dec_seq)
        return (layer * 48 + row * 6 + 5, 0, 0)

    return pl.pallas_call(
        _residual_kernel,
        out_shape=jax.ShapeDtypeStruct((n_rows, D_MODEL), F32),
        grid=(n_rows // tm,),
        in_specs=[pl.BlockSpec((tm, D_MODEL), lambda i: (blk0 + i, 0)),
                  pl.BlockSpec((tm, D_MODEL), lambda i: (blk0 + i, 0)),
                  pl.BlockSpec((1, 1, D_MODEL), mod_index)],
        out_specs=pl.BlockSpec((tm, D_MODEL), lambda i: (i, 0)),
        compiler_params=_cparams(("arbitrary",)),
        name="final_residual",
    )(x1, y, mod)


def kernel(x_prompt, x_sample, cache_diff_k, cache_diff_v, cache_swa_k, cache_swa_v, state_mlstm_C, state_mlstm_n, state_mlstm_m, c, c_ctx, w_mod, b_mod, norm1_g, norm2_g, w_in, mlstm_ig_b, mlstm_fg_b, mlstm_norm_g, diff_qn_g, diff_kn_g, diff_lam, diff_out_g, swa_qn_g, swa_kn_g, swa_sink, hy_conv_w, hy_conv_b, hy_w1, hy_b1, hy_freq, hy_w2, hy_b2, hy_w3, hy_b3, hy_bias, w_out, ffn_w_up, ffn_conv_w, ffn_conv_b, ffn_w_down):
    n_ctx, ctx_seq = x_prompt.shape[:2]
    n_dec, dec_seq = x_sample.shape[:2]
    n_ctx_tok = n_ctx * ctx_seq
    n_dec_tok = n_dec * dec_seq
    dec_blk0 = n_ctx_tok // dec_seq

    x = (x_prompt.reshape(n_ctx_tok, D_MODEL), x_sample.reshape(n_dec_tok, D_MODEL))
    w_pack, wg_hi, wg_lo = pack_in_weights(w_in)
    cond8 = jnp.concatenate([c_ctx[None, :], c, jnp.zeros((8 - 1 - n_dec, D_MODEL), F32)], axis=0)
    mod = mod_vectors(cond8, w_mod, b_mod)

    dft = {}
    for t in (ctx_seq, dec_seq):
        fwd, inv = _dft_matrices(t)
        dft[t] = dict(fwd=fwd.astype(BF16), inv=inv.astype(BF16), split=_split2(fwd))

    delta = None
    caches = []
    for l in range(DEPTH):
        x, z, gates = in_projection(x, delta, mod, l, norm1_g[l], w_pack, wg_hi, wg_lo, n_ctx_tok, dec_seq)

        gate_bias = jnp.pad(jnp.concatenate([mlstm_ig_b[l].reshape(-1), mlstm_fg_b[l].reshape(-1)]),
                            (0, LANES - 4 * H_A)).reshape(1, LANES)
        lam_init = 0.8 - 0.6 * math.exp(-0.3 * l)

        a_ctx, c_new, n_new, m_new = mlstm_mixer(z, gates, gate_bias, mlstm_norm_g[l], None,
                                                 n_ctx, ctx_seq, 0, ctx_seq, True)
        b_ctx, diff_k = diff_mixer(z, diff_qn_g[l], diff_kn_g[l], diff_lam[l], diff_out_g[l], lam_init,
                                   None, n_ctx, ctx_seq, 0)
        c_ctx_out, swa_k = swa_mixer(z, swa_qn_g[l], swa_kn_g[l], swa_sink[l], None, n_ctx, ctx_seq, 0)
        spec_ctx = hyena_filter_spectra(ctx_seq, hy_w1[l], hy_b1[l], hy_freq[l], hy_w2[l], hy_b2[l],
                                        hy_w3[l], hy_b3[l], *dft[ctx_seq]["split"])
        d_ctx = hyena_mixer(z, hy_conv_w[l], hy_conv_b[l], hy_bias[l], spec_ctx, dft[ctx_seq]["fwd"],
                            dft[ctx_seq]["inv"], n_ctx, ctx_seq, 0)
        a_dec, = mlstm_mixer(z, gates, gate_bias, mlstm_norm_g[l],
                             (state_mlstm_C[:, l], state_mlstm_n[:, l], state_mlstm_m[:, l]),
                             n_dec, dec_seq, dec_blk0, 512, False)
        b_dec, = diff_mixer(z, diff_qn_g[l], diff_kn_g[l], diff_lam[l], diff_out_g[l], lam_init,
                            (cache_diff_k[:, l], cache_diff_v[:, l]), n_dec, dec_seq, dec_blk0)
        c_dec, = swa_mixer(z, swa_qn_g[l], swa_kn_g[l], swa_sink[l],
                           (cache_swa_k[:, l], cache_swa_v[:, l]), n_dec, dec_seq, dec_blk0)
        spec_dec = hyena_filter_spectra(dec_seq, hy_w1[l], hy_b1[l], hy_freq[l], hy_w2[l], hy_b2[l],
                                        hy_w3[l], hy_b3[l], *dft[dec_seq]["split"])
        d_dec = hyena_mixer(z, hy_conv_w[l], hy_conv_b[l], hy_bias[l], spec_dec, dft[dec_seq]["fwd"],
                            dft[dec_seq]["inv"], n_dec, dec_seq, dec_blk0)

        x, h2 = out_projection((a_ctx, b_ctx, c_ctx_out, d_ctx), (a_dec, b_dec, c_dec, d_dec),
                               w_out[l].astype(BF16), x, mod, l, norm2_g[l], n_ctx_tok, dec_seq)
        delta = conv_ffn(h2, l, ffn_w_up, ffn_conv_w, ffn_conv_b, ffn_w_down, n_ctx_tok, ctx_seq, dec_seq)

        zc = z[:n_ctx_tok]
        caches.append((
            diff_k.reshape(n_ctx, ctx_seq, H_B, 2, DK_B),
            zc[:, ZB_V * LANES:(ZB_V + 4) * LANES].reshape(n_ctx, ctx_seq, H_B, DV_B),
            swa_k.reshape(n_ctx, ctx_seq, KV_C, DH_C),
            zc[:, ZC_V * LANES:(ZC_V + 2) * LANES].reshape(n_ctx, ctx_seq, KV_C, DH_C),
            c_new,
            n_new.reshape(n_ctx, 2, H_A, DH_A),
            m_new.reshape(n_ctx, 2, H_A),
        ))

    y_prompt = final_residual(x, delta, mod, DEPTH - 1, 0, n_ctx_tok, n_ctx_tok, dec_seq)
    y_sample = final_residual(x, delta, mod, DEPTH - 1, n_ctx_tok, n_dec_tok, n_ctx_tok, dec_seq)
    outs = [jnp.stack([s[i] for s in caches], axis=1) for i in range(7)]
    return (y_prompt.reshape(n_ctx, ctx_seq, D_MODEL), y_sample.reshape(n_dec, dec_seq, D_MODEL), *outs)
```

```python
import functools
import math

import jax
import jax.numpy as jnp
from jax import lax
from jax.experimental import pallas as pl
from jax.experimental.pallas import tpu as pltpu

D_MODEL = 2048
DEPTH = 2
GRID_W = 64
D_GRP = 512
N_MIXERS = D_MODEL // D_GRP
H_A = 4
DH_A = 128
H_B = 4
DV_B = 128
DK_B = 64
H_C = 4
KV_C = 2
G_C = 2
DH_C = 128
WINDOW = 128
HY_ORDER = 2
HY_POS_DIM = 33
HY_BANDS = (HY_POS_DIM - 1) // 2
HY_FF = 64
HY_FAST = math.log(1e-2) / 0.3
HY_SLOW = math.log(1e-2) / 1.5
D_FF = 5632
ROPE_BASE = 10000.0
EPS = 1e-6
N_A = 4 * D_GRP + 4 * H_A

LANES = 128
SUBLANES = 8
V7X_VMEM_LIMIT_BYTES = 56 * 1024 * 1024
Z_COLS = 12 * D_GRP
INPROJ_TN = 3 * D_GRP
FFN_TF = 256
ZA_Q, ZA_K, ZA_V, ZA_O = 0, 4, 8, 12
ZB_Q, ZB_K, ZB_V = 16, 20, 24
ZC_Q, ZC_K, ZC_V = 28, 32, 34
ZD_Y, ZD_X1, ZD_X2 = 36, 40, 44

BF16 = jnp.bfloat16
F32 = jnp.float32


def _cparams(sem):
    return pltpu.CompilerParams(dimension_semantics=sem, vmem_limit_bytes=V7X_VMEM_LIMIT_BYTES)


def _dot(a, b):
    return jnp.dot(a.astype(BF16), b.astype(BF16), preferred_element_type=F32)


def _dot_nt(a, b):
    return lax.dot_general(a.astype(BF16), b.astype(BF16), (((1,), (1,)), ((), ())),
                           preferred_element_type=F32)


def _bf16_head(a):
    bits = lax.bitcast_convert_type(a, jnp.int32) & jnp.int32(-65536)
    return lax.bitcast_convert_type(bits, F32)


def _split2(a):
    head = _bf16_head(a)
    return head.astype(BF16), (a - head).astype(BF16)


def _split3(a):
    head = _bf16_head(a)
    rest = a - head
    mid = _bf16_head(rest)
    return head.astype(BF16), mid.astype(BF16), (rest - mid).astype(BF16)


def _dot3(a, b):
    ah, al = _split2(a)
    bh, bl = _split2(b)
    return (jnp.dot(ah, bh, preferred_element_type=F32) + jnp.dot(al, bh, preferred_element_type=F32)
            + jnp.dot(ah, bl, preferred_element_type=F32))


def _iota(shape, dim):
    return lax.broadcasted_iota(jnp.int32, shape, dim)


def _sigmoid(x):
    return 1.0 / (1.0 + jnp.exp(-x))


def _log_sigmoid(x):
    return jnp.minimum(x, 0.0) - jnp.log(1.0 + jnp.exp(-jnp.abs(x)))


def _mod_kernel(c_ref, w_ref, b_ref, o_ref):
    k = pl.program_id(1)
    c = c_ref[...]
    part = _dot(c * _sigmoid(c), w_ref[0])

    @pl.when(k == 0)
    def _():
        o_ref[0] = part + b_ref[0]

    @pl.when(k > 0)
    def _():
        o_ref[0] += part


def mod_vectors(cond8, w_mod, b_mod):
    tk = 256
    n = w_mod.shape[2]
    out = pl.pallas_call(
        _mod_kernel,
        out_shape=jax.ShapeDtypeStruct((DEPTH, 8, n), F32),
        grid=(DEPTH, D_MODEL // tk),
        in_specs=[pl.BlockSpec((8, tk), lambda l, k: (0, k)),
                  pl.BlockSpec((1, tk, n), lambda l, k: (l, k, 0)),
                  pl.BlockSpec((1, 1, n), lambda l, k: (l, 0, 0))],
        out_specs=pl.BlockSpec((1, 8, n), lambda l, k: (l, 0, 0)),
        compiler_params=_cparams(("arbitrary", "arbitrary")),
        name="mod_vectors",
    )(cond8, w_mod, b_mod.reshape(DEPTH, 1, n))
    return out.reshape(DEPTH * 8 * 6, 1, D_MODEL)


def _mod_row_index(layer, part, tile_rows, n_ctx_tok, dec_seq):
    def index(i, *_):
        start = i * tile_rows
        row = jnp.where(start < n_ctx_tok, 0, 1 + (start - n_ctx_tok) // dec_seq)
        return (layer * 48 + row * 6 + part, 0, 0)
    return index


def _ada_norm(x, ng, sc, sh):
    y = x * lax.rsqrt(jnp.mean(x * x, axis=-1, keepdims=True) + EPS)
    return y * ng * (1.0 + sc) + sh


def _pack_in_kernel(a_ref, b_ref, w_ref, gh_ref, gl_ref):
    c = pl.program_id(1)
    n_aligned = 4 * D_GRP // a_ref.shape[1]
    gap = 4 * H_A

    @pl.when(c < n_aligned)
    def _():
        w_ref[0, 0] = a_ref[0].T.astype(BF16)

    @pl.when(c >= n_aligned)
    def _():
        w_ref[0, 0] = jnp.concatenate([a_ref[0, gap:, :], b_ref[0]], axis=0).T.astype(BF16)

    @pl.when(c == n_aligned)
    def _():
        rows = a_ref[0, 0:LANES, :]
        hi, lo = _split2(jnp.where(_iota(rows.shape, 0) < gap, rows, 0.0).T)
        gh_ref[0] = hi
        gl_ref[0] = lo


def pack_in_weights(w_in):
    tc = 512
    gap = 4 * H_A
    n_blk = Z_COLS // tc
    per_tile = INPROJ_TN // tc
    w_t = jnp.swapaxes(w_in, 1, 2)
    gate = pl.BlockSpec((1, D_MODEL, LANES), lambda l, c: (l, 0, 0))
    return pl.pallas_call(
        _pack_in_kernel,
        out_shape=[jax.ShapeDtypeStruct((DEPTH, Z_COLS // INPROJ_TN, D_MODEL, INPROJ_TN), BF16),
                   jax.ShapeDtypeStruct((DEPTH, D_MODEL, LANES), BF16),
                   jax.ShapeDtypeStruct((DEPTH, D_MODEL, LANES), BF16)],
        grid=(DEPTH, n_blk),
        in_specs=[pl.BlockSpec((1, tc, D_MODEL), lambda l, c: (l, c, 0)),
                  pl.BlockSpec((1, gap, D_MODEL), lambda l, c: (l, (c + 1) * (tc // gap), 0))],
        out_specs=[pl.BlockSpec((1, 1, D_MODEL, tc), lambda l, c: (l, c // per_tile, 0, c % per_tile)),
                   gate, gate],
        compiler_params=_cparams(("arbitrary", "arbitrary")),
        name="pack_in_weights",
    )(w_t, w_t)


def _inproj_kernel(*refs, has_delta, n_ctx_blk):
    if has_delta:
        (x_ref, y_ref, g2_ref, ng_ref, sc_ref, sh_ref, w_ref, wgh_ref, wgl_ref,
         xn_ref, z_ref, gate_ref, hb_ref) = refs
    else:
        (xc_ref, xd_ref, ng_ref, sc_ref, sh_ref, w_ref, wgh_ref, wgl_ref,
         xn_ref, z_ref, gate_ref, hb_ref) = refs

    i = pl.program_id(0)

    @pl.when(pl.program_id(1) == 0)
    def _():
        if has_delta:
            x = x_ref[...] + g2_ref[0] * y_ref[...]
        else:
            x = jnp.where(i < n_ctx_blk, xc_ref[...], xd_ref[...])
        xn_ref[...] = x
        h = _ada_norm(x, ng_ref[...], sc_ref[0], sh_ref[0])
        hb_ref[...] = h.astype(BF16)
        hh, hl = _split2(h)
        wgh = wgh_ref[0]
        gate_ref[...] = (jnp.dot(hh, wgh, preferred_element_type=F32)
                         + jnp.dot(hl, wgh, preferred_element_type=F32)
                         + jnp.dot(hh, wgl_ref[0], preferred_element_type=F32))

    z_ref[...] = jnp.dot(hb_ref[...], w_ref[0, 0], preferred_element_type=F32)


def in_projection(x, delta, mod, layer, norm_g, w_pack, wg_hi, wg_lo, n_ctx_tok, dec_seq):
    tm, tn = 512, INPROJ_TN
    has_delta = delta is not None
    n_ctx_blk = n_ctx_tok // tm
    row = pl.BlockSpec((tm, D_MODEL), lambda i, j: (i, 0))
    modspec = lambda part: pl.BlockSpec((1, 1, D_MODEL), _mod_row_index(layer, part, tm, n_ctx_tok, dec_seq))
    vec = pl.BlockSpec((1, D_MODEL), lambda i, j: (0, 0))
    if has_delta:
        m = x.shape[0]
        g2spec = pl.BlockSpec((1, 1, D_MODEL), _mod_row_index(layer - 1, 5, tm, n_ctx_tok, dec_seq))
        in_specs = [row, row, g2spec]
        args = [x, delta, mod]
    else:
        x_ctx, x_dec = x
        m = x_ctx.shape[0] + x_dec.shape[0]
        in_specs = [pl.BlockSpec((tm, D_MODEL), lambda i, j: (jnp.minimum(i, n_ctx_blk - 1), 0)),
                    pl.BlockSpec((tm, D_MODEL), lambda i, j: (jnp.maximum(i - n_ctx_blk, 0), 0))]
        args = [x_ctx, x_dec]
    gate_w = pl.BlockSpec((1, D_MODEL, LANES), lambda i, j: (layer, 0, 0))
    in_specs += [vec, modspec(1), modspec(0),
                 pl.BlockSpec((1, 1, D_MODEL, tn), lambda i, j: (layer, j, 0, 0)), gate_w, gate_w]
    args += [norm_g.reshape(1, D_MODEL), mod, mod, w_pack, wg_hi, wg_lo]
    return pl.pallas_call(
        functools.partial(_inproj_kernel, has_delta=has_delta, n_ctx_blk=n_ctx_blk),
        out_shape=[jax.ShapeDtypeStruct((m, D_MODEL), F32), jax.ShapeDtypeStruct((m, Z_COLS), F32),
                   jax.ShapeDtypeStruct((m, LANES), F32)],
        grid=(m // tm, Z_COLS // tn),
        in_specs=in_specs,
        out_specs=[row, pl.BlockSpec((tm, tn), lambda i, j: (i, j)), pl.BlockSpec((tm, LANES), lambda i, j: (i, 0))],
        scratch_shapes=[pltpu.VMEM((tm, D_MODEL), BF16)],
        compiler_params=_cparams(("arbitrary", "arbitrary")),
        name="in_projection",
    )(*args)


def _mlstm_kernel(*refs, n_tok, chunk, has_state, want_state):
    refs = list(refs)
    q_ref, k_ref, v_ref, o_ref, gate_ref, gb_ref, ng_ref = refs[:7]
    pos = 7
    if has_state:
        c0_ref, n0_ref, m0_ref = refs[pos:pos + 3]
        pos += 3
    out_ref = refs[pos]
    pos += 1
    if want_state:
        cn_ref, nn_ref, mn_ref = refs[pos:pos + 3]
        pos += 3
    hacc_ref = refs[pos]

    L = chunk
    nc = n_tok // L
    scale = DH_A ** -0.5
    r_idx = _iota((L, L), 0)
    c_idx = _iota((L, L), 1)
    ones_b = jnp.ones((L, DH_A), BF16)

    g = gate_ref[...] + gb_ref[...]
    vals = jnp.where(_iota((n_tok, LANES), 1) < 2 * H_A, g, _log_sigmoid(g))

    for d in range(2):
        causal = (c_idx <= r_idx) if d == 0 else (c_idx >= r_idx)
        tri = jnp.where(causal, 1.0, 0.0).astype(BF16)
        if has_state:
            c_mem = [c0_ref[0, d, h] for h in range(H_A)]
            n_mem = [n0_ref[0, d, h] for h in range(H_A)]
            m_prev = [m0_ref[0, d, h] for h in range(H_A)]
            pad = jnp.zeros((SUBLANES - 1, DH_A), F32)
            n_mat = [_dot(jnp.concatenate([n, pad], axis=0).T, jnp.ones((SUBLANES, DH_A), BF16)) for n in n_mem]
        else:
            c_mem = [None] * H_A
            n_mem = [None] * H_A
            n_mat = [None] * H_A
            m_prev = [jnp.zeros((1, 1), F32)] * H_A
        order = range(nc) if d == 0 else range(nc - 1, -1, -1)
        for ci, c in enumerate(order):
            rows = slice(c * L, (c + 1) * L)
            v_c = vals[rows]
            p0, p1, p2 = _split3(v_c)
            cum = (jnp.dot(tri, p0, preferred_element_type=F32) + jnp.dot(tri, p1, preferred_element_type=F32)
                   + jnp.dot(tri, p2, preferred_element_type=F32))
            v_t = v_c.T
            cum_t = cum.T
            for h in range(H_A):
                col_i = d * H_A + h
                col_f = 2 * H_A + d * H_A + h
                lanes = slice(h * DH_A, (h + 1) * DH_A)
                c_col = cum[:, col_f:col_f + 1]
                ig_col = v_c[:, col_i:col_i + 1]
                c_row = cum_t[col_f:col_f + 1, :]
                ig_row = v_t[col_i:col_i + 1, :]

                dmat = jnp.where(causal, c_col - c_row + ig_row, -jnp.inf)
                inter = c_col + m_prev[h]
                m_col = jnp.maximum(inter, jnp.max(dmat, axis=1, keepdims=True))
                dexp = jnp.exp(dmat - m_col)
                qs = q_ref[rows, lanes] * scale
                qb = qs.astype(BF16)
                kf = k_ref[rows, lanes]
                vb = v_ref[rows, lanes].astype(BF16)
                pb = (_dot_nt(qb, kf) * dexp).astype(BF16)
                num = jnp.dot(pb, vb, preferred_element_type=F32)
                den = jnp.dot(pb, ones_b, preferred_element_type=F32)
                carried = c_mem[h] is not None
                if carried:
                    w_inter = jnp.broadcast_to(jnp.exp(inter - m_col), (L, DH_A))
                    num = num + w_inter * _dot(qb, c_mem[h])
                    den = den + w_inter * _dot(qb, n_mat[h])
                hh = num / jnp.maximum(jnp.abs(den), jnp.exp(-m_col))
                if d == 0:
                    hacc_ref[rows, lanes] = hh
                else:
                    hacc_ref[rows, lanes] += hh

                if want_state or ci < nc - 1:
                    total = c_col[L - 1:L, :] if d == 0 else c_col[0:1, :]
                    w_row = total - c_row + ig_row
                    m_new = jnp.maximum(total + m_prev[h], jnp.max(w_row, axis=1, keepdims=True))
                    decay = jnp.exp(total + m_prev[h] - m_new)
                    wk = jnp.exp(total - c_col + ig_col - m_new)
                    kw = kf * wk
                    kw_t = kw.T.astype(BF16)
                    c_upd = jnp.dot(kw_t, vb, preferred_element_type=F32)
                    c_mem[h] = decay * c_mem[h] + c_upd if carried else c_upd
                    if ci < nc - 1:
                        n_upd = jnp.dot(kw_t, ones_b, preferred_element_type=F32)
                        n_mat[h] = decay * n_mat[h] + n_upd if carried else n_upd
                    if want_state:
                        n_row = jnp.sum(kw, axis=0, keepdims=True)
                        n_mem[h] = decay * n_mem[h] + n_row if carried else n_row
                    m_prev[h] = m_new
        if want_state:
            for h in range(H_A):
                cn_ref[0, d, h] = c_mem[h]
                nn_ref[0, d, h] = n_mem[h]
                mn_ref[0, d, h] = m_prev[h]

    for h in range(H_A):
        lanes = slice(h * DH_A, (h + 1) * DH_A)
        hs = hacc_ref[:, lanes]
        y = hs * lax.rsqrt(jnp.mean(hs * hs, axis=1, keepdims=True) + EPS) * ng_ref[:, lanes]
        out_ref[:, lanes] = (y * _sigmoid(o_ref[:, lanes])).astype(BF16)


def mlstm_mixer(z, gates, gate_bias, norm_g, state, n_seq, n_tok, row_blk0, chunk, want_state):
    has_state = state is not None
    col = lambda base: pl.BlockSpec((n_tok, D_GRP), lambda b: (row_blk0 + b, base * LANES // D_GRP))
    in_specs = [col(ZA_Q), col(ZA_K), col(ZA_V), col(ZA_O),
                pl.BlockSpec((n_tok, LANES), lambda b: (row_blk0 + b, 0)),
                pl.BlockSpec((1, LANES), lambda b: (0, 0)),
                pl.BlockSpec((1, D_GRP), lambda b: (0, 0))]
    args = [z, z, z, z, gates, gate_bias, norm_g.reshape(1, D_GRP)]
    st_specs = [pl.BlockSpec((1, 2, H_A, DH_A, DH_A), lambda b: (b, 0, 0, 0, 0)),
                pl.BlockSpec((1, 2, H_A, 1, DH_A), lambda b: (b, 0, 0, 0, 0)),
                pl.BlockSpec((1, 2, H_A, 1, 1), lambda b: (b, 0, 0, 0, 0))]
    if has_state:
        c0, n0, m0 = state
        in_specs += st_specs
        args += [c0, n0.reshape(n_seq, 2, H_A, 1, DH_A), m0.reshape(n_seq, 2, H_A, 1, 1)]
    out_shape = [jax.ShapeDtypeStruct((n_seq * n_tok, D_GRP), BF16)]
    out_specs = [pl.BlockSpec((n_tok, D_GRP), lambda b: (b, 0))]
    if want_state:
        out_shape += [jax.ShapeDtypeStruct((n_seq, 2, H_A, DH_A, DH_A), F32),
                      jax.ShapeDtypeStruct((n_seq, 2, H_A, 1, DH_A), F32),
                      jax.ShapeDtypeStruct((n_seq, 2, H_A, 1, 1), F32)]
        out_specs += st_specs
    outs = pl.pallas_call(
        functools.partial(_mlstm_kernel, n_tok=n_tok, chunk=chunk, has_state=has_state, want_state=want_state),
        out_shape=out_shape,
        grid=(n_seq,),
        in_specs=in_specs,
        out_specs=out_specs,
        scratch_shapes=[pltpu.VMEM((n_tok, D_GRP), F32)],
        compiler_params=_cparams(("arbitrary",)),
        name="mlstm_mixer",
    )(*args)
    return outs


def _rope_tables(n_tok, dh):
    rows = n_tok // GRID_W
    t_row = jnp.repeat(jnp.arange(rows, dtype=F32), GRID_W)
    t_col = jnp.tile(jnp.arange(GRID_W, dtype=F32), rows)
    n_freq = dh // 4
    inv = ROPE_BASE ** (-jnp.arange(n_freq, dtype=F32) / n_freq)
    ang = jnp.concatenate([t_row[:, None] * inv, t_col[:, None] * inv], axis=-1)
    cos, sin = jnp.cos(ang), jnp.sin(ang)
    reps = LANES // dh
    cos_t = jnp.tile(jnp.concatenate([cos, cos], axis=-1), (1, reps))
    sin_t = jnp.tile(jnp.concatenate([-sin, sin], axis=-1), (1, reps))
    return cos_t, sin_t


def _rope(x, cos_t, sin_t, dh):
    half = dh // 2
    if dh == LANES:
        partner = pltpu.roll(x, half, 1)
    else:
        first = (_iota(x.shape, 1) & half) == 0
        partner = jnp.where(first, pltpu.roll(x, LANES - half, 1), pltpu.roll(x, half, 1))
    return x * cos_t + partner * sin_t


def _rms(x, g):
    return x * lax.rsqrt(jnp.mean(x * x, axis=1, keepdims=True) + EPS) * g


def _with_ones(v):
    return jnp.concatenate([v.astype(BF16), jnp.ones((v.shape[0], LANES), BF16)], axis=1)


def _diff_kernel(*refs, n_tok, q_blk, lam_init, has_ctx):
    refs = list(refs)
    q_ref, k_ref, v_ref, qg_ref, kg_ref, lam_ref, og_ref = refs[:7]
    pos = 7
    if has_ctx:
        ck_ref, cv_ref, cos_ref, sin_ref = refs[pos:pos + 4]
        pos += 4
    out_ref = refs[pos]
    pos += 1
    if not has_ctx:
        kout_ref = refs[pos]

    grp0 = _iota((n_tok, LANES), 1) < DK_B

    def group_norm(x, g):
        x2 = x * x
        s0 = jnp.sum(jnp.where(grp0, x2, 0.0), axis=1, keepdims=True)
        s1 = jnp.sum(jnp.where(grp0, 0.0, x2), axis=1, keepdims=True)
        ms = jnp.where(grp0, s0, s1) * (1.0 / DK_B)
        return x * lax.rsqrt(ms + EPS) * g

    qn = group_norm(q_ref[...], qg_ref[...])
    kn = group_norm(k_ref[...], kg_ref[...])
    if has_ctx:
        cos_t, sin_t = cos_ref[...], sin_ref[...]
        qn = _rope(qn, cos_t, sin_t, DK_B)
        kn = _rope(kn, cos_t, sin_t, DK_B)
    else:
        kout_ref[...] = kn
    kb = kn.astype(BF16)
    vb = _with_ones(v_ref[...])
    if has_ctx:
        ckb = ck_ref[0].astype(BF16)
        cvb = _with_ones(cv_ref[0])

    lp = lam_ref[...]
    lam = (jnp.exp(jnp.sum(lp[0:1] * lp[1:2], axis=1, keepdims=True))
           - jnp.exp(jnp.sum(lp[2:3] * lp[3:4], axis=1, keepdims=True)) + lam_init)
    scale = DK_B ** -0.5
    assert math.frexp(scale)[0] == 0.5
    grp0_q = _iota((q_blk, LANES), 1) < DK_B

    for qi in range(n_tok // q_blk):
        rows = slice(qi * q_blk, (qi + 1) * q_blk)
        q_rows = qn[rows] * scale
        outs = []
        for comp in range(2):
            keep = grp0_q if comp == 0 else jnp.logical_not(grp0_q)
            qm = jnp.where(keep, q_rows, 0.0).astype(BF16)
            s = _dot_nt(qm, kb)
            mx = jnp.max(s, axis=1, keepdims=True)
            if has_ctx:
                sc = _dot_nt(qm, ckb)
                mx = jnp.maximum(mx, jnp.max(sc, axis=1, keepdims=True))
            r = _dot(jnp.exp(s - mx), vb)
            if has_ctx:
                r = r + _dot(jnp.exp(sc - mx), cvb)
            outs.append(r[:, :DV_B] / r[:, DV_B:])
        a = outs[0] - lam * outs[1]
        out_ref[rows, :] = (_rms(a, og_ref[...]) * (1.0 - lam_init)).astype(BF16)


def diff_mixer(z, qn_g, kn_g, lam, out_g, lam_init, ctx, n_seq, n_tok, row_blk0):
    has_ctx = ctx is not None
    col = lambda base: pl.BlockSpec((n_tok, LANES), lambda b, h: (row_blk0 + b, base + h))
    vec = pl.BlockSpec((1, LANES), lambda b, h: (0, 0))
    in_specs = [col(ZB_Q), col(ZB_K), col(ZB_V), vec, vec,
                pl.BlockSpec((4, LANES), lambda b, h: (0, 0)), vec]
    args = [z, z, z, jnp.tile(qn_g, 2).reshape(1, LANES), jnp.tile(kn_g, 2).reshape(1, LANES),
            jnp.pad(lam, ((0, 0), (0, LANES - DK_B))), out_g.reshape(1, LANES)]
    if has_ctx:
        ck, cv = ctx
        s_ctx = ck.shape[1]
        cos_t, sin_t = _rope_tables(n_tok, DK_B)
        tab = pl.BlockSpec((n_tok, LANES), lambda b, h: (0, 0))
        cspec = pl.BlockSpec((1, s_ctx, LANES), lambda b, h: (b, 0, h))
        in_specs += [cspec, cspec, tab, tab]
        args += [ck.reshape(n_seq, s_ctx, H_B * 2 * DK_B), cv.reshape(n_seq, s_ctx, H_B * DV_B), cos_t, sin_t]
    out_shape = [jax.ShapeDtypeStruct((n_seq * n_tok, D_GRP), BF16)]
    out_specs = [pl.BlockSpec((n_tok, LANES), lambda b, h: (b, h))]
    if not has_ctx:
        out_shape.append(jax.ShapeDtypeStruct((n_seq * n_tok, D_GRP), F32))
        out_specs.append(pl.BlockSpec((n_tok, LANES), lambda b, h: (b, h)))
    return pl.pallas_call(
        functools.partial(_diff_kernel, n_tok=n_tok, q_blk=min(n_tok, 256), lam_init=lam_init, has_ctx=has_ctx),
        out_shape=out_shape,
        grid=(n_seq, H_B),
        in_specs=in_specs,
        out_specs=out_specs,
        compiler_params=_cparams(("arbitrary", "arbitrary")),
        name="diff_mixer",
    )(*args)


def _swa_kernel(*refs, n_tok, q_blk, has_ctx):
    refs = list(refs)
    sink_ref, q_ref, k_ref, v_ref, qg_ref, kg_ref = refs[:6]
    pos = 6
    if has_ctx:
        ck_ref, cv_ref, cos_ref, sin_ref = refs[pos:pos + 4]
        pos += 4
    out_ref = refs[pos]
    pos += 1
    if not has_ctx:
        kout_ref = refs[pos]

    kv = pl.program_id(1)
    kn = _rms(k_ref[...], kg_ref[...])
    if has_ctx:
        cos_t, sin_t = cos_ref[...], sin_ref[...]
        kn = _rope(kn, cos_t, sin_t, DH_C)
        ckb = ck_ref[0].astype(BF16)
        cvb = _with_ones(cv_ref[0])
    else:
        kout_ref[...] = kn
    kb = kn.astype(BF16)
    vb = _with_ones(v_ref[...])
    scale = DH_C ** -0.5

    for g in range(G_C):
        sink = sink_ref[kv * G_C + g]
        qn = _rms(q_ref[:, g * DH_C:(g + 1) * DH_C], qg_ref[...])
        if has_ctx:
            qn = _rope(qn, cos_t, sin_t, DH_C)
        for qi in range(n_tok // q_blk):
            rows = slice(qi * q_blk, (qi + 1) * q_blk)
            qb = qn[rows].astype(BF16)
            s = _dot_nt(qb, kb) * scale
            if has_ctx:
                qpos = qi * q_blk + _iota((q_blk, n_tok), 0)
                kpos = _iota((q_blk, n_tok), 1)
                s = jnp.where(jnp.abs(kpos - qpos) <= WINDOW, s, -jnp.inf)
            mx = jnp.maximum(jnp.max(s, axis=1, keepdims=True), sink)
            if has_ctx:
                sc = _dot_nt(qb, ckb) * scale
                mx = jnp.maximum(mx, jnp.max(sc, axis=1, keepdims=True))
            r = _dot(jnp.exp(s - mx), vb)
            if has_ctx:
                r = r + _dot(jnp.exp(sc - mx), cvb)
            den = r[:, DH_C:] + jnp.exp(sink - mx)
            out_ref[rows, g * DH_C:(g + 1) * DH_C] = (r[:, :DH_C] / den).astype(BF16)


def swa_mixer(z, qn_g, kn_g, sink, ctx, n_seq, n_tok, row_blk0):
    has_ctx = ctx is not None
    vec = pl.BlockSpec((1, LANES), lambda b, kv: (0, 0))
    in_specs = [pl.BlockSpec(memory_space=pltpu.SMEM),
                pl.BlockSpec((n_tok, G_C * DH_C), lambda b, kv: (row_blk0 + b, ZC_Q // G_C + kv)),
                pl.BlockSpec((n_tok, LANES), lambda b, kv: (row_blk0 + b, ZC_K + kv)),
                pl.BlockSpec((n_tok, LANES), lambda b, kv: (row_blk0 + b, ZC_V + kv)),
                vec, vec]
    args = [sink.astype(F32), z, z, z, qn_g.reshape(1, LANES), kn_g.reshape(1, LANES)]
    if has_ctx:
        ck, cv = ctx
        s_ctx = ck.shape[1]
        cos_t, sin_t = _rope_tables(n_tok, DH_C)
        tab = pl.BlockSpec((n_tok, LANES), lambda b, kv: (0, 0))
        cspec = pl.BlockSpec((1, s_ctx, LANES), lambda b, kv: (b, 0, kv))
        in_specs += [cspec, cspec, tab, tab]
        args += [ck.reshape(n_seq, s_ctx, KV_C * DH_C), cv.reshape(n_seq, s_ctx, KV_C * DH_C), cos_t, sin_t]
    out_shape = [jax.ShapeDtypeStruct((n_seq * n_tok, D_GRP), BF16)]
    out_specs = [pl.BlockSpec((n_tok, G_C * DH_C), lambda b, kv: (b, kv))]
    if not has_ctx:
        out_shape.append(jax.ShapeDtypeStruct((n_seq * n_tok, KV_C * DH_C), F32))
        out_specs.append(pl.BlockSpec((n_tok, LANES), lambda b, kv: (b, kv)))
    return pl.pallas_call(
        functools.partial(_swa_kernel, n_tok=n_tok, q_blk=min(n_tok, 256), has_ctx=has_ctx),
        out_shape=out_shape,
        grid=(n_seq, KV_C),
        in_specs=in_specs,
        out_specs=out_specs,
        compiler_params=_cparams(("arbitrary", "arbitrary")),
        name="swa_mixer",
    )(*args)


def _dft_matrices(n_tok):
    k = jnp.arange(n_tok, dtype=jnp.int32)[:, None]
    s = jnp.arange(n_tok, dtype=jnp.int32)[None, :]
    ang = (math.pi / n_tok) * ((k * s) % (2 * n_tok)).astype(F32)
    cos, sin = jnp.cos(ang), jnp.sin(ang)
    alt_s = jnp.where(s % 2 == 0, 1.0, -1.0).astype(F32)
    fwd = jnp.concatenate([cos, jnp.where(k == 0, alt_s, -sin)], axis=0)
    inv_re = jnp.where(s == 0, 1.0, 2.0 * cos) / (2 * n_tok)
    alt_t = jnp.where(k % 2 == 0, 1.0, -1.0).astype(F32)
    inv_im = jnp.where(s == 0, alt_t, -2.0 * sin) / (2 * n_tok)
    inv = jnp.concatenate([inv_re, inv_im], axis=1)
    return fwd, inv


def _hyena_filter_kernel(feat_ref, w1_ref, b1_ref, fr_ref, w2_ref, b2_ref, w3_ref, b3_ref, win_ref,
                         fh_ref, fl_ref, g_ref, *, n_tok):
    fr = fr_ref[...]
    h = jnp.sin(fr * (_dot3(feat_ref[...], w1_ref[...]) + b1_ref[...]))
    h = jnp.sin(fr * (_dot3(h, w2_ref[...]) + b2_ref[...]))
    f = _dot3(h, w3_ref[...]) + b3_ref[...]
    win = win_ref[...]
    h_fwd = f[:, :D_GRP] * win
    h_bwd = jnp.where(_iota((n_tok, D_GRP), 0) == 0, 0.0, f[:, D_GRP:] * win)
    fh, fl = fh_ref[...], fl_ref[...]

    def spectrum(x):
        xh, xl = _split2(x)
        return (jnp.dot(fh, xh, preferred_element_type=F32) + jnp.dot(fl, xh, preferred_element_type=F32)
                + jnp.dot(fh, xl, preferred_element_type=F32))

    conj_sign = jnp.where(_iota((2 * n_tok, D_GRP), 0) <= n_tok, 1.0, -1.0)
    g_ref[0] = spectrum(h_fwd) + conj_sign * spectrum(h_bwd)


def hyena_filter_spectra(n_tok, w1, b1, freq, w2, b2, w3, b3, fwd_hi, fwd_lo):
    t = jnp.linspace(0.0, 1.0, n_tok, dtype=F32)[:, None]
    w = (2.0 * math.pi / n_tok) * jnp.arange(n_tok, dtype=F32)[:, None]
    bands = jnp.linspace(1e-4, HY_BANDS - 1, HY_BANDS, dtype=F32)[None, :]
    feats = jnp.concatenate([t, jnp.cos(bands * w), -jnp.sin(bands * w)], axis=-1)
    feats = jnp.pad(feats, ((0, 0), (0, LANES - HY_POS_DIM)))
    rates = jnp.abs(jnp.linspace(HY_FAST, HY_SLOW, D_GRP, dtype=F32))
    window = jnp.exp(-t * rates)
    padv = lambda a: jnp.pad(a, (0, LANES - HY_FF)).reshape(1, LANES)
    w1p = jnp.pad(w1, ((0, LANES - HY_POS_DIM), (0, LANES - HY_FF)))
    w2p = jnp.pad(w2, ((0, LANES - HY_FF), (0, LANES - HY_FF)))
    w3p = jnp.pad(w3, ((0, LANES - HY_FF), (0, 0)))
    full = lambda shape: pl.BlockSpec(shape, lambda o: (0,) * len(shape))
    return pl.pallas_call(
        functools.partial(_hyena_filter_kernel, n_tok=n_tok),
        out_shape=jax.ShapeDtypeStruct((HY_ORDER, 2 * n_tok, D_GRP), F32),
        grid=(HY_ORDER,),
        in_specs=[full((n_tok, LANES)), full((LANES, LANES)), full((1, LANES)), full((1, LANES)),
                  full((LANES, LANES)), full((1, LANES)),
                  pl.BlockSpec((LANES, 2 * D_GRP), lambda o: (0, o)),
                  pl.BlockSpec((1, 2 * D_GRP), lambda o: (0, o)),
                  full((n_tok, D_GRP)), full((2 * n_tok, n_tok)), full((2 * n_tok, n_tok))],
        out_specs=pl.BlockSpec((1, 2 * n_tok, D_GRP), lambda o: (o, 0, 0)),
        compiler_params=_cparams(("arbitrary",)),
        name="hyena_filter_spectra",
    )(feats, w1p, padv(b1), padv(freq), w2p, padv(b2), w3p, b3.reshape(1, -1), window, fwd_hi, fwd_lo)


def _hyena_conv_kernel(zy_ref, zx1_ref, zx2_ref, cwy_ref, cw1_ref, cw2_ref, cby_ref, cb1_ref, cb2_ref,
                       skip_ref, g_ref, fwd_ref, inv_ref, out_ref, *, n_tok):
    width = zy_ref.shape[1]
    row = _iota((n_tok, width), 0)
    first, last = row == 0, row == n_tok - 1

    def dwconv(x_ref, w_ref, b_ref):
        x = x_ref[...]
        w = w_ref[...]
        prev = jnp.where(first, 0.0, pltpu.roll(x, 1, 0))
        nxt = jnp.where(last, 0.0, pltpu.roll(x, n_tok - 1, 0))
        return prev * w[0:1] + x * w[1:2] + nxt * w[2:3] + b_ref[...]

    y = dwconv(zy_ref, cwy_ref, cby_ref)
    gates = (dwconv(zx1_ref, cw1_ref, cb1_ref), dwconv(zx2_ref, cw2_ref, cb2_ref))
    fwd = fwd_ref[...]
    for o in range(HY_ORDER):
        u = jnp.dot(fwd, y.astype(BF16), preferred_element_type=F32)
        a, b = u[:n_tok], u[n_tok:]
        c, d = g_ref[o, :n_tok, :], g_ref[o, n_tok:, :]
        bd = b * d
        y_re = a * c - jnp.where(first, 0.0, bd)
        y_im = jnp.where(first, bd, a * d + b * c)
        conv = (jnp.dot(inv_ref[:, :n_tok], y_re.astype(BF16), preferred_element_type=F32)
                + jnp.dot(inv_ref[:, n_tok:], y_im.astype(BF16), preferred_element_type=F32))
        y = gates[o] * (conv + skip_ref[o:o + 1, :] * y)
    out_ref[...] = y.astype(BF16)


def hyena_mixer(z, conv_w, conv_b, skip, spectra, fwd_hi, inv_hi, n_seq, n_tok, row_blk0):
    width = 256
    nw = D_GRP // width
    zcol = lambda base: pl.BlockSpec((n_tok, width), lambda b, j: (row_blk0 + b, base * LANES // width + j))
    wcol = lambda part: pl.BlockSpec((3, width), lambda b, j: (0, part * nw + j))
    bcol = lambda part: pl.BlockSpec((1, width), lambda b, j: (0, part * nw + j))
    cb = conv_b.reshape(1, 3 * D_GRP)
    in_specs = [zcol(ZD_Y), zcol(ZD_X1), zcol(ZD_X2), wcol(0), wcol(1), wcol(2), bcol(0), bcol(1), bcol(2),
                pl.BlockSpec((HY_ORDER, width), lambda b, j: (0, j)),
                pl.BlockSpec((HY_ORDER, 2 * n_tok, width), lambda b, j: (0, 0, j)),
                pl.BlockSpec((2 * n_tok, n_tok), lambda b, j: (0, 0)),
                pl.BlockSpec((n_tok, 2 * n_tok), lambda b, j: (0, 0))]
    args = [z, z, z, conv_w, conv_w, conv_w, cb, cb, cb, skip, spectra, fwd_hi, inv_hi]
    return pl.pallas_call(
        functools.partial(_hyena_conv_kernel, n_tok=n_tok),
        out_shape=jax.ShapeDtypeStruct((n_seq * n_tok, D_GRP), BF16),
        grid=(n_seq, nw),
        in_specs=in_specs,
        out_specs=pl.BlockSpec((n_tok, width), lambda b, j: (b, j)),
        compiler_params=_cparams(("arbitrary", "arbitrary")),
        name="hyena_mixer",
    )(*args)


def _outproj_kernel(*refs, n_ctx_blk):
    ctx_parts, dec_parts = refs[0:N_MIXERS], refs[N_MIXERS:2 * N_MIXERS]
    w_ref, x_ref, g1_ref, ng_ref, sc_ref, sh_ref, x1_ref, h2_ref = refs[2 * N_MIXERS:]
    is_ctx = pl.program_id(0) < n_ctx_blk
    acc = None
    for g in range(N_MIXERS):
        part = jnp.where(is_ctx, ctx_parts[g][...], dec_parts[g][...])
        term = jnp.dot(part, w_ref[g * D_GRP:(g + 1) * D_GRP, :], preferred_element_type=F32)
        acc = term if acc is None else acc + term
    x1 = x_ref[...] + g1_ref[0] * acc
    x1_ref[...] = x1
    h2_ref[...] = _ada_norm(x1, ng_ref[...], sc_ref[0], sh_ref[0]).astype(BF16)


def out_projection(ctx_parts, dec_parts, w_out_bf16, x, mod, layer, norm_g, n_ctx_tok, dec_seq):
    m = x.shape[0]
    tm = 512
    n_ctx_blk = n_ctx_tok // tm
    modspec = lambda part: pl.BlockSpec((1, 1, D_MODEL), _mod_row_index(layer, part, tm, n_ctx_tok, dec_seq))
    ctx_part = pl.BlockSpec((tm, D_GRP), lambda i: (jnp.minimum(i, n_ctx_blk - 1), 0))
    dec_part = pl.BlockSpec((tm, D_GRP), lambda i: (jnp.maximum(i - n_ctx_blk, 0), 0))
    row = pl.BlockSpec((tm, D_MODEL), lambda i: (i, 0))
    return pl.pallas_call(
        functools.partial(_outproj_kernel, n_ctx_blk=n_ctx_blk),
        out_shape=[jax.ShapeDtypeStruct((m, D_MODEL), F32), jax.ShapeDtypeStruct((m, D_MODEL), BF16)],
        grid=(m // tm,),
        in_specs=[ctx_part] * N_MIXERS + [dec_part] * N_MIXERS
        + [pl.BlockSpec((D_MODEL, D_MODEL), lambda i: (0, 0)), row,
           modspec(2), pl.BlockSpec((1, D_MODEL), lambda i: (0, 0)), modspec(4), modspec(3)],
        out_specs=[row, row],
        compiler_params=_cparams(("arbitrary",)),
        name="out_projection",
    )(*ctx_parts, *dec_parts, w_out_bf16, x, mod, norm_g.reshape(1, D_MODEL), mod, mod)


def _ffn_kernel(h_ref, wa_ref, wb_ref, cwa_ref, cwb_ref, cba_ref, cbb_ref, wd_ref, y_ref,
                ua0_ref, ub0_ref, ua1_ref, ub1_ref, *, tile_rows, n_ctx_tok, ctx_seq, dec_seq):
    i = pl.program_id(0)
    j = pl.program_id(1)
    tf = wd_ref.shape[1]
    body = slice(SUBLANES, SUBLANES + tile_rows)

    @pl.when((i == 0) & (j == 0))
    def _():
        for ref in (ua0_ref, ub0_ref, ua1_ref, ub1_ref):
            ref[...] = jnp.zeros_like(ref)

    half = tile_rows // 2

    def park(dst, w_ref):
        dst[body, :] = jnp.dot(h_ref[...], w_ref[0].astype(BF16), preferred_element_type=F32)

    def gated(ready_a, ready_b, r0, seq):
        lo = SUBLANES + r0
        edges = seq != tile_rows
        if edges:
            pos = (r0 + _iota((half, tf), 0)) & (seq - 1)
            first, last = pos == 0, pos == seq - 1

        def dwconv(u_ref, cw_ref, cb_ref):
            cw = cw_ref[0].astype(BF16)
            prev = u_ref[lo - 1:lo - 1 + half, :]
            nxt = u_ref[lo + 1:lo + 1 + half, :]
            if edges:
                prev = jnp.where(first, 0.0, prev)
                nxt = jnp.where(last, 0.0, nxt)
            prev, cur, nxt = prev.astype(BF16), u_ref[lo:lo + half, :].astype(BF16), nxt.astype(BF16)
            return prev * cw[0:1] + cur * cw[1:2] + nxt * cw[2:3] + cb_ref[0].astype(BF16)

        ua = dwconv(ready_a, cwa_ref, cba_ref)
        ub = dwconv(ready_b, cwb_ref, cbb_ref)
        return ua * _sigmoid(ua) * ub

    def step(park_a, park_b, ready_a, ready_b, first_finish, seq):
        wd = wd_ref[0].astype(BF16)
        for r0, dst, w_ref in ((0, park_a, wa_ref), (half, park_b, wb_ref)):
            park(dst, w_ref)
            contrib = jnp.dot(gated(ready_a, ready_b, r0, seq), wd, preferred_element_type=F32)
            if first_finish:
                y_ref[r0:r0 + half, :] = contrib
            else:
                y_ref[r0:r0 + half, :] += contrib

    @pl.when(j == 0)
    def _():
        park(ua0_ref, wa_ref)
        park(ub0_ref, wb_ref)

    is_ctx = i * tile_rows < n_ctx_tok
    for in_group, seq in ((is_ctx, ctx_seq), (jnp.logical_not(is_ctx), dec_seq)):
        @pl.when(in_group & (j == 1))
        def _():
            step(ua1_ref, ub1_ref, ua0_ref, ub0_ref, True, seq)

        @pl.when(in_group & (j > 1) & (j % 2 == 0))
        def _():
            step(ua0_ref, ub0_ref, ua1_ref, ub1_ref, False, seq)

        @pl.when(in_group & (j > 1) & (j % 2 == 1))
        def _():
            step(ua1_ref, ub1_ref, ua0_ref, ub0_ref, False, seq)


def conv_ffn(h2, layer, w_up, conv_w, conv_b, w_down, n_ctx_tok, ctx_seq, dec_seq):
    m = h2.shape[0]
    tm, tf = 1024, FFN_TF
    nf = D_FF // tf
    for seq in (ctx_seq, dec_seq):
        assert tm % seq == 0 and seq & (seq - 1) == 0, seq
    assert n_ctx_tok % tm == 0 and m % tm == 0
    cb = conv_b.reshape(DEPTH, 1, 2 * D_FF)
    up = lambda j: jnp.minimum(j, nf - 1)
    dn = lambda j: jnp.maximum(j - 1, 0)
    park = pltpu.VMEM((tm + 2 * SUBLANES, tf), F32)
    return pl.pallas_call(
        functools.partial(_ffn_kernel, tile_rows=tm, n_ctx_tok=n_ctx_tok, ctx_seq=ctx_seq, dec_seq=dec_seq),
        out_shape=jax.ShapeDtypeStruct((m, D_MODEL), F32),
        grid=(m // tm, nf + 1),
        in_specs=[pl.BlockSpec((tm, D_MODEL), lambda i, j: (i, 0)),
                  pl.BlockSpec((1, D_MODEL, tf), lambda i, j: (layer, 0, up(j))),
                  pl.BlockSpec((1, D_MODEL, tf), lambda i, j: (layer, 0, nf + up(j))),
                  pl.BlockSpec((1, 3, tf), lambda i, j: (layer, 0, dn(j))),
                  pl.BlockSpec((1, 3, tf), lambda i, j: (layer, 0, nf + dn(j))),
                  pl.BlockSpec((1, 1, tf), lambda i, j: (layer, 0, dn(j))),
                  pl.BlockSpec((1, 1, tf), lambda i, j: (layer, 0, nf + dn(j))),
                  pl.BlockSpec((1, tf, D_MODEL), lambda i, j: (layer, dn(j), 0))],
        out_specs=pl.BlockSpec((tm, D_MODEL), lambda i, j: (i, 0)),
        scratch_shapes=[park, park, park, park],
        compiler_params=_cparams(("arbitrary", "arbitrary")),
        name="conv_ffn",
    )(h2, w_up, w_up, conv_w, conv_w, cb, cb, w_down)


def _residual_kernel(x_ref, y_ref, g_ref, o_ref):
    o_ref[...] = x_ref[...] + g_ref[0] * y_ref[...]


def final_residual(x1, y, mod, layer, row0, n_rows, n_ctx_tok, dec_seq):
    tm = 512
    blk0 = row0 // tm

    def mod_index(i):
        start = (blk0 + i) * tm
        row = jnp.where(start < n_ctx_tok, 0, 1 + (start - n_ctx_tok) // dec_seq)
        return (layer * 48 + row * 6 + 5, 0, 0)

    return pl.pallas_call(
        _residual_kernel,
        out_shape=jax.ShapeDtypeStruct((n_rows, D_MODEL), F32),
        grid=(n_rows // tm,),
        in_specs=[pl.BlockSpec((tm, D_MODEL), lambda i: (blk0 + i, 0)),
                  pl.BlockSpec((tm, D_MODEL), lambda i: (blk0 + i, 0)),
                  pl.BlockSpec((1, 1, D_MODEL), mod_index)],
        out_specs=pl.BlockSpec((tm, D_MODEL), lambda i: (i, 0)),
        compiler_params=_cparams(("arbitrary",)),
        name="final_residual",
    )(x1, y, mod)


def kernel(x_prompt, x_sample, cache_diff_k, cache_diff_v, cache_swa_k, cache_swa_v, state_mlstm_C, state_mlstm_n, state_mlstm_m, c, c_ctx, w_mod, b_mod, norm1_g, norm2_g, w_in, mlstm_ig_b, mlstm_fg_b, mlstm_norm_g, diff_qn_g, diff_kn_g, diff_lam, diff_out_g, swa_qn_g, swa_kn_g, swa_sink, hy_conv_w, hy_conv_b, hy_w1, hy_b1, hy_freq, hy_w2, hy_b2, hy_w3, hy_b3, hy_bias, w_out, ffn_w_up, ffn_conv_w, ffn_conv_b, ffn_w_down):
    n_ctx, ctx_seq = x_prompt.shape[:2]
    n_dec, dec_seq = x_sample.shape[:2]
    n_ctx_tok = n_ctx * ctx_seq
    n_dec_tok = n_dec * dec_seq
    dec_blk0 = n_ctx_tok // dec_seq

    x = (x_prompt.reshape(n_ctx_tok, D_MODEL), x_sample.reshape(n_dec_tok, D_MODEL))
    w_pack, wg_hi, wg_lo = pack_in_weights(w_in)
    cond8 = jnp.concatenate([c_ctx[None, :], c, jnp.zeros((8 - 1 - n_dec, D_MODEL), F32)], axis=0)
    mod = mod_vectors(cond8, w_mod, b_mod)

    dft = {}
    for t in (ctx_seq, dec_seq):
        fwd, inv = _dft_matrices(t)
        dft[t] = dict(fwd=fwd.astype(BF16), inv=inv.astype(BF16), split=_split2(fwd))

    delta = None
    caches = []
    for l in range(DEPTH):
        x, z, gates = in_projection(x, delta, mod, l, norm1_g[l], w_pack, wg_hi, wg_lo, n_ctx_tok, dec_seq)

        gate_bias = jnp.pad(jnp.concatenate([mlstm_ig_b[l].reshape(-1), mlstm_fg_b[l].reshape(-1)]),
                            (0, LANES - 4 * H_A)).reshape(1, LANES)
        lam_init = 0.8 - 0.6 * math.exp(-0.3 * l)

        a_ctx, c_new, n_new, m_new = mlstm_mixer(z, gates, gate_bias, mlstm_norm_g[l], None,
                                                 n_ctx, ctx_seq, 0, ctx_seq, True)
        b_ctx, diff_k = diff_mixer(z, diff_qn_g[l], diff_kn_g[l], diff_lam[l], diff_out_g[l], lam_init,
                                   None, n_ctx, ctx_seq, 0)
        c_ctx_out, swa_k = swa_mixer(z, swa_qn_g[l], swa_kn_g[l], swa_sink[l], None, n_ctx, ctx_seq, 0)
        spec_ctx = hyena_filter_spectra(ctx_seq, hy_w1[l], hy_b1[l], hy_freq[l], hy_w2[l], hy_b2[l],
                                        hy_w3[l], hy_b3[l], *dft[ctx_seq]["split"])
        d_ctx = hyena_mixer(z, hy_conv_w[l], hy_conv_b[l], hy_bias[l], spec_ctx, dft[ctx_seq]["fwd"],
                            dft[ctx_seq]["inv"], n_ctx, ctx_seq, 0)
        a_dec, = mlstm_mixer(z, gates, gate_bias, mlstm_norm_g[l],
                             (state_mlstm_C[:, l], state_mlstm_n[:, l], state_mlstm_m[:, l]),
                             n_dec, dec_seq, dec_blk0, 512, False)
        b_dec, = diff_mixer(z, diff_qn_g[l], diff_kn_g[l], diff_lam[l], diff_out_g[l], lam_init,
                            (cache_diff_k[:, l], cache_diff_v[:, l]), n_dec, dec_seq, dec_blk0)
        c_dec, = swa_mixer(z, swa_qn_g[l], swa_kn_g[l], swa_sink[l],
                           (cache_swa_k[:, l], cache_swa_v[:, l]), n_dec, dec_seq, dec_blk0)
        spec_dec = hyena_filter_spectra(dec_seq, hy_w1[l], hy_b1[l], hy_freq[l], hy_w2[l], hy_b2[l],
                                        hy_w3[l], hy_b3[l], *dft[dec_seq]["split"])
        d_dec = hyena_mixer(z, hy_conv_w[l], hy_conv_b[l], hy_bias[l], spec_dec, dft[dec_seq]["fwd"],
                            dft[dec_seq]["inv"], n_dec, dec_seq, dec_blk0)

        x, h2 = out_projection((a_ctx, b_ctx, c_ctx_out, d_ctx), (a_dec, b_dec, c_dec, d_dec),
                               w_out[l].astype(BF16), x, mod, l, norm2_g[l], n_ctx_tok, dec_seq)
        delta = conv_ffn(h2, l, ffn_w_up, ffn_conv_w, ffn_conv_b, ffn_w_down, n_ctx_tok, ctx_seq, dec_seq)

        zc = z[:n_ctx_tok]
        caches.append((
            diff_k.reshape(n_ctx, ctx_seq, H_B, 2, DK_B),
            zc[:, ZB_V * LANES:(ZB_V + 4) * LANES].reshape(n_ctx, ctx_seq, H_B, DV_B),
            swa_k.reshape(n_ctx, ctx_seq, KV_C, DH_C),
            zc[:, ZC_V * LANES:(ZC_V + 2) * LANES].reshape(n_ctx, ctx_seq, KV_C, DH_C),
            c_new,
            n_new.reshape(n_ctx, 2, H_A, DH_A),
            m_new.reshape(n_ctx, 2, H_A),
        ))

    y_prompt = final_residual(x, delta, mod, DEPTH - 1, 0, n_ctx_tok, n_ctx_tok, dec_seq)
    y_sample = final_residual(x, delta, mod, DEPTH - 1, n_ctx_tok, n_dec_tok, n_ctx_tok, dec_seq)
    outs = [jnp.stack([s[i] for s in caches], axis=1) for i in range(7)]
    return (y_prompt.reshape(n_ctx, ctx_seq, D_MODEL), y_sample.reshape(n_dec, dec_seq, D_MODEL), *outs)
```

```python
import functools
import math

import jax
import jax.numpy as jnp
from jax import lax
from jax.experimental import pallas as pl
from jax.experimental.pallas import tpu as pltpu

D_MODEL = 2048
DEPTH = 2
GRID_W = 64
D_GRP = 512
N_MIXERS = D_MODEL // D_GRP
H_A = 4
DH_A = 128
H_B = 4
DV_B = 128
DK_B = 64
H_C = 4
KV_C = 2
G_C = 2
DH_C = 128
WINDOW = 128
HY_ORDER = 2
HY_POS_DIM = 33
HY_BANDS = (HY_POS_DIM - 1) // 2
HY_FF = 64
HY_FAST = math.log(1e-2) / 0.3
HY_SLOW = math.log(1e-2) / 1.5
D_FF = 5632
ROPE_BASE = 10000.0
EPS = 1e-6
N_A = 4 * D_GRP + 4 * H_A

LANES = 128
SUBLANES = 8
V7X_VMEM_LIMIT_BYTES = 56 * 1024 * 1024
Z_COLS = 12 * D_GRP
INPROJ_TN = 3 * D_GRP
FFN_TF = 256
ZA_Q, ZA_K, ZA_V, ZA_O = 0, 4, 8, 12
ZB_Q, ZB_K, ZB_V = 16, 20, 24
ZC_Q, ZC_K, ZC_V = 28, 32, 34
ZD_Y, ZD_X1, ZD_X2 = 36, 40, 44

BF16 = jnp.bfloat16
F32 = jnp.float32


def _cparams(sem):
    return pltpu.CompilerParams(dimension_semantics=sem, vmem_limit_bytes=V7X_VMEM_LIMIT_BYTES)


def _dot(a, b):
    return jnp.dot(a.astype(BF16), b.astype(BF16), preferred_element_type=F32)


def _dot_nt(a, b):
    return lax.dot_general(a.astype(BF16), b.astype(BF16), (((1,), (1,)), ((), ())),
                           preferred_element_type=F32)


def _bf16_head(a):
    bits = lax.bitcast_convert_type(a, jnp.int32) & jnp.int32(-65536)
    return lax.bitcast_convert_type(bits, F32)


def _split2(a):
    head = _bf16_head(a)
    return head.astype(BF16), (a - head).astype(BF16)


def _split3(a):
    head = _bf16_head(a)
    rest = a - head
    mid = _bf16_head(rest)
    return head.astype(BF16), mid.astype(BF16), (rest - mid).astype(BF16)


def _dot3(a, b):
    ah, al = _split2(a)
    bh, bl = _split2(b)
    return (jnp.dot(ah, bh, preferred_element_type=F32) + jnp.dot(al, bh, preferred_element_type=F32)
            + jnp.dot(ah, bl, preferred_element_type=F32))


def _iota(shape, dim):
    return lax.broadcasted_iota(jnp.int32, shape, dim)


def _sigmoid(x):
    return 1.0 / (1.0 + jnp.exp(-x))


def _log_sigmoid(x):
    return jnp.minimum(x, 0.0) - jnp.log(1.0 + jnp.exp(-jnp.abs(x)))


def _mod_kernel(c_ref, w_ref, b_ref, o_ref):
    k = pl.program_id(1)
    c = c_ref[...]
    part = _dot(c * _sigmoid(c), w_ref[0])

    @pl.when(k == 0)
    def _():
        o_ref[0] = part + b_ref[0]

    @pl.when(k > 0)
    def _():
        o_ref[0] += part


def mod_vectors(cond8, w_mod, b_mod):
    tk = 256
    n = w_mod.shape[2]
    out = pl.pallas_call(
        _mod_kernel,
        out_shape=jax.ShapeDtypeStruct((DEPTH, 8, n), F32),
        grid=(DEPTH, D_MODEL // tk),
        in_specs=[pl.BlockSpec((8, tk), lambda l, k: (0, k)),
                  pl.BlockSpec((1, tk, n), lambda l, k: (l, k, 0)),
                  pl.BlockSpec((1, 1, n), lambda l, k: (l, 0, 0))],
        out_specs=pl.BlockSpec((1, 8, n), lambda l, k: (l, 0, 0)),
        compiler_params=_cparams(("arbitrary", "arbitrary")),
        name="mod_vectors",
    )(cond8, w_mod, b_mod.reshape(DEPTH, 1, n))
    return out.reshape(DEPTH * 8 * 6, 1, D_MODEL)


def _mod_row_index(layer, part, tile_rows, n_ctx_tok, dec_seq):
    def index(i, *_):
        start = i * tile_rows
        row = jnp.where(start < n_ctx_tok, 0, 1 + (start - n_ctx_tok) // dec_seq)
        return (layer * 48 + row * 6 + part, 0, 0)
    return index


def _ada_norm(x, ng, sc, sh):
    y = x * lax.rsqrt(jnp.mean(x * x, axis=-1, keepdims=True) + EPS)
    return y * ng * (1.0 + sc) + sh


def _pack_in_kernel(a_ref, b_ref, w_ref, gh_ref, gl_ref):
    c = pl.program_id(1)
    n_aligned = 4 * D_GRP // a_ref.shape[1]
    gap = 4 * H_A

    @pl.when(c < n_aligned)
    def _():
        w_ref[0, 0] = a_ref[0].T.astype(BF16)

    @pl.when(c >= n_aligned)
    def _():
        w_ref[0, 0] = jnp.concatenate([a_ref[0, gap:, :], b_ref[0]], axis=0).T.astype(BF16)

    @pl.when(c == n_aligned)
    def _():
        rows = a_ref[0, 0:LANES, :]
        hi, lo = _split2(jnp.where(_iota(rows.shape, 0) < gap, rows, 0.0).T)
        gh_ref[0] = hi
        gl_ref[0] = lo


def pack_in_weights(w_in):
    tc = 512
    gap = 4 * H_A
    n_blk = Z_COLS // tc
    per_tile = INPROJ_TN // tc
    w_t = jnp.swapaxes(w_in, 1, 2)
    gate = pl.BlockSpec((1, D_MODEL, LANES), lambda l, c: (l, 0, 0))
    return pl.pallas_call(
        _pack_in_kernel,
        out_shape=[jax.ShapeDtypeStruct((DEPTH, Z_COLS // INPROJ_TN, D_MODEL, INPROJ_TN), BF16),
                   jax.ShapeDtypeStruct((DEPTH, D_MODEL, LANES), BF16),
                   jax.ShapeDtypeStruct((DEPTH, D_MODEL, LANES), BF16)],
        grid=(DEPTH, n_blk),
        in_specs=[pl.BlockSpec((1, tc, D_MODEL), lambda l, c: (l, c, 0)),
                  pl.BlockSpec((1, gap, D_MODEL), lambda l, c: (l, (c + 1) * (tc // gap), 0))],
        out_specs=[pl.BlockSpec((1, 1, D_MODEL, tc), lambda l, c: (l, c // per_tile, 0, c % per_tile)),
                   gate, gate],
        compiler_params=_cparams(("arbitrary", "arbitrary")),
        name="pack_in_weights",
    )(w_t, w_t)


def _inproj_kernel(*refs, has_delta, n_ctx_blk):
    if has_delta:
        (x_ref, y_ref, g2_ref, ng_ref, sc_ref, sh_ref, w_ref, wgh_ref, wgl_ref,
         xn_ref, z_ref, gate_ref, hb_ref) = refs
    else:
        (xc_ref, xd_ref, ng_ref, sc_ref, sh_ref, w_ref, wgh_ref, wgl_ref,
         xn_ref, z_ref, gate_ref, hb_ref) = refs

    i = pl.program_id(0)

    @pl.when(pl.program_id(1) == 0)
    def _():
        if has_delta:
            x = x_ref[...] + g2_ref[0] * y_ref[...]
        else:
            x = jnp.where(i < n_ctx_blk, xc_ref[...], xd_ref[...])
        xn_ref[...] = x
        h = _ada_norm(x, ng_ref[...], sc_ref[0], sh_ref[0])
        hb_ref[...] = h.astype(BF16)
        hh, hl = _split2(h)
        wgh = wgh_ref[0]
        gate_ref[...] = (jnp.dot(hh, wgh, preferred_element_type=F32)
                         + jnp.dot(hl, wgh, preferred_element_type=F32)
                         + jnp.dot(hh, wgl_ref[0], preferred_element_type=F32))

    z_ref[...] = jnp.dot(hb_ref[...], w_ref[0, 0], preferred_element_type=F32)


def in_projection(x, delta, mod, layer, norm_g, w_pack, wg_hi, wg_lo, n_ctx_tok, dec_seq):
    tm, tn = 512, INPROJ_TN
    has_delta = delta is not None
    n_ctx_blk = n_ctx_tok // tm
    row = pl.BlockSpec((tm, D_MODEL), lambda i, j: (i, 0))
    modspec = lambda part: pl.BlockSpec((1, 1, D_MODEL), _mod_row_index(layer, part, tm, n_ctx_tok, dec_seq))
    vec = pl.BlockSpec((1, D_MODEL), lambda i, j: (0, 0))
    if has_delta:
        m = x.shape[0]
        g2spec = pl.BlockSpec((1, 1, D_MODEL), _mod_row_index(layer - 1, 5, tm, n_ctx_tok, dec_seq))
        in_specs = [row, row, g2spec]
        args = [x, delta, mod]
    else:
        x_ctx, x_dec = x
        m = x_ctx.shape[0] + x_dec.shape[0]
        in_specs = [pl.BlockSpec((tm, D_MODEL), lambda i, j: (jnp.minimum(i, n_ctx_blk - 1), 0)),
                    pl.BlockSpec((tm, D_MODEL), lambda i, j: (jnp.maximum(i - n_ctx_blk, 0), 0))]
        args = [x_ctx, x_dec]
    gate_w = pl.BlockSpec((1, D_MODEL, LANES), lambda i, j: (layer, 0, 0))
    in_specs += [vec, modspec(1), modspec(0),
                 pl.BlockSpec((1, 1, D_MODEL, tn), lambda i, j: (layer, j, 0, 0)), gate_w, gate_w]
    args += [norm_g.reshape(1, D_MODEL), mod, mod, w_pack, wg_hi, wg_lo]
    return pl.pallas_call(
        functools.partial(_inproj_kernel, has_delta=has_delta, n_ctx_blk=n_ctx_blk),
        out_shape=[jax.ShapeDtypeStruct((m, D_MODEL), F32), jax.ShapeDtypeStruct((m, Z_COLS), F32),
                   jax.ShapeDtypeStruct((m, LANES), F32)],
        grid=(m // tm, Z_COLS // tn),
        in_specs=in_specs,
        out_specs=[row, pl.BlockSpec((tm, tn), lambda i, j: (i, j)), pl.BlockSpec((tm, LANES), lambda i, j: (i, 0))],
        scratch_shapes=[pltpu.VMEM((tm, D_MODEL), BF16)],
        compiler_params=_cparams(("arbitrary", "arbitrary")),
        name="in_projection",
    )(*args)


def _mlstm_kernel(*refs, n_tok, chunk, has_state, want_state):
    refs = list(refs)
    q_ref, k_ref, v_ref, o_ref, gate_ref, gb_ref, ng_ref = refs[:7]
    pos = 7
    if has_state:
        c0_ref, n0_ref, m0_ref = refs[pos:pos + 3]
        pos += 3
    out_ref = refs[pos]
    pos += 1
    if want_state:
        cn_ref, nn_ref, mn_ref = refs[pos:pos + 3]
        pos += 3
    hacc_ref = refs[pos]

    L = chunk
    nc = n_tok // L
    scale = DH_A ** -0.5
    r_idx = _iota((L, L), 0)
    c_idx = _iota((L, L), 1)
    ones_b = jnp.ones((L, DH_A), BF16)

    g = gate_ref[...] + gb_ref[...]
    vals = jnp.where(_iota((n_tok, LANES), 1) < 2 * H_A, g, _log_sigmoid(g))

    for d in range(2):
        causal = (c_idx <= r_idx) if d == 0 else (c_idx >= r_idx)
        tri = jnp.where(causal, 1.0, 0.0).astype(BF16)
        if has_state:
            c_mem = [c0_ref[0, d, h] for h in range(H_A)]
            n_mem = [n0_ref[0, d, h] for h in range(H_A)]
            m_prev = [m0_ref[0, d, h] for h in range(H_A)]
            pad = jnp.zeros((SUBLANES - 1, DH_A), F32)
            n_mat = [_dot(jnp.concatenate([n, pad], axis=0).T, jnp.ones((SUBLANES, DH_A), BF16)) for n in n_mem]
        else:
            c_mem = [None] * H_A
            n_mem = [None] * H_A
            n_mat = [None] * H_A
            m_prev = [jnp.zeros((1, 1), F32)] * H_A
        order = range(nc) if d == 0 else range(nc - 1, -1, -1)
        for ci, c in enumerate(order):
            rows = slice(c * L, (c + 1) * L)
            v_c = vals[rows]
            p0, p1, p2 = _split3(v_c)
            cum = (jnp.dot(tri, p0, preferred_element_type=F32) + jnp.dot(tri, p1, preferred_element_type=F32)
                   + jnp.dot(tri, p2, preferred_element_type=F32))
            v_t = v_c.T
            cum_t = cum.T
            for h in range(H_A):
                col_i = d * H_A + h
                col_f = 2 * H_A + d * H_A + h
                lanes = slice(h * DH_A, (h + 1) * DH_A)
                c_col = cum[:, col_f:col_f + 1]
                ig_col = v_c[:, col_i:col_i + 1]
                c_row = cum_t[col_f:col_f + 1, :]
                ig_row = v_t[col_i:col_i + 1, :]

                dmat = jnp.where(causal, c_col - c_row + ig_row, -jnp.inf)
                inter = c_col + m_prev[h]
                m_col = jnp.maximum(inter, jnp.max(dmat, axis=1, keepdims=True))
                dexp = jnp.exp(dmat - m_col)
                qs = q_ref[rows, lanes] * scale
                qb = qs.astype(BF16)
                kf = k_ref[rows, lanes]
                vb = jnp.concatenate([v_ref[rows, lanes].astype(BF16), ones_b], axis=1)
                pb = (_dot_nt(qb, kf) * dexp).astype(BF16)
                nd = jnp.dot(pb, vb, preferred_element_type=F32)
                carried = c_mem[h] is not None
                if carried:
                    w_inter = jnp.broadcast_to(jnp.exp(inter - m_col), (L, 2 * DH_A))
                    nd = nd + w_inter * _dot(qb, jnp.concatenate([c_mem[h], n_mat[h]], axis=1))
                hh = nd[:, :DH_A] / jnp.maximum(jnp.abs(nd[:, DH_A:]), jnp.exp(-m_col))
                if d == 0:
                    hacc_ref[rows, lanes] = hh
                else:
                    hacc_ref[rows, lanes] += hh

                if want_state or ci < nc - 1:
                    total = c_col[L - 1:L, :] if d == 0 else c_col[0:1, :]
                    w_row = total - c_row + ig_row
                    m_new = jnp.maximum(total + m_prev[h], jnp.max(w_row, axis=1, keepdims=True))
                    decay = jnp.exp(total + m_prev[h] - m_new)
                    wk = jnp.exp(total - c_col + ig_col - m_new)
                    kw = kf * wk
                    kw_t = kw.T.astype(BF16)
                    upd = jnp.dot(kw_t, vb, preferred_element_type=F32)
                    c_upd, n_upd = upd[:, :DH_A], upd[:, DH_A:]
                    c_mem[h] = decay * c_mem[h] + c_upd if carried else c_upd
                    if ci < nc - 1:
                        n_mat[h] = decay * n_mat[h] + n_upd if carried else n_upd
                    if want_state:
                        n_row = jnp.sum(kw, axis=0, keepdims=True)
                        n_mem[h] = decay * n_mem[h] + n_row if carried else n_row
                    m_prev[h] = m_new
        if want_state:
            for h in range(H_A):
                cn_ref[0, d, h] = c_mem[h]
                nn_ref[0, d, h] = n_mem[h]
                mn_ref[0, d, h] = m_prev[h]

    for h in range(H_A):
        lanes = slice(h * DH_A, (h + 1) * DH_A)
        hs = hacc_ref[:, lanes]
        y = hs * lax.rsqrt(jnp.mean(hs * hs, axis=1, keepdims=True) + EPS) * ng_ref[:, lanes]
        out_ref[:, lanes] = (y * _sigmoid(o_ref[:, lanes])).astype(BF16)


def mlstm_mixer(z, gates, gate_bias, norm_g, state, n_seq, n_tok, row_blk0, chunk, want_state):
    has_state = state is not None
    col = lambda base: pl.BlockSpec((n_tok, D_GRP), lambda b: (row_blk0 + b, base * LANES // D_GRP))
    in_specs = [col(ZA_Q), col(ZA_K), col(ZA_V), col(ZA_O),
                pl.BlockSpec((n_tok, LANES), lambda b: (row_blk0 + b, 0)),
                pl.BlockSpec((1, LANES), lambda b: (0, 0)),
                pl.BlockSpec((1, D_GRP), lambda b: (0, 0))]
    args = [z, z, z, z, gates, gate_bias, norm_g.reshape(1, D_GRP)]
    st_specs = [pl.BlockSpec((1, 2, H_A, DH_A, DH_A), lambda b: (b, 0, 0, 0, 0)),
                pl.BlockSpec((1, 2, H_A, 1, DH_A), lambda b: (b, 0, 0, 0, 0)),
                pl.BlockSpec((1, 2, H_A, 1, 1), lambda b: (b, 0, 0, 0, 0))]
    if has_state:
        c0, n0, m0 = state
        in_specs += st_specs
        args += [c0, n0.reshape(n_seq, 2, H_A, 1, DH_A), m0.reshape(n_seq, 2, H_A, 1, 1)]
    out_shape = [jax.ShapeDtypeStruct((n_seq * n_tok, D_GRP), BF16)]
    out_specs = [pl.BlockSpec((n_tok, D_GRP), lambda b: (b, 0))]
    if want_state:
        out_shape += [jax.ShapeDtypeStruct((n_seq, 2, H_A, DH_A, DH_A), F32),
                      jax.ShapeDtypeStruct((n_seq, 2, H_A, 1, DH_A), F32),
                      jax.ShapeDtypeStruct((n_seq, 2, H_A, 1, 1), F32)]
        out_specs += st_specs
    outs = pl.pallas_call(
        functools.partial(_mlstm_kernel, n_tok=n_tok, chunk=chunk, has_state=has_state, want_state=want_state),
        out_shape=out_shape,
        grid=(n_seq,),
        in_specs=in_specs,
        out_specs=out_specs,
        scratch_shapes=[pltpu.VMEM((n_tok, D_GRP), F32)],
        compiler_params=_cparams(("arbitrary",)),
        name="mlstm_mixer",
    )(*args)
    return outs


def _rope_tables(n_tok, dh):
    rows = n_tok // GRID_W
    t_row = jnp.repeat(jnp.arange(rows, dtype=F32), GRID_W)
    t_col = jnp.tile(jnp.arange(GRID_W, dtype=F32), rows)
    n_freq = dh // 4
    inv = ROPE_BASE ** (-jnp.arange(n_freq, dtype=F32) / n_freq)
    ang = jnp.concatenate([t_row[:, None] * inv, t_col[:, None] * inv], axis=-1)
    cos, sin = jnp.cos(ang), jnp.sin(ang)
    reps = LANES // dh
    cos_t = jnp.tile(jnp.concatenate([cos, cos], axis=-1), (1, reps))
    sin_t = jnp.tile(jnp.concatenate([-sin, sin], axis=-1), (1, reps))
    return cos_t, sin_t


def _rope(x, cos_t, sin_t, dh):
    half = dh // 2
    if dh == LANES:
        partner = pltpu.roll(x, half, 1)
    else:
        first = (_iota(x.shape, 1) & half) == 0
        partner = jnp.where(first, pltpu.roll(x, LANES - half, 1), pltpu.roll(x, half, 1))
    return x * cos_t + partner * sin_t


def _rms(x, g):
    return x * lax.rsqrt(jnp.mean(x * x, axis=1, keepdims=True) + EPS) * g


def _with_ones(v):
    return jnp.concatenate([v.astype(BF16), jnp.ones((v.shape[0], LANES), BF16)], axis=1)


def _diff_kernel(*refs, n_tok, q_blk, lam_init, has_ctx):
    refs = list(refs)
    q_ref, k_ref, v_ref, qg_ref, kg_ref, lam_ref, og_ref = refs[:7]
    pos = 7
    if has_ctx:
        ck_ref, cv_ref, cos_ref, sin_ref = refs[pos:pos + 4]
        pos += 4
    out_ref = refs[pos]
    pos += 1
    if not has_ctx:
        kout_ref = refs[pos]

    grp0 = _iota((n_tok, LANES), 1) < DK_B

    def group_norm(x, g):
        x2 = x * x
        s0 = jnp.sum(jnp.where(grp0, x2, 0.0), axis=1, keepdims=True)
        s1 = jnp.sum(jnp.where(grp0, 0.0, x2), axis=1, keepdims=True)
        ms = jnp.where(grp0, s0, s1) * (1.0 / DK_B)
        return x * lax.rsqrt(ms + EPS) * g

    qn = group_norm(q_ref[...], qg_ref[...])
    kn = group_norm(k_ref[...], kg_ref[...])
    if has_ctx:
        cos_t, sin_t = cos_ref[...], sin_ref[...]
        qn = _rope(qn, cos_t, sin_t, DK_B)
        kn = _rope(kn, cos_t, sin_t, DK_B)
    else:
        kout_ref[...] = kn
    kb = kn.astype(BF16)
    vb = _with_ones(v_ref[...])
    if has_ctx:
        ckb = ck_ref[0].astype(BF16)
        cvb = _with_ones(cv_ref[0])

    lp = lam_ref[...]
    lam = (jnp.exp(jnp.sum(lp[0:1] * lp[1:2], axis=1, keepdims=True))
           - jnp.exp(jnp.sum(lp[2:3] * lp[3:4], axis=1, keepdims=True)) + lam_init)
    scale = DK_B ** -0.5
    assert math.frexp(scale)[0] == 0.5
    grp0_q = _iota((q_blk, LANES), 1) < DK_B

    for qi in range(n_tok // q_blk):
        rows = slice(qi * q_blk, (qi + 1) * q_blk)
        q_rows = qn[rows] * scale
        outs = []
        for comp in range(2):
            keep = grp0_q if comp == 0 else jnp.logical_not(grp0_q)
            qm = jnp.where(keep, q_rows, 0.0).astype(BF16)
            s = _dot_nt(qm, kb)
            mx = jnp.max(s, axis=1, keepdims=True)
            if has_ctx:
                sc = _dot_nt(qm, ckb)
                mx = jnp.maximum(mx, jnp.max(sc, axis=1, keepdims=True))
            r = _dot(jnp.exp(s - mx), vb)
            if has_ctx:
                r = r + _dot(jnp.exp(sc - mx), cvb)
            outs.append(r[:, :DV_B] / r[:, DV_B:])
        a = outs[0] - lam * outs[1]
        out_ref[rows, :] = (_rms(a, og_ref[...]) * (1.0 - lam_init)).astype(BF16)


def diff_mixer(z, qn_g, kn_g, lam, out_g, lam_init, ctx, n_seq, n_tok, row_blk0):
    has_ctx = ctx is not None
    col = lambda base: pl.BlockSpec((n_tok, LANES), lambda b, h: (row_blk0 + b, base + h))
    vec = pl.BlockSpec((1, LANES), lambda b, h: (0, 0))
    in_specs = [col(ZB_Q), col(ZB_K), col(ZB_V), vec, vec,
                pl.BlockSpec((4, LANES), lambda b, h: (0, 0)), vec]
    args = [z, z, z, jnp.tile(qn_g, 2).reshape(1, LANES), jnp.tile(kn_g, 2).reshape(1, LANES),
            jnp.pad(lam, ((0, 0), (0, LANES - DK_B))), out_g.reshape(1, LANES)]
    if has_ctx:
        ck, cv = ctx
        s_ctx = ck.shape[1]
        cos_t, sin_t = _rope_tables(n_tok, DK_B)
        tab = pl.BlockSpec((n_tok, LANES), lambda b, h: (0, 0))
        cspec = pl.BlockSpec((1, s_ctx, LANES), lambda b, h: (b, 0, h))
        in_specs += [cspec, cspec, tab, tab]
        args += [ck.reshape(n_seq, s_ctx, H_B * 2 * DK_B), cv.reshape(n_seq, s_ctx, H_B * DV_B), cos_t, sin_t]
    out_shape = [jax.ShapeDtypeStruct((n_seq * n_tok, D_GRP), BF16)]
    out_specs = [pl.BlockSpec((n_tok, LANES), lambda b, h: (b, h))]
    if not has_ctx:
        out_shape.append(jax.ShapeDtypeStruct((n_seq * n_tok, D_GRP), F32))
        out_specs.append(pl.BlockSpec((n_tok, LANES), lambda b, h: (b, h)))
    return pl.pallas_call(
        functools.partial(_diff_kernel, n_tok=n_tok, q_blk=min(n_tok, 256), lam_init=lam_init, has_ctx=has_ctx),
        out_shape=out_shape,
        grid=(n_seq, H_B),
        in_specs=in_specs,
        out_specs=out_specs,
        compiler_params=_cparams(("arbitrary", "arbitrary")),
        name="diff_mixer",
    )(*args)


def _swa_kernel(*refs, n_tok, q_blk, has_ctx):
    refs = list(refs)
    sink_ref, q_ref, k_ref, v_ref, qg_ref, kg_ref = refs[:6]
    pos = 6
    if has_ctx:
        ck_ref, cv_ref, cos_ref, sin_ref = refs[pos:pos + 4]
        pos += 4
    out_ref = refs[pos]
    pos += 1
    if not has_ctx:
        kout_ref = refs[pos]

    kv = pl.program_id(1)
    kn = _rms(k_ref[...], kg_ref[...])
    if has_ctx:
        cos_t, sin_t = cos_ref[...], sin_ref[...]
        kn = _rope(kn, cos_t, sin_t, DH_C)
        ckb = ck_ref[0].astype(BF16)
        cvb = _with_ones(cv_ref[0])
    else:
        kout_ref[...] = kn
    kb = kn.astype(BF16)
    vb = _with_ones(v_ref[...])
    scale = DH_C ** -0.5

    for g in range(G_C):
        sink = sink_ref[kv * G_C + g]
        qn = _rms(q_ref[:, g * DH_C:(g + 1) * DH_C], qg_ref[...])
        if has_ctx:
            qn = _rope(qn, cos_t, sin_t, DH_C)
        for qi in range(n_tok // q_blk):
            rows = slice(qi * q_blk, (qi + 1) * q_blk)
            qb = qn[rows].astype(BF16)
            s = _dot_nt(qb, kb) * scale
            if has_ctx:
                qpos = qi * q_blk + _iota((q_blk, n_tok), 0)
                kpos = _iota((q_blk, n_tok), 1)
                s = jnp.where(jnp.abs(kpos - qpos) <= WINDOW, s, -jnp.inf)
            mx = jnp.maximum(jnp.max(s, axis=1, keepdims=True), sink)
            if has_ctx:
                sc = _dot_nt(qb, ckb) * scale
                mx = jnp.maximum(mx, jnp.max(sc, axis=1, keepdims=True))
            r = _dot(jnp.exp(s - mx), vb)
            if has_ctx:
                r = r + _dot(jnp.exp(sc - mx), cvb)
            den = r[:, DH_C:] + jnp.exp(sink - mx)
            out_ref[rows, g * DH_C:(g + 1) * DH_C] = (r[:, :DH_C] / den).astype(BF16)


def swa_mixer(z, qn_g, kn_g, sink, ctx, n_seq, n_tok, row_blk0):
    has_ctx = ctx is not None
    vec = pl.BlockSpec((1, LANES), lambda b, kv: (0, 0))
    in_specs = [pl.BlockSpec(memory_space=pltpu.SMEM),
                pl.BlockSpec((n_tok, G_C * DH_C), lambda b, kv: (row_blk0 + b, ZC_Q // G_C + kv)),
                pl.BlockSpec((n_tok, LANES), lambda b, kv: (row_blk0 + b, ZC_K + kv)),
                pl.BlockSpec((n_tok, LANES), lambda b, kv: (row_blk0 + b, ZC_V + kv)),
                vec, vec]
    args = [sink.astype(F32), z, z, z, qn_g.reshape(1, LANES), kn_g.reshape(1, LANES)]
    if has_ctx:
        ck, cv = ctx
        s_ctx = ck.shape[1]
        cos_t, sin_t = _rope_tables(n_tok, DH_C)
        tab = pl.BlockSpec((n_tok, LANES), lambda b, kv: (0, 0))
        cspec = pl.BlockSpec((1, s_ctx, LANES), lambda b, kv: (b, 0, kv))
        in_specs += [cspec, cspec, tab, tab]
        args += [ck.reshape(n_seq, s_ctx, KV_C * DH_C), cv.reshape(n_seq, s_ctx, KV_C * DH_C), cos_t, sin_t]
    out_shape = [jax.ShapeDtypeStruct((n_seq * n_tok, D_GRP), BF16)]
    out_specs = [pl.BlockSpec((n_tok, G_C * DH_C), lambda b, kv: (b, kv))]
    if not has_ctx:
        out_shape.append(jax.ShapeDtypeStruct((n_seq * n_tok, KV_C * DH_C), F32))
        out_specs.append(pl.BlockSpec((n_tok, LANES), lambda b, kv: (b, kv)))
    return pl.pallas_call(
        functools.partial(_swa_kernel, n_tok=n_tok, q_blk=min(n_tok, 256), has_ctx=has_ctx),
        out_shape=out_shape,
        grid=(n_seq, KV_C),
        in_specs=in_specs,
        out_specs=out_specs,
        compiler_params=_cparams(("arbitrary", "arbitrary")),
        name="swa_mixer",
    )(*args)


def _dft_matrices(n_tok):
    k = jnp.arange(n_tok, dtype=jnp.int32)[:, None]
    s = jnp.arange(n_tok, dtype=jnp.int32)[None, :]
    ang = (math.pi / n_tok) * ((k * s) % (2 * n_tok)).astype(F32)
    cos, sin = jnp.cos(ang), jnp.sin(ang)
    alt_s = jnp.where(s % 2 == 0, 1.0, -1.0).astype(F32)
    fwd = jnp.concatenate([cos, jnp.where(k == 0, alt_s, -sin)], axis=0)
    inv_re = jnp.where(s == 0, 1.0, 2.0 * cos) / (2 * n_tok)
    alt_t = jnp.where(k % 2 == 0, 1.0, -1.0).astype(F32)
    inv_im = jnp.where(s == 0, alt_t, -2.0 * sin) / (2 * n_tok)
    inv = jnp.concatenate([inv_re, inv_im], axis=1)
    return fwd, inv


def _hyena_filter_kernel(feat_ref, w1_ref, b1_ref, fr_ref, w2_ref, b2_ref, w3_ref, b3_ref, win_ref,
                         fh_ref, fl_ref, g_ref, *, n_tok):
    fr = fr_ref[...]
    h = jnp.sin(fr * (_dot3(feat_ref[...], w1_ref[...]) + b1_ref[...]))
    h = jnp.sin(fr * (_dot3(h, w2_ref[...]) + b2_ref[...]))
    f = _dot3(h, w3_ref[...]) + b3_ref[...]
    win = win_ref[...]
    h_fwd = f[:, :D_GRP] * win
    h_bwd = jnp.where(_iota((n_tok, D_GRP), 0) == 0, 0.0, f[:, D_GRP:] * win)
    fh, fl = fh_ref[...], fl_ref[...]

    def spectrum(x):
        xh, xl = _split2(x)
        return (jnp.dot(fh, xh, preferred_element_type=F32) + jnp.dot(fl, xh, preferred_element_type=F32)
                + jnp.dot(fh, xl, preferred_element_type=F32))

    conj_sign = jnp.where(_iota((2 * n_tok, D_GRP), 0) <= n_tok, 1.0, -1.0)
    g_ref[0] = spectrum(h_fwd) + conj_sign * spectrum(h_bwd)


def hyena_filter_spectra(n_tok, w1, b1, freq, w2, b2, w3, b3, fwd_hi, fwd_lo):
    t = jnp.linspace(0.0, 1.0, n_tok, dtype=F32)[:, None]
    w = (2.0 * math.pi / n_tok) * jnp.arange(n_tok, dtype=F32)[:, None]
    bands = jnp.linspace(1e-4, HY_BANDS - 1, HY_BANDS, dtype=F32)[None, :]
    feats = jnp.concatenate([t, jnp.cos(bands * w), -jnp.sin(bands * w)], axis=-1)
    feats = jnp.pad(feats, ((0, 0), (0, LANES - HY_POS_DIM)))
    rates = jnp.abs(jnp.linspace(HY_FAST, HY_SLOW, D_GRP, dtype=F32))
    window = jnp.exp(-t * rates)
    padv = lambda a: jnp.pad(a, (0, LANES - HY_FF)).reshape(1, LANES)
    w1p = jnp.pad(w1, ((0, LANES - HY_POS_DIM), (0, LANES - HY_FF)))
    w2p = jnp.pad(w2, ((0, LANES - HY_FF), (0, LANES - HY_FF)))
    w3p = jnp.pad(w3, ((0, LANES - HY_FF), (0, 0)))
    full = lambda shape: pl.BlockSpec(shape, lambda o: (0,) * len(shape))
    return pl.pallas_call(
        functools.partial(_hyena_filter_kernel, n_tok=n_tok),
        out_shape=jax.ShapeDtypeStruct((HY_ORDER, 2 * n_tok, D_GRP), F32),
        grid=(HY_ORDER,),
        in_specs=[full((n_tok, LANES)), full((LANES, LANES)), full((1, LANES)), full((1, LANES)),
                  full((LANES, LANES)), full((1, LANES)),
                  pl.BlockSpec((LANES, 2 * D_GRP), lambda o: (0, o)),
                  pl.BlockSpec((1, 2 * D_GRP), lambda o: (0, o)),
                  full((n_tok, D_GRP)), full((2 * n_tok, n_tok)), full((2 * n_tok, n_tok))],
        out_specs=pl.BlockSpec((1, 2 * n_tok, D_GRP), lambda o: (o, 0, 0)),
        compiler_params=_cparams(("arbitrary",)),
        name="hyena_filter_spectra",
    )(feats, w1p, padv(b1), padv(freq), w2p, padv(b2), w3p, b3.reshape(1, -1), window, fwd_hi, fwd_lo)


def _hyena_conv_kernel(zy_ref, zx1_ref, zx2_ref, cwy_ref, cw1_ref, cw2_ref, cby_ref, cb1_ref, cb2_ref,
                       skip_ref, g_ref, fwd_ref, inv_ref, out_ref, *, n_tok):
    width = zy_ref.shape[1]
    row = _iota((n_tok, width), 0)
    first, last = row == 0, row == n_tok - 1

    def dwconv(x_ref, w_ref, b_ref):
        x = x_ref[...]
        w = w_ref[...]
        prev = jnp.where(first, 0.0, pltpu.roll(x, 1, 0))
        nxt = jnp.where(last, 0.0, pltpu.roll(x, n_tok - 1, 0))
        return prev * w[0:1] + x * w[1:2] + nxt * w[2:3] + b_ref[...]

    y = dwconv(zy_ref, cwy_ref, cby_ref)
    gates = (dwconv(zx1_ref, cw1_ref, cb1_ref), dwconv(zx2_ref, cw2_ref, cb2_ref))
    fwd = fwd_ref[...]
    for o in range(HY_ORDER):
        u = jnp.dot(fwd, y.astype(BF16), preferred_element_type=F32)
        a, b = u[:n_tok], u[n_tok:]
        c, d = g_ref[o, :n_tok, :], g_ref[o, n_tok:, :]
        bd = b * d
        y_re = a * c - jnp.where(first, 0.0, bd)
        y_im = jnp.where(first, bd, a * d + b * c)
        conv = (jnp.dot(inv_ref[:, :n_tok], y_re.astype(BF16), preferred_element_type=F32)
                + jnp.dot(inv_ref[:, n_tok:], y_im.astype(BF16), preferred_element_type=F32))
        y = gates[o] * (conv + skip_ref[o:o + 1, :] * y)
    out_ref[...] = y.astype(BF16)


def hyena_mixer(z, conv_w, conv_b, skip, spectra, fwd_hi, inv_hi, n_seq, n_tok, row_blk0):
    width = 256
    nw = D_GRP // width
    zcol = lambda base: pl.BlockSpec((n_tok, width), lambda b, j: (row_blk0 + b, base * LANES // width + j))
    wcol = lambda part: pl.BlockSpec((3, width), lambda b, j: (0, part * nw + j))
    bcol = lambda part: pl.BlockSpec((1, width), lambda b, j: (0, part * nw + j))
    cb = conv_b.reshape(1, 3 * D_GRP)
    in_specs = [zcol(ZD_Y), zcol(ZD_X1), zcol(ZD_X2), wcol(0), wcol(1), wcol(2), bcol(0), bcol(1), bcol(2),
                pl.BlockSpec((HY_ORDER, width), lambda b, j: (0, j)),
                pl.BlockSpec((HY_ORDER, 2 * n_tok, width), lambda b, j: (0, 0, j)),
                pl.BlockSpec((2 * n_tok, n_tok), lambda b, j: (0, 0)),
                pl.BlockSpec((n_tok, 2 * n_tok), lambda b, j: (0, 0))]
    args = [z, z, z, conv_w, conv_w, conv_w, cb, cb, cb, skip, spectra, fwd_hi, inv_hi]
    return pl.pallas_call(
        functools.partial(_hyena_conv_kernel, n_tok=n_tok),
        out_shape=jax.ShapeDtypeStruct((n_seq * n_tok, D_GRP), BF16),
        grid=(n_seq, nw),
        in_specs=in_specs,
        out_specs=pl.BlockSpec((n_tok, width), lambda b, j: (b, j)),
        compiler_params=_cparams(("arbitrary", "arbitrary")),
        name="hyena_mixer",
    )(*args)


def _outproj_kernel(*refs, n_ctx_blk):
    ctx_parts, dec_parts = refs[0:N_MIXERS], refs[N_MIXERS:2 * N_MIXERS]
    w_ref, x_ref, g1_ref, ng_ref, sc_ref, sh_ref, x1_ref, h2_ref = refs[2 * N_MIXERS:]
    is_ctx = pl.program_id(0) < n_ctx_blk
    acc = None
    for g in range(N_MIXERS):
        part = jnp.where(is_ctx, ctx_parts[g][...], dec_parts[g][...])
        term = jnp.dot(part, w_ref[g * D_GRP:(g + 1) * D_GRP, :], preferred_element_type=F32)
        acc = term if acc is None else acc + term
    x1 = x_ref[...] + g1_ref[0] * acc
    x1_ref[...] = x1
    h2_ref[...] = _ada_norm(x1, ng_ref[...], sc_ref[0], sh_ref[0]).astype(BF16)


def out_projection(ctx_parts, dec_parts, w_out_bf16, x, mod, layer, norm_g, n_ctx_tok, dec_seq):
    m = x.shape[0]
    tm = 512
    n_ctx_blk = n_ctx_tok // tm
    modspec = lambda part: pl.BlockSpec((1, 1, D_MODEL), _mod_row_index(layer, part, tm, n_ctx_tok, dec_seq))
    ctx_part = pl.BlockSpec((tm, D_GRP), lambda i: (jnp.minimum(i, n_ctx_blk - 1), 0))
    dec_part = pl.BlockSpec((tm, D_GRP), lambda i: (jnp.maximum(i - n_ctx_blk, 0), 0))
    row = pl.BlockSpec((tm, D_MODEL), lambda i: (i, 0))
    return pl.pallas_call(
        functools.partial(_outproj_kernel, n_ctx_blk=n_ctx_blk),
        out_shape=[jax.ShapeDtypeStruct((m, D_MODEL), F32), jax.ShapeDtypeStruct((m, D_MODEL), BF16)],
        grid=(m // tm,),
        in_specs=[ctx_part] * N_MIXERS + [dec_part] * N_MIXERS
        + [pl.BlockSpec((D_MODEL, D_MODEL), lambda i: (0, 0)), row,
           modspec(2), pl.BlockSpec((1, D_MODEL), lambda i: (0, 0)), modspec(4), modspec(3)],
        out_specs=[row, row],
        compiler_params=_cparams(("arbitrary",)),
        name="out_projection",
    )(*ctx_parts, *dec_parts, w_out_bf16, x, mod, norm_g.reshape(1, D_MODEL), mod, mod)


def _ffn_kernel(h_ref, wa_ref, wb_ref, cwa_ref, cwb_ref, cba_ref, cbb_ref, wd_ref, y_ref,
                ua0_ref, ub0_ref, ua1_ref, ub1_ref, *, tile_rows, n_ctx_tok, ctx_seq, dec_seq):
    i = pl.program_id(0)
    j = pl.program_id(1)
    tf = wd_ref.shape[1]
    body = slice(SUBLANES, SUBLANES + tile_rows)

    @pl.when((i == 0) & (j == 0))
    def _():
        for ref in (ua0_ref, ub0_ref, ua1_ref, ub1_ref):
            ref[...] = jnp.zeros_like(ref)

    half = tile_rows // 2

    def park(dst, w_ref):
        dst[body, :] = jnp.dot(h_ref[...], w_ref[0].astype(BF16), preferred_element_type=F32)

    def gated(ready_a, ready_b, r0, seq):
        lo = SUBLANES + r0
        edges = seq != tile_rows
        if edges:
            pos = (r0 + _iota((half, tf), 0)) & (seq - 1)
            first, last = pos == 0, pos == seq - 1

        def dwconv(u_ref, cw_ref, cb_ref):
            cw = cw_ref[0].astype(BF16)
            prev = u_ref[lo - 1:lo - 1 + half, :]
            nxt = u_ref[lo + 1:lo + 1 + half, :]
            if edges:
                prev = jnp.where(first, 0.0, prev)
                nxt = jnp.where(last, 0.0, nxt)
            prev, cur, nxt = prev.astype(BF16), u_ref[lo:lo + half, :].astype(BF16), nxt.astype(BF16)
            return prev * cw[0:1] + cur * cw[1:2] + nxt * cw[2:3] + cb_ref[0].astype(BF16)

        ua = dwconv(ready_a, cwa_ref, cba_ref)
        ub = dwconv(ready_b, cwb_ref, cbb_ref)
        return ua * _sigmoid(ua) * ub

    def step(park_a, park_b, ready_a, ready_b, first_finish, seq):
        wd = wd_ref[0].astype(BF16)
        for r0, dst, w_ref in ((0, park_a, wa_ref), (half, park_b, wb_ref)):
            park(dst, w_ref)
            contrib = jnp.dot(gated(ready_a, ready_b, r0, seq), wd, preferred_element_type=F32)
            if first_finish:
                y_ref[r0:r0 + half, :] = contrib
            else:
                y_ref[r0:r0 + half, :] += contrib

    @pl.when(j == 0)
    def _():
        park(ua0_ref, wa_ref)
        park(ub0_ref, wb_ref)

    is_ctx = i * tile_rows < n_ctx_tok
    for in_group, seq in ((is_ctx, ctx_seq), (jnp.logical_not(is_ctx), dec_seq)):
        @pl.when(in_group & (j == 1))
        def _():
            step(ua1_ref, ub1_ref, ua0_ref, ub0_ref, True, seq)

        @pl.when(in_group & (j > 1) & (j % 2 == 0))
        def _():
            step(ua0_ref, ub0_ref, ua1_ref, ub1_ref, False, seq)

        @pl.when(in_group & (j > 1) & (j % 2 == 1))
        def _():
            step(ua1_ref, ub1_ref, ua0_ref, ub0_ref, False, seq)


def conv_ffn(h2, layer, w_up, conv_w, conv_b, w_down, n_ctx_tok, ctx_seq, dec_seq):
    m = h2.shape[0]
    tm, tf = 1024, FFN_TF
    nf = D_FF // tf
    for seq in (ctx_seq, dec_seq):
        assert tm % seq == 0 and seq & (seq - 1) == 0, seq
    assert n_ctx_tok % tm == 0 and m % tm == 0
    cb = conv_b.reshape(DEPTH, 1, 2 * D_FF)
    up = lambda j: jnp.minimum(j, nf - 1)
    dn = lambda j: jnp.maximum(j - 1, 0)
    park = pltpu.VMEM((tm + 2 * SUBLANES, tf), F32)
    return pl.pallas_call(
        functools.partial(_ffn_kernel, tile_rows=tm, n_ctx_tok=n_ctx_tok, ctx_seq=ctx_seq, dec_seq=dec_seq),
        out_shape=jax.ShapeDtypeStruct((m, D_MODEL), F32),
        grid=(m // tm, nf + 1),
        in_specs=[pl.BlockSpec((tm, D_MODEL), lambda i, j: (i, 0)),
                  pl.BlockSpec((1, D_MODEL, tf), lambda i, j: (layer, 0, up(j))),
                  pl.BlockSpec((1, D_MODEL, tf), lambda i, j: (layer, 0, nf + up(j))),
                  pl.BlockSpec((1, 3, tf), lambda i, j: (layer, 0, dn(j))),
                  pl.BlockSpec((1, 3, tf), lambda i, j: (layer, 0, nf + dn(j))),
                  pl.BlockSpec((1, 1, tf), lambda i, j: (layer, 0, dn(j))),
                  pl.BlockSpec((1, 1, tf), lambda i, j: (layer, 0, nf + dn(j))),
                  pl.BlockSpec((1, tf, D_MODEL), lambda i, j: (layer, dn(j), 0))],
        out_specs=pl.BlockSpec((tm, D_MODEL), lambda i, j: (i, 0)),
        scratch_shapes=[park, park, park, park],
        compiler_params=_cparams(("arbitrary", "arbitrary")),
        name="conv_ffn",
    )(h2, w_up, w_up, conv_w, conv_w, cb, cb, w_down)


def _residual_kernel(x_ref, y_ref, g_ref, o_ref):
    o_ref[...] = x_ref[...] + g_ref[0] * y_ref[...]


def final_residual(x1, y, mod, layer, row0, n_rows, n_ctx_tok, dec_seq):
    tm = 512
    blk0 = row0 // tm

    def mod_index(i):
        start = (blk0 + i) * tm
        row = jnp.where(start < n_ctx_tok, 0, 1 + (start - n_ctx_tok) // dec_seq)
        return (layer * 48 + row * 6 + 5, 0, 0)

    return pl.pallas_call(
        _residual_kernel,
        out_shape=jax.ShapeDtypeStruct((n_rows, D_MODEL), F32),
        grid=(n_rows // tm,),
        in_specs=[pl.BlockSpec((tm, D_MODEL), lambda i: (blk0 + i, 0)),
                  pl.BlockSpec((tm, D_MODEL), lambda i: (blk0 + i, 0)),
                  pl.BlockSpec((1, 1, D_MODEL), mod_index)],
        out_specs=pl.BlockSpec((tm, D_MODEL), lambda i: (i, 0)),
        compiler_params=_cparams(("arbitrary",)),
        name="final_residual",
    )(x1, y, mod)


def kernel(x_prompt, x_sample, cache_diff_k, cache_diff_v, cache_swa_k, cache_swa_v, state_mlstm_C, state_mlstm_n, state_mlstm_m, c, c_ctx, w_mod, b_mod, norm1_g, norm2_g, w_in, mlstm_ig_b, mlstm_fg_b, mlstm_norm_g, diff_qn_g, diff_kn_g, diff_lam, diff_out_g, swa_qn_g, swa_kn_g, swa_sink, hy_conv_w, hy_conv_b, hy_w1, hy_b1, hy_freq, hy_w2, hy_b2, hy_w3, hy_b3, hy_bias, w_out, ffn_w_up, ffn_conv_w, ffn_conv_b, ffn_w_down):
    n_ctx, ctx_seq = x_prompt.shape[:2]
    n_dec, dec_seq = x_sample.shape[:2]
    n_ctx_tok = n_ctx * ctx_seq
    n_dec_tok = n_dec * dec_seq
    dec_blk0 = n_ctx_tok // dec_seq

    x = (x_prompt.reshape(n_ctx_tok, D_MODEL), x_sample.reshape(n_dec_tok, D_MODEL))
    w_pack, wg_hi, wg_lo = pack_in_weights(w_in)
    cond8 = jnp.concatenate([c_ctx[None, :], c, jnp.zeros((8 - 1 - n_dec, D_MODEL), F32)], axis=0)
    mod = mod_vectors(cond8, w_mod, b_mod)

    dft = {}
    for t in (ctx_seq, dec_seq):
        fwd, inv = _dft_matrices(t)
        dft[t] = dict(fwd=fwd.astype(BF16), inv=inv.astype(BF16), split=_split2(fwd))

    delta = None
    caches = []
    for l in range(DEPTH):
        x, z, gates = in_projection(x, delta, mod, l, norm1_g[l], w_pack, wg_hi, wg_lo, n_ctx_tok, dec_seq)

        gate_bias = jnp.pad(jnp.concatenate([mlstm_ig_b[l].reshape(-1), mlstm_fg_b[l].reshape(-1)]),
                            (0, LANES - 4 * H_A)).reshape(1, LANES)
        lam_init = 0.8 - 0.6 * math.exp(-0.3 * l)

        a_ctx, c_new, n_new, m_new = mlstm_mixer(z, gates, gate_bias, mlstm_norm_g[l], None,
                                                 n_ctx, ctx_seq, 0, ctx_seq, True)
        b_ctx, diff_k = diff_mixer(z, diff_qn_g[l], diff_kn_g[l], diff_lam[l], diff_out_g[l], lam_init,
                                   None, n_ctx, ctx_seq, 0)
        c_ctx_out, swa_k = swa_mixer(z, swa_qn_g[l], swa_kn_g[l], swa_sink[l], None, n_ctx, ctx_seq, 0)
        spec_ctx = hyena_filter_spectra(ctx_seq, hy_w1[l], hy_b1[l], hy_freq[l], hy_w2[l], hy_b2[l],
                                        hy_w3[l], hy_b3[l], *dft[ctx_seq]["split"])
        d_ctx = hyena_mixer(z, hy_conv_w[l], hy_conv_b[l], hy_bias[l], spec_ctx, dft[ctx_seq]["fwd"],
                            dft[ctx_seq]["inv"], n_ctx, ctx_seq, 0)
        a_dec, = mlstm_mixer(z, gates, gate_bias, mlstm_norm_g[l],
                             (state_mlstm_C[:, l], state_mlstm_n[:, l], state_mlstm_m[:, l]),
                             n_dec, dec_seq, dec_blk0, 512, False)
        b_dec, = diff_mixer(z, diff_qn_g[l], diff_kn_g[l], diff_lam[l], diff_out_g[l], lam_init,
                            (cache_diff_k[:, l], cache_diff_v[:, l]), n_dec, dec_seq, dec_blk0)
        c_dec, = swa_mixer(z, swa_qn_g[l], swa_kn_g[l], swa_sink[l],
                           (cache_swa_k[:, l], cache_swa_v[:, l]), n_dec, dec_seq, dec_blk0)
        spec_dec = hyena_filter_spectra(dec_seq, hy_w1[l], hy_b1[l], hy_freq[l], hy_w2[l], hy_b2[l],
                                        hy_w3[l], hy_b3[l], *dft[dec_seq]["split"])
        d_dec = hyena_mixer(z, hy_conv_w[l], hy_conv_b[l], hy_bias[l], spec_dec, dft[dec_seq]["fwd"],
                            dft[dec_seq]["inv"], n_dec, dec_seq, dec_blk0)

        x, h2 = out_projection((a_ctx, b_ctx, c_ctx_out, d_ctx), (a_dec, b_dec, c_dec, d_dec),
                               w_out[l].astype(BF16), x, mod, l, norm2_g[l], n_ctx_tok, dec_seq)
        delta = conv_ffn(h2, l, ffn_w_up, ffn_conv_w, ffn_conv_b, ffn_w_down, n_ctx_tok, ctx_seq, dec_seq)

        zc = z[:n_ctx_tok]
        caches.append((
            diff_k.reshape(n_ctx, ctx_seq, H_B, 2, DK_B),
            zc[:, ZB_V * LANES:(ZB_V + 4) * LANES].reshape(n_ctx, ctx_seq, H_B, DV_B),
            swa_k.reshape(n_ctx, ctx_seq, KV_C, DH_C),
            zc[:, ZC_V * LANES:(ZC_V + 2) * LANES].reshape(n_ctx, ctx_seq, KV_C, DH_C),
            c_new,
            n_new.reshape(n_ctx, 2, H_A, DH_A),
            m_new.reshape(n_ctx, 2, H_A),
        ))

    y_prompt = final_residual(x, delta, mod, DEPTH - 1, 0, n_ctx_tok, n_ctx_tok, dec_seq)
    y_sample = final_residual(x, delta, mod, DEPTH - 1, n_ctx_tok, n_dec_tok, n_ctx_tok, dec_seq)
    outs = [jnp.stack([s[i] for s in caches], axis=1) for i in range(7)]
    return (y_prompt.reshape(n_ctx, ctx_seq, D_MODEL), y_sample.reshape(n_dec, dec_seq, D_MODEL), *outs)
```
